```python
import math
import jax
import jax.numpy as jnp
from jax import lax
import numpy as np

D_MODEL = 1024
BATCH = 4
SEQ = 8192
DEPTH = 1

N_META = 16
ATT_HEAD_DIM = 64
ATT_HEADS = D_MODEL // 128
ATT_WIDTH = ATT_HEADS * ATT_HEAD_DIM
Q_BLOCK = 128
SSM_WIDTH = D_MODEL // 2
SSM_GROUP_CH = 16
SSM_GROUPS = SSM_WIDTH // SSM_GROUP_CH
SSM_STATE = 64
DT_MIN = 0.001
DT_MAX = 0.1
IN_COLS = 3 * ATT_WIDTH + SSM_WIDTH + 2 * D_MODEL
MOE_GROUPS = 4
EXPERTS_PER_GROUP = 8
N_EXPERTS = MOE_GROUPS * EXPERTS_PER_GROUP
INNER_TOP_K = 2
EXPERT_FF = D_MODEL // 2
MOE_BLOCK = 128

RMS_EPS = 1e-6
NEG_BIG = -1e30

kernel_name = "hybrid_stickbreak_s5_hmoe_block"


def rms_norm(x, g):
    x32 = x.astype(jnp.float32)
    y = x32 * lax.rsqrt(jnp.mean(x32 * x32, axis=-1, keepdims=True) + RMS_EPS)
    return (y * g.astype(jnp.float32)).astype(x.dtype)


def stick_breaking_attention(q, k, v):
    b, l, _ = q.shape
    pad = (-N_META) % Q_BLOCK
    lp = l + pad

    def heads(t):
        t = jnp.pad(t, ((0, 0), (pad, 0), (0, 0)))
        return t.reshape(b, lp, ATT_HEADS, ATT_HEAD_DIM).transpose(0, 2, 1, 3)

    qh, kh, vh = heads(q), heads(k), heads(v)
    n_blocks = lp // Q_BLOCK
    key_pos = jnp.arange(lp)
    scale = ATT_HEAD_DIM ** -0.5

    def one_block(i):
        start = i * Q_BLOCK
        qb = lax.dynamic_slice_in_dim(qh, start, Q_BLOCK, axis=2)
        z = jnp.einsum('bhqd,bhkd->bhqk', qb, kh).astype(jnp.float32) * scale
        q_pos = start + jnp.arange(Q_BLOCK)
        valid = (key_pos[None, :] < q_pos[:, None]) & (key_pos[None, :] >= pad)
        log_beta = jnp.where(valid, jax.nn.log_sigmoid(z), NEG_BIG)
        log_keep = jnp.where(valid, jax.nn.log_sigmoid(-z), 0.0)
        tail = lax.cumsum(log_keep, axis=3, reverse=True)
        log_stick = jnp.concatenate([tail[..., 1:], jnp.zeros_like(tail[..., :1])], axis=-1)
        w = jnp.exp(log_beta + log_stick)
        return jnp.einsum('bhqk,bhkd->bhqd', w.astype(vh.dtype), vh)

    out = lax.map(one_block, jnp.arange(n_blocks))
    out = out.transpose(1, 0, 3, 2, 4).reshape(b, lp, ATT_WIDTH)
    return out[:, pad:]


def s5_ssm(u, lam_re, lam_im, log_dt, b_re, b_im, c_re, c_im, d_skip):
    b, l, _ = u.shape
    f32 = jnp.float32
    ug = u.astype(f32).reshape(b, l, SSM_GROUPS, SSM_GROUP_CH)
    lr, li = lam_re.astype(f32), lam_im.astype(f32)
    dt = jnp.exp(log_dt.astype(f32))[:, None]
    mag = jnp.exp(lr * dt)
    lb_re, lb_im = mag * jnp.cos(li * dt), mag * jnp.sin(li * dt)
    nr, ni = lb_re - 1.0, lb_im
    den = lr * lr + li * li
    f_re = (nr * lr + ni * li) / den
    f_im = (ni * lr - nr * li) / den
    br, bi = b_re.astype(f32), b_im.astype(f32)
    bb_re = f_re[:, :, None] * br - f_im[:, :, None] * bi
    bb_im = f_re[:, :, None] * bi + f_im[:, :, None] * br
    bu_re = jnp.einsum('blgc,gpc->lbgp', ug, bb_re)
    bu_im = jnp.einsum('blgc,gpc->lbgp', ug, bb_im)
    a_re = jnp.broadcast_to(lb_re, (l,) + lb_re.shape)
    a_im = jnp.broadcast_to(lb_im, (l,) + lb_im.shape)

    def combine(e1, e2):
        a1r, a1i, b1r, b1i = e1
        a2r, a2i, b2r, b2i = e2
        ar = a2r * a1r - a2i * a1i
        ai = a2r * a1i + a2i * a1r
        a2r_b, a2i_b = a2r[:, None], a2i[:, None]
        new_br = a2r_b * b1r - a2i_b * b1i + b2r
        new_bi = a2r_b * b1i + a2i_b * b1r + b2i
        return ar, ai, new_br, new_bi

    _, _, x_re, x_im = lax.associative_scan(combine, (a_re, a_im, bu_re, bu_im), axis=0)
    y = (jnp.einsum('lbgp,gcp->blgc', x_re, c_re.astype(f32))
         - jnp.einsum('lbgp,gcp->blgc', x_im, c_im.astype(f32))
         + d_skip.astype(f32).reshape(SSM_GROUPS, SSM_GROUP_CH) * ug)
    return y.reshape(b, l, SSM_WIDTH)


def hierarchical_moe(x, rg_w, rg_b, re_w, re_b, w1, w3, w2):
    b, l, d = x.shape
    t = b * l
    xt = x.reshape(t, d)
    group_probs = jax.nn.softmax((xt @ rg_w).astype(jnp.float32) + rg_b.astype(jnp.float32), axis=-1)
    p_top, g_top = lax.top_k(group_probs, 1)
    e_logits = ((xt @ re_w).astype(jnp.float32) + re_b.astype(jnp.float32)).reshape(t, MOE_GROUPS, EXPERTS_PER_GROUP)
    sel = jnp.take_along_axis(e_logits, g_top[:, :, None], axis=1)[:, 0]
    e_val, e_idx = lax.top_k(sel, INNER_TOP_K)
    gate = jax.nn.softmax(e_val, axis=-1) * p_top
    expert_id = g_top * EXPERTS_PER_GROUP + e_idx

    n = t * INNER_TOP_K
    n_blocks = (n + N_EXPERTS * (MOE_BLOCK - 1)) // MOE_BLOCK + 1
    cap = n_blocks * MOE_BLOCK
    e_flat = expert_id.reshape(n).astype(jnp.int32)
    tok_flat = jnp.repeat(jnp.arange(t, dtype=jnp.int32), INNER_TOP_K)
    w_flat = gate.reshape(n)
    order = jnp.argsort(e_flat)
    e_s, t_s, w_s = e_flat[order], tok_flat[order], w_flat[order]
    counts = jnp.bincount(e_flat, length=N_EXPERTS)
    padded = ((counts + MOE_BLOCK - 1) // MOE_BLOCK) * MOE_BLOCK
    start = jnp.cumsum(counts) - counts
    pend = jnp.cumsum(padded)
    pstart = pend - padded
    dest = pstart[e_s] + (jnp.arange(n, dtype=jnp.int32) - start[e_s])
    slot_tok = jnp.full((cap,), t, dtype=jnp.int32).at[dest].set(t_s)
    slot_w = jnp.zeros((cap,), jnp.float32).at[dest].set(w_s)
    block_e = jnp.minimum(jnp.searchsorted(pend, jnp.arange(n_blocks) * MOE_BLOCK, side='right'), N_EXPERTS - 1)
    x_pad = jnp.concatenate([xt, jnp.zeros((1, d), xt.dtype)], axis=0)
    xb = x_pad[slot_tok].reshape(n_blocks, MOE_BLOCK, d)

    def expert_block(args):
        xi, e = args
        hdn = jax.nn.silu(xi @ w1[e]) * (xi @ w3[e])
        return hdn @ w2[e]

    yb = lax.map(expert_block, (xb, block_e)).reshape(cap, d)
    y = jax.ops.segment_sum(yb * slot_w[:, None].astype(yb.dtype), slot_tok, num_segments=t + 1)[:t]
    return y.reshape(b, l, d)


def setup_inputs(seed: int = 0) -> dict:
    key = jax.random.key(seed)
    ks = jax.random.split(key, 32)
    nrm = jax.random.normal
    f32 = jnp.float32
    D, G, P, C, S = D_MODEL, SSM_GROUPS, SSM_STATE, SSM_GROUP_CH, SSM_WIDTH
    lam_im0 = math.pi * jnp.arange(P, dtype=f32)
    return {
        "x": nrm(ks[0], (BATCH, SEQ, D), f32),
        "meta_tokens": nrm(ks[1], (N_META, D), f32),
        "norm_mix_g": 1.0 + 0.01 * nrm(ks[2], (DEPTH, D), f32),
        "w_in": nrm(ks[3], (DEPTH, D, IN_COLS), f32) * D ** -0.5,
        "ssm_lambda_re": -0.5 + 0.01 * nrm(ks[4], (DEPTH, G, P), f32),
        "ssm_lambda_im": lam_im0 + 0.01 * nrm(ks[5], (DEPTH, G, P), f32),
        "ssm_log_dt": jax.random.uniform(ks[6], (DEPTH, G), f32, math.log(DT_MIN), math.log(DT_MAX)),
        "ssm_b_re": nrm(ks[7], (DEPTH, G, P, C), f32) * (2 * C) ** -0.5,
        "ssm_b_im": nrm(ks[8], (DEPTH, G, P, C), f32) * (2 * C) ** -0.5,
        "ssm_c_re": nrm(ks[9], (DEPTH, G, C, P), f32) * P ** -0.5,
        "ssm_c_im": nrm(ks[10], (DEPTH, G, C, P), f32) * P ** -0.5,
        "ssm_d": nrm(ks[11], (DEPTH, S), f32),
        "ssm_glu_w": nrm(ks[12], (DEPTH, S, S), f32) * S ** -0.5,
        "ssm_glu_b": 0.01 * nrm(ks[13], (DEPTH, S), f32),
        "w_branch_attn": nrm(ks[14], (DEPTH, ATT_WIDTH, D), f32) * ATT_WIDTH ** -0.5,
        "w_branch_ssm": nrm(ks[15], (DEPTH, S, D), f32) * S ** -0.5,
        "w_out": nrm(ks[16], (DEPTH, D, D), f32) * D ** -0.5,
        "norm_ffn_g": 1.0 + 0.01 * nrm(ks[17], (DEPTH, D), f32),
        "router_group_w": nrm(ks[18], (DEPTH, D, MOE_GROUPS), f32) * D ** -0.5,
        "router_group_b": 0.01 * nrm(ks[19], (DEPTH, MOE_GROUPS), f32),
        "router_expert_w": nrm(ks[20], (DEPTH, D, N_EXPERTS), f32) * D ** -0.5,
        "router_expert_b": 0.01 * nrm(ks[21], (DEPTH, N_EXPERTS), f32),
        "expert_w1": nrm(ks[22], (DEPTH, N_EXPERTS, D, EXPERT_FF), f32) * D ** -0.5,
        "expert_w3": nrm(ks[23], (DEPTH, N_EXPERTS, D, EXPERT_FF), f32) * D ** -0.5,
        "expert_w2": nrm(ks[24], (DEPTH, N_EXPERTS, EXPERT_FF, D), f32) * EXPERT_FF ** -0.5,
        "norm_final_g": 1.0 + 0.01 * nrm(ks[25], (D,), f32),
    }


def reference(x, meta_tokens, norm_mix_g, w_in, ssm_lambda_re, ssm_lambda_im, ssm_log_dt,
              ssm_b_re, ssm_b_im, ssm_c_re, ssm_c_im, ssm_d, ssm_glu_w, ssm_glu_b,
              w_branch_attn, w_branch_ssm, w_out, norm_ffn_g, router_group_w, router_group_b,
              router_expert_w, router_expert_b, expert_w1, expert_w3, expert_w2, norm_final_g):
    b = x.shape[0]
    meta = jnp.broadcast_to(meta_tokens.astype(x.dtype)[None], (b, N_META, D_MODEL))
    h = jnp.concatenate([meta, x], axis=1)
    splits = np.cumsum([ATT_WIDTH, ATT_WIDTH, ATT_WIDTH, SSM_WIDTH, D_MODEL]).tolist()
    for layer in range(DEPTH):
        xn = rms_norm(h, norm_mix_g[layer])
        proj = xn @ w_in[layer]
        q, k, v, u, ga, gb = jnp.split(proj, splits, axis=-1)
        attn = stick_breaking_attention(q, k, v)
        y = jax.nn.gelu(s5_ssm(u, ssm_lambda_re[layer], ssm_lambda_im[layer], ssm_log_dt[layer],
                               ssm_b_re[layer], ssm_b_im[layer], ssm_c_re[layer], ssm_c_im[layer],
                               ssm_d[layer]))
        ssm = (y * jax.nn.sigmoid(y @ ssm_glu_w[layer].astype(jnp.float32)
                                  + ssm_glu_b[layer].astype(jnp.float32))).astype(x.dtype)
        merged = (jax.nn.sigmoid(ga) * (attn @ w_branch_attn[layer])
                  + jax.nn.sigmoid(gb) * (ssm @ w_branch_ssm[layer]))
        h = h + merged @ w_out[layer]
        hn = rms_norm(h, norm_ffn_g[layer])
        h = h + hierarchical_moe(hn, router_group_w[layer], router_group_b[layer],
                                 router_expert_w[layer], router_expert_b[layer],
                                 expert_w1[layer], expert_w3[layer], expert_w2[layer])
    return rms_norm(h, norm_final_g)[:, N_META:]
```

```python
import functools
import math

import jax
import jax.numpy as jnp
from jax import lax
from jax.experimental import pallas as pl
from jax.experimental.pallas import tpu as pltpu

F32 = jnp.float32
BF16 = jnp.bfloat16

N_META = 16
Q_BLOCK = 128
HEAD_DIM = 64
SSM_GROUP_CH = 16
SSM_STATE = 64
MOE_GROUPS = 4
EXPERTS_PER_GROUP = 8
N_EXPERTS = MOE_GROUPS * EXPERTS_PER_GROUP
RMS_EPS = 1e-6
DT_MIN = 0.001

LANES = 128
SSM_CHUNK = 64
MOE_TILE = 256
STICK_CUTOFF = 104.0
VMEM_LIMIT = 48 * 1024 * 1024


def _cparams(sem):
    return pltpu.CompilerParams(dimension_semantics=sem, vmem_limit_bytes=VMEM_LIMIT)


def _inproj_kernel(x_ref, g_ref, w_ref, q_ref, k_ref, v_ref, u_ref, ga_ref, gb_ref, *, aw, sw, d):
    x = x_ref[...]
    ms = jnp.mean(x * x, axis=-1, keepdims=True)
    xn = (x * lax.rsqrt(ms + RMS_EPS) * g_ref[...]).astype(BF16)

    def proj(lo, hi):
        return jnp.dot(xn, w_ref[:, lo:hi], preferred_element_type=F32)

    q_ref[...] = (proj(0, aw) * (HEAD_DIM ** -0.5)).astype(BF16)
    k_ref[...] = proj(aw, 2 * aw).astype(BF16)
    v_ref[...] = proj(2 * aw, 3 * aw).astype(BF16)
    o = 3 * aw
    u_ref[...] = proj(o, o + sw).astype(BF16)
    o += sw
    ga_ref[...] = jax.nn.sigmoid(proj(o, o + d)).astype(BF16)
    gb_ref[...] = jax.nn.sigmoid(proj(o + d, o + 2 * d)).astype(BF16)


def _inproj(x2, g, w_bf, aw, sw, tm):
    m, d = x2.shape
    cols = w_bf.shape[1]
    row = lambda n: pl.BlockSpec((tm, n), lambda i: (i, 0))
    return pl.pallas_call(
        functools.partial(_inproj_kernel, aw=aw, sw=sw, d=d),
        grid=(m // tm,),
        in_specs=[row(d), pl.BlockSpec((1, d), lambda i: (0, 0)), pl.BlockSpec((d, cols), lambda i: (0, 0))],
        out_specs=[row(aw), row(aw), row(aw), row(sw), row(d), row(d)],
        out_shape=[jax.ShapeDtypeStruct((m, n), BF16) for n in (aw, aw, aw, sw, d, d)],
        compiler_params=_cparams(("parallel",)),
        name="inproj",
    )(x2, g, w_bf)


def _attn_kernel(q_ref, kt_ref, v_ref, tri_ref, o_ref, carry_ref, acc_ref, *, heads, pad):
    i = pl.program_id(1)
    carry_ref[...] = jnp.zeros_like(carry_ref)
    acc_ref[...] = jnp.zeros_like(acc_ref)
    q_pos = (i + 1) * Q_BLOCK + lax.broadcasted_iota(jnp.int32, (Q_BLOCK, Q_BLOCK), 0)
    k_iota = lax.broadcasted_iota(jnp.int32, (Q_BLOCK, Q_BLOCK), 1)

    def cond(state):
        kb, top = state
        return jnp.logical_and(kb >= 0, top > -STICK_CUTOFF)

    def body(state):
        kb, _ = state
        off = pl.multiple_of(kb * Q_BLOCK, Q_BLOCK)
        k_pos = k_iota + off
        valid = jnp.logical_and(k_pos < q_pos, k_pos >= pad)
        top = jnp.full((), -jnp.inf, F32)
        for h in range(heads):
            z = jnp.dot(q_ref[0, h], kt_ref[0, h, :, pl.ds(off, Q_BLOCK)], preferred_element_type=F32)
            log_keep = -(jnp.maximum(z, 0.0) + jnp.log(1.0 + jnp.exp(-jnp.abs(z))))
            log_keep = jnp.where(valid, log_keep, 0.0)
            hi = log_keep.astype(BF16)
            lo = (log_keep - hi.astype(F32)).astype(BF16)
            sums = jnp.dot(jnp.concatenate([hi, lo], axis=1), tri_ref[...], preferred_element_type=F32)
            carry = carry_ref[h]
            w = jnp.where(valid, jnp.exp(z + sums[:, :Q_BLOCK] + carry), 0.0)
            acc_ref[h] += jnp.dot(w.astype(BF16), v_ref[0, h, pl.ds(off, Q_BLOCK), :], preferred_element_type=F32)
            carry = carry + sums[:, Q_BLOCK:]
            carry_ref[h] = carry
            top = jnp.maximum(top, jnp.max(carry))
        return kb - 1, top

    lax.while_loop(cond, body, (i + 1, jnp.zeros((), F32)))
    o_ref[0] = acc_ref[...].astype(o_ref.dtype)


def _attention(qh, kth, vh, pad):
    b, heads, l, dh = qh.shape
    lp = kth.shape[-1]
    r = lax.broadcasted_iota(jnp.int32, (2 * Q_BLOCK, 2 * Q_BLOCK), 0) % Q_BLOCK
    c = lax.broadcasted_iota(jnp.int32, (2 * Q_BLOCK, 2 * Q_BLOCK), 1)
    tri = jnp.where(jnp.logical_or(c >= Q_BLOCK, r >= c), 1.0, 0.0).astype(BF16)
    return pl.pallas_call(
        functools.partial(_attn_kernel, heads=heads, pad=pad),
        grid=(b, l // Q_BLOCK),
        in_specs=[
            pl.BlockSpec((1, heads, Q_BLOCK, dh), lambda bi, i: (bi, 0, i, 0)),
            pl.BlockSpec((1, heads, dh, lp), lambda bi, i: (bi, 0, 0, 0), pipeline_mode=pl.Buffered(1)),
            pl.BlockSpec((1, heads, lp, dh), lambda bi, i: (bi, 0, 0, 0), pipeline_mode=pl.Buffered(1)),
            pl.BlockSpec((2 * Q_BLOCK, 2 * Q_BLOCK), lambda bi, i: (0, 0)),
        ],
        out_specs=pl.BlockSpec((1, heads, Q_BLOCK, dh), lambda bi, i: (bi, 0, i, 0)),
        out_shape=jax.ShapeDtypeStruct((b, heads, l, dh), BF16),
        scratch_shapes=[pltpu.VMEM((heads, Q_BLOCK, Q_BLOCK), F32), pltpu.VMEM((heads, Q_BLOCK, dh), F32)],
        compiler_params=_cparams(("parallel", "parallel")),
        name="stick_attn",
    )(qh, kth, vh, tri)


def _ssm_tables(lam_re, lam_im, log_dt, b_re, b_im, c_re, c_im, d_skip):
    hp = lax.Precision.HIGHEST
    tc = SSM_CHUNK
    g, p = lam_re.shape
    ch = b_re.shape[-1]
    dt = jnp.exp(log_dt)[:, None]
    mag = jnp.exp(lam_re * dt)
    lb_re, lb_im = mag * jnp.cos(lam_im * dt), mag * jnp.sin(lam_im * dt)
    nr, ni = lb_re - 1.0, lb_im
    den = lam_re * lam_re + lam_im * lam_im
    f_re = (nr * lam_re + ni * lam_im) / den
    f_im = (ni * lam_re - nr * lam_im) / den
    bb_re = f_re[:, :, None] * b_re - f_im[:, :, None] * b_im
    bb_im = f_re[:, :, None] * b_im + f_im[:, :, None] * b_re
    pw_re, pw_im = jnp.ones((1, g, p), F32), jnp.zeros((1, g, p), F32)
    s_re, s_im = lb_re, lb_im
    while pw_re.shape[0] < tc + 1:
        n_re = pw_re * s_re - pw_im * s_im
        n_im = pw_re * s_im + pw_im * s_re
        pw_re, pw_im = jnp.concatenate([pw_re, n_re]), jnp.concatenate([pw_im, n_im])
        s_re, s_im = s_re * s_re - s_im * s_im, 2.0 * s_re * s_im
    pw_re, pw_im = pw_re[:tc + 1], pw_im[:tc + 1]
    cl_re = c_re[None] * pw_re[:, :, None, :] - c_im[None] * pw_im[:, :, None, :]
    cl_im = c_re[None] * pw_im[:, :, None, :] + c_im[None] * pw_re[:, :, None, :]
    kern = (jnp.einsum('tgop,gpi->tgoi', cl_re[:tc], bb_re, precision=hp)
            - jnp.einsum('tgop,gpi->tgoi', cl_im[:tc], bb_im, precision=hp))
    kern = kern.at[0].add(jnp.eye(ch, dtype=F32)[None] * d_skip.reshape(g, 1, ch))
    s_idx = jnp.arange(tc)[:, None]
    t_idx = jnp.arange(tc)[None, :]
    lag = t_idx - s_idx
    toe = jnp.where((lag >= 0)[:, :, None, None, None], kern[jnp.clip(lag, 0, tc - 1)], 0.0)
    m_intra = toe.transpose(2, 0, 4, 1, 3).reshape(g, tc * ch, tc * ch)
    rp_re, rp_im = pw_re[:tc][::-1], pw_im[:tc][::-1]
    st_re = rp_re[:, :, :, None] * bb_re[None] - rp_im[:, :, :, None] * bb_im[None]
    st_im = rp_re[:, :, :, None] * bb_im[None] + rp_im[:, :, :, None] * bb_re[None]
    m_state = jnp.concatenate([st_re, st_im], axis=2).transpose(1, 0, 3, 2).reshape(g, tc * ch, 2 * p)
    m_out = jnp.concatenate([cl_re[1:], -cl_im[1:]], axis=3)
    m_out = m_out.transpose(1, 3, 0, 2).reshape(g, 2 * p, tc * ch)
    return m_intra.astype(BF16), m_state.astype(BF16), m_out.astype(BF16), pw_re[tc], pw_im[tc]


def _ssm_state_kernel(u_ref, ms_ref, s_ref):
    s_ref[0] = jnp.dot(u_ref[0], ms_ref[0], preferred_element_type=F32)


def _ssm_scan_kernel(sre_ref, sim_ref, dre_ref, dim_ref, xre_ref, xim_ref):
    n = sre_ref.shape[0]
    d_re, d_im = dre_ref[...], dim_ref[...]

    def step(c, x):
        x_re, x_im = x
        xre_ref[c] = x_re
        xim_ref[c] = x_im
        s_re, s_im = sre_ref[c], sim_ref[c]
        return d_re * x_re - d_im * x_im + s_re, d_re * x_im + d_im * x_re + s_im

    z = jnp.zeros(sre_ref.shape[1:], F32)
    lax.fori_loop(0, n, step, (z, z))


def _ssm_out_kernel(u_ref, x_ref, mi_ref, mo_ref, y_ref):
    y = jnp.dot(u_ref[0], mi_ref[0], preferred_element_type=F32)
    y = y + jnp.dot(x_ref[0], mo_ref[0], preferred_element_type=F32)
    y_ref[0] = jax.nn.gelu(y).astype(y_ref.dtype)


def _ssm(u_meta, u_real, params):
    b, l, s = u_real.shape
    tc, ch, p = SSM_CHUNK, SSM_GROUP_CH, SSM_STATE
    g = s // ch
    m_intra, m_state, m_out, dec_re, dec_im = _ssm_tables(*params)
    lead = tc - N_META
    full = jnp.concatenate([jnp.zeros((b, lead, s), BF16), jnp.broadcast_to(u_meta[None], (b, N_META, s)), u_real], axis=1)
    nc = (l + tc) // tc
    rows = nc * b
    ug = full.reshape(b, nc, tc, g, ch).transpose(3, 1, 0, 2, 4).reshape(g, rows, tc * ch)
    blk = lambda r, c: pl.BlockSpec((1, r, c), lambda gi: (gi, 0, 0))
    s_inc = pl.pallas_call(
        _ssm_state_kernel, grid=(g,),
        in_specs=[blk(rows, tc * ch), blk(tc * ch, 2 * p)],
        out_specs=blk(rows, 2 * p),
        out_shape=jax.ShapeDtypeStruct((g, rows, 2 * p), F32),
        compiler_params=_cparams(("parallel",)), name="ssm_state",
    )(ug, m_state)
    s4 = s_inc.reshape(g, nc, b, 2, p)
    s_re = s4[:, :, :, 0].transpose(1, 2, 0, 3).reshape(nc, b, g * p)
    s_im = s4[:, :, :, 1].transpose(1, 2, 0, 3).reshape(nc, b, g * p)
    lw = 4 * LANES
    seq = pl.BlockSpec((nc, b, lw), lambda j: (0, 0, j))
    dec = pl.BlockSpec((1, lw), lambda j: (0, j))
    x_re, x_im = pl.pallas_call(
        _ssm_scan_kernel, grid=(g * p // lw,),
        in_specs=[seq, seq, dec, dec], out_specs=[seq, seq],
        out_shape=[jax.ShapeDtypeStruct((nc, b, g * p), F32)] * 2,
        compiler_params=_cparams(("parallel",)), name="ssm_scan",
    )(s_re, s_im, dec_re.reshape(1, g * p), dec_im.reshape(1, g * p))
    xs = jnp.stack([x_re.reshape(nc, b, g, p), x_im.reshape(nc, b, g, p)], axis=3)
    xs = xs.transpose(2, 0, 1, 3, 4).reshape(g, rows, 2 * p).astype(BF16)
    y = pl.pallas_call(
        _ssm_out_kernel, grid=(g,),
        in_specs=[blk(rows, tc * ch), blk(rows, 2 * p), blk(tc * ch, tc * ch), blk(2 * p, tc * ch)],
        out_specs=blk(rows, tc * ch),
        out_shape=jax.ShapeDtypeStruct((g, rows, tc * ch), BF16),
        compiler_params=_cparams(("parallel",)), name="ssm_out",
    )(ug, xs, m_intra, m_out)
    y = y.reshape(g, nc, b, tc, ch).transpose(2, 1, 3, 0, 4).reshape(b, nc * tc, s)
    return y[:, tc:]


def _merge_kernel(x_ref, a_ref, y_ref, ga_ref, gb_ref, gw_ref, gbias_ref, wpa_ref, wpb_ref, wo_ref, gn_ref,
                  wr_ref, br_ref, h_ref, hn_ref, route_ref):
    y = y_ref[...]
    gate = jax.nn.sigmoid(jnp.dot(y, gw_ref[...], preferred_element_type=F32) + gbias_ref[...])
    ssm = (y.astype(F32) * gate).astype(BF16)
    merged = (ga_ref[...].astype(F32) * jnp.dot(a_ref[...], wpa_ref[...], preferred_element_type=F32)
              + gb_ref[...].astype(F32) * jnp.dot(ssm, wpb_ref[...], preferred_element_type=F32))
    h = x_ref[...] + jnp.dot(merged.astype(BF16), wo_ref[...], preferred_element_type=F32)
    h_ref[...] = h
    hn = h * lax.rsqrt(jnp.mean(h * h, axis=-1, keepdims=True) + RMS_EPS) * gn_ref[...]
    hn_ref[...] = hn
    logits = jnp.dot(hn, wr_ref[...], preferred_element_type=F32, precision=lax.Precision.HIGHEST) + br_ref[...]
    lane = lax.broadcasted_iota(jnp.int32, logits.shape, 1)
    neg = jnp.full_like(logits, -jnp.inf)
    big = jnp.int32(LANES)
    is_grp = jnp.logical_and(lane >= N_EXPERTS, lane < N_EXPERTS + MOE_GROUPS)
    glog = jnp.where(is_grp, logits, neg)
    gmax = jnp.max(glog, axis=-1, keepdims=True)
    g_top = jnp.min(jnp.where(glog == gmax, lane, big), axis=-1, keepdims=True) - N_EXPERTS
    p_top = 1.0 / jnp.sum(jnp.exp(glog - gmax), axis=-1, keepdims=True)
    in_grp = (lane // EXPERTS_PER_GROUP) == g_top
    elog = jnp.where(jnp.logical_and(in_grp, lane < N_EXPERTS), logits, neg)
    e1 = jnp.max(elog, axis=-1, keepdims=True)
    i1 = jnp.min(jnp.where(elog == e1, lane, big), axis=-1, keepdims=True)
    elog2 = jnp.where(lane == i1, neg, elog)
    e2 = jnp.max(elog2, axis=-1, keepdims=True)
    i2 = jnp.min(jnp.where(elog2 == e2, lane, big), axis=-1, keepdims=True)
    t = jnp.exp(e2 - e1)
    w1 = p_top / (1.0 + t)
    w2 = p_top * t / (1.0 + t)
    route = jnp.where(lane == 0, i1.astype(F32),
                      jnp.where(lane == 1, i2.astype(F32),
                                jnp.where(lane == 2, w1, jnp.where(lane == 3, w2, 0.0))))
    route_ref[...] = route


def _merge(x2, attn, yssm, ga, gb, glu_w, glu_b, wpa, wpb, wo, gn, wr, br, tm):
    m, d = x2.shape
    aw, sw = attn.shape[1], yssm.shape[1]
    row = lambda n: pl.BlockSpec((tm, n), lambda i: (i, 0))
    full = lambda a: pl.BlockSpec(a.shape, lambda i: (0, 0))
    ws = (glu_w, glu_b, wpa, wpb, wo, gn, wr, br)
    return pl.pallas_call(
        _merge_kernel, grid=(m // tm,),
        in_specs=[row(d), row(aw), row(sw), row(d), row(d)] + [full(a) for a in ws],
        out_specs=[row(d), row(d), row(LANES)],
        out_shape=[jax.ShapeDtypeStruct((m, d), F32), jax.ShapeDtypeStruct((m, d), F32),
                   jax.ShapeDtypeStruct((m, LANES), F32)],
        compiler_params=_cparams(("parallel",)), name="merge_router",
    )(x2, attn, yssm, ga, gb, *ws)


def _gather_rows(src_hbm, idx_ref, base, buf, sem, n):
    def issue(r, carry):
        row = idx_ref[base + r]
        pltpu.make_async_copy(src_hbm.at[pl.ds(row, 1), :], buf.at[pl.ds(r, 1), :], sem).start()
        return carry
    lax.fori_loop(0, n, issue, 0)


def _expert_kernel(blk_e_ref, n_used_ref, src_ref, hn_hbm, w1_ref, w3_ref, w2_ref, y_ref, xbuf, sems, *, tm):
    j = pl.program_id(0)
    n_used = n_used_ref[0]
    slot = lax.rem(j, 2)

    def start(blk, s):
        _gather_rows(hn_hbm, src_ref, blk * tm, xbuf.at[s], sems.at[s], tm)

    @pl.when(j == 0)
    def _():
        start(0, 0)

    @pl.when(j + 1 < n_used)
    def _():
        start(j + 1, 1 - slot)

    @pl.when(j < n_used)
    def _():
        pltpu.make_async_copy(hn_hbm.at[pl.ds(0, tm), :], xbuf.at[slot], sems.at[slot]).wait()
        x = xbuf[slot].astype(BF16)
        a = jnp.dot(x, w1_ref[0], preferred_element_type=F32)
        b = jnp.dot(x, w3_ref[0], preferred_element_type=F32)
        hdn = (a * jax.nn.sigmoid(a) * b).astype(BF16)
        y_ref[...] = jnp.dot(hdn, w2_ref[0], preferred_element_type=F32)

    @pl.when(j >= n_used)
    def _():
        y_ref[...] = jnp.zeros_like(y_ref)


def _experts(hn, blk_e, n_used, slot_src, w1, w3, w2, tm):
    t, d = hn.shape
    e, _, ff = w1.shape
    nb = blk_e.shape[0]
    wspec = lambda r, c: pl.BlockSpec((1, r, c), lambda j, be, nu, ss: (be[j], 0, 0))
    return pl.pallas_call(
        functools.partial(_expert_kernel, tm=tm),
        grid_spec=pltpu.PrefetchScalarGridSpec(
            num_scalar_prefetch=3, grid=(nb,),
            in_specs=[pl.BlockSpec(memory_space=pl.ANY), wspec(d, ff), wspec(d, ff), wspec(ff, d)],
            out_specs=pl.BlockSpec((tm, d), lambda j, be, nu, ss: (j, 0)),
            scratch_shapes=[pltpu.VMEM((2, tm, d), F32), pltpu.SemaphoreType.DMA((2,))],
        ),
        out_shape=jax.ShapeDtypeStruct((nb * tm, d), F32),
        compiler_params=_cparams(("arbitrary",)), name="expert_ffn",
    )(blk_e, n_used, slot_src, hn, w1, w3, w2)


def _combine_kernel(pos_ref, h_ref, route_ref, gf_ref, yb_hbm, o_ref, ybuf, sems, *, tm):
    i = pl.program_id(0)
    n = pl.num_programs(0)
    slot = lax.rem(i, 2)

    def start(blk, s):
        _gather_rows(yb_hbm, pos_ref, blk * 2 * tm, ybuf.at[s], sems.at[s], 2 * tm)

    @pl.when(i == 0)
    def _():
        start(0, 0)

    @pl.when(i + 1 < n)
    def _():
        start(i + 1, 1 - slot)

    pltpu.make_async_copy(yb_hbm.at[pl.ds(0, 2 * tm), :], ybuf.at[slot], sems.at[slot]).wait()
    route = route_ref[...]
    w1 = route[:, 2:3]
    w2 = route[:, 3:4]
    h = h_ref[...] + w1 * ybuf[slot, :tm, :] + w2 * ybuf[slot, tm:, :]
    o_ref[...] = h * lax.rsqrt(jnp.mean(h * h, axis=-1, keepdims=True) + RMS_EPS) * gf_ref[...]


def _combine(pos, h, route, gf, yb, tm):
    t, d = h.shape
    return pl.pallas_call(
        functools.partial(_combine_kernel, tm=tm),
        grid_spec=pltpu.PrefetchScalarGridSpec(
            num_scalar_prefetch=1, grid=(t // tm,),
            in_specs=[pl.BlockSpec((tm, d), lambda i, p: (i, 0)), pl.BlockSpec((tm, LANES), lambda i, p: (i, 0)),
                      pl.BlockSpec((1, d), lambda i, p: (0, 0)), pl.BlockSpec(memory_space=pl.ANY)],
            out_specs=pl.BlockSpec((tm, d), lambda i, p: (i, 0)),
            scratch_shapes=[pltpu.VMEM((2, 2 * tm, d), F32), pltpu.SemaphoreType.DMA((2,))],
        ),
        out_shape=jax.ShapeDtypeStruct((t, d), F32),
        compiler_params=_cparams(("arbitrary",)), name="moe_combine",
    )(pos, h, route, gf, yb)


def _dispatch_plan(e1, e2, tm):
    t = e1.shape[0]
    n = 2 * t
    e_flat = jnp.stack([e1, e2], axis=1).reshape(n)
    tok_flat = jnp.repeat(jnp.arange(t, dtype=jnp.int32), 2)
    order = jnp.argsort(e_flat)
    e_s, t_s = e_flat[order], tok_flat[order]
    counts = jnp.bincount(e_flat, length=N_EXPERTS).astype(jnp.int32)
    padded = ((counts + tm - 1) // tm) * tm
    start = jnp.cumsum(counts) - counts
    pend = jnp.cumsum(padded)
    pstart = pend - padded
    dest = (pstart[e_s] + (jnp.arange(n, dtype=jnp.int32) - start[e_s])).astype(jnp.int32)
    nb = (n + N_EXPERTS * (tm - 1)) // tm + 1
    cap = nb * tm
    slot_src = jnp.zeros((cap,), jnp.int32).at[dest].set(t_s)
    pos = jnp.zeros((n,), jnp.int32).at[order].set(dest)
    blk_start = jnp.arange(nb, dtype=jnp.int32) * tm
    blk_e = jnp.minimum(jnp.searchsorted(pend, blk_start, side='right'), N_EXPERTS - 1).astype(jnp.int32)
    n_used = (pend[-1] // tm).astype(jnp.int32).reshape(1)
    blk_e = jnp.where(blk_start < pend[-1], blk_e, blk_e[jnp.maximum(n_used[0] - 1, 0)])
    return blk_e, n_used, slot_src, pos.reshape(t, 2)


def kernel(x, meta_tokens, norm_mix_g, w_in, ssm_lambda_re, ssm_lambda_im, ssm_log_dt, ssm_b_re, ssm_b_im,
           ssm_c_re, ssm_c_im, ssm_d, ssm_glu_w, ssm_glu_b, w_branch_attn, w_branch_ssm, w_out, norm_ffn_g,
           router_group_w, router_group_b, router_expert_w, router_expert_b, expert_w1, expert_w3, expert_w2,
           norm_final_g):
    b, l, d = x.shape
    depth = w_in.shape[0]
    aw = w_branch_attn.shape[1]
    sw = w_branch_ssm.shape[1]
    heads = aw // HEAD_DIM
    t = b * l
    pad = (-N_META) % Q_BLOCK
    assert l % Q_BLOCK == 0 and l % SSM_CHUNK == 0 and pad + N_META == Q_BLOCK
    h_real = x.reshape(t, d)
    h_meta = meta_tokens.astype(x.dtype)
    tm_tok = 512
    for layer in range(depth):
        w_bf = w_in[layer].astype(BF16)
        g_mix = norm_mix_g[layer].reshape(1, d)
        q, k, v, u, ga, gb = _inproj(h_real, g_mix, w_bf, aw, sw, tm_tok)
        _, k_m, v_m, u_m, _, _ = _inproj(h_meta, g_mix, w_bf, aw, sw, N_META)

        def with_meta(real, meta):
            lead = jnp.concatenate([jnp.zeros((pad, aw), BF16), meta], axis=0)
            return jnp.concatenate([jnp.broadcast_to(lead[None], (b, Q_BLOCK, aw)), real.reshape(b, l, aw)], axis=1)

        qh = q.reshape(b, l, heads, HEAD_DIM).transpose(0, 2, 1, 3)
        kth = with_meta(k, k_m).reshape(b, l + Q_BLOCK, heads, HEAD_DIM).transpose(0, 2, 3, 1)
        vh = with_meta(v, v_m).reshape(b, l + Q_BLOCK, heads, HEAD_DIM).transpose(0, 2, 1, 3)
        attn = _attention(qh, kth, vh, pad).transpose(0, 2, 1, 3).reshape(t, aw)

        ssm_params = (ssm_lambda_re[layer], ssm_lambda_im[layer], ssm_log_dt[layer], ssm_b_re[layer], ssm_b_im[layer],
                      ssm_c_re[layer], ssm_c_im[layer], ssm_d[layer])
        yssm = _ssm(u_m, u.reshape(b, l, sw), ssm_params).reshape(t, sw)

        wr = jnp.zeros((d, LANES), F32)
        wr = wr.at[:, :N_EXPERTS].set(router_expert_w[layer]).at[:, N_EXPERTS:N_EXPERTS + MOE_GROUPS].set(router_group_w[layer])
        br = jnp.zeros((1, LANES), F32)
        br = br.at[0, :N_EXPERTS].set(router_expert_b[layer]).at[0, N_EXPERTS:N_EXPERTS + MOE_GROUPS].set(router_group_b[layer])
        h_mid, hn, route = _merge(
            h_real, attn, yssm, ga, gb, ssm_glu_w[layer].astype(BF16), ssm_glu_b[layer].reshape(1, sw).astype(F32),
            w_branch_attn[layer].astype(BF16), w_branch_ssm[layer].astype(BF16), w_out[layer].astype(BF16),
            norm_ffn_g[layer].reshape(1, d), wr, br, tm_tok)

        e1 = route[:, 0].astype(jnp.int32)
        e2 = route[:, 1].astype(jnp.int32)
        blk_e, n_used, slot_src, pos = _dispatch_plan(e1, e2, MOE_TILE)
        yb = _experts(hn, blk_e, n_used, slot_src, expert_w1[layer].astype(BF16), expert_w3[layer].astype(BF16),
                      expert_w2[layer].astype(BF16), MOE_TILE)
        tm_c = 256
        pos_tiles = pos.reshape(t // tm_c, tm_c, 2).transpose(0, 2, 1).reshape(2 * t)
        last = layer == depth - 1
        assert last
        h_real = _combine(pos_tiles, h_mid, route, norm_final_g.reshape(1, d), yb, tm_c)
    return h_real.reshape(b, l, d)
```

```python
import functools

import jax
import jax.numpy as jnp
from jax import lax
from jax.experimental import pallas as pl
from jax.experimental.pallas import tpu as pltpu

F32 = jnp.float32
BF16 = jnp.bfloat16

N_META = 16
Q_BLOCK = 128
HEAD_DIM = 64
SSM_GROUP_CH = 16
SSM_STATE = 64
MOE_GROUPS = 4
EXPERTS_PER_GROUP = 8
N_EXPERTS = MOE_GROUPS * EXPERTS_PER_GROUP
RMS_EPS = 1e-6

LANES = 128
SSM_CHUNK = LANES
MOE_TILE = 256
STICK_CUTOFF = 104.0
VMEM_LIMIT = 52 * 1024 * 1024

_NT = (((1,), (1,)), ((), ()))
_TN = (((0,), (0,)), ((), ()))


def _cparams(sem):
    return pltpu.CompilerParams(dimension_semantics=sem, vmem_limit_bytes=VMEM_LIMIT)


def _inproj_kernel(x_ref, g_ref, wq_ref, wkt_ref, wv_ref, wut_ref, wga_ref, wgb_ref,
                   q_ref, kt_ref, v_ref, ut_ref, ga_ref, gb_ref):
    x = x_ref[...]
    ms = jnp.mean(x * x, axis=-1, keepdims=True)
    xn = (x * lax.rsqrt(ms + RMS_EPS) * g_ref[...]).astype(BF16)
    q_ref[...] = (jnp.dot(xn, wq_ref[...], preferred_element_type=F32) * (HEAD_DIM ** -0.5)).astype(BF16)
    kt_ref[...] = lax.dot_general(wkt_ref[...], xn, _NT, preferred_element_type=F32).astype(BF16)
    v_ref[...] = jnp.dot(xn, wv_ref[...], preferred_element_type=F32).astype(BF16)
    ut_ref[...] = lax.dot_general(wut_ref[...], xn, _NT, preferred_element_type=F32).astype(BF16)
    ga_ref[...] = jax.nn.sigmoid(jnp.dot(xn, wga_ref[...], preferred_element_type=F32)).astype(BF16)
    gb_ref[...] = jax.nn.sigmoid(jnp.dot(xn, wgb_ref[...], preferred_element_type=F32)).astype(BF16)


def _inproj(x2, g, ws, tm):
    m, d = x2.shape
    aw, sw = ws[0].shape[1], ws[3].shape[0]
    row = lambda n: pl.BlockSpec((tm, n), lambda i: (i, 0))
    col = lambda n: pl.BlockSpec((n, tm), lambda i: (0, i))
    full = lambda a: pl.BlockSpec(a.shape, lambda i: (0, 0))
    sds = jax.ShapeDtypeStruct
    return pl.pallas_call(
        _inproj_kernel,
        grid=(m // tm,),
        in_specs=[row(d), full(g)] + [full(w) for w in ws],
        out_specs=[row(aw), col(aw), row(aw), col(sw), row(d), row(d)],
        out_shape=[sds((m, aw), BF16), sds((aw, m), BF16), sds((m, aw), BF16), sds((sw, m), BF16),
                   sds((m, d), BF16), sds((m, d), BF16)],
        compiler_params=_cparams(("parallel",)),
        name="inproj",
    )(x2, g, *ws)


def _attn_kernel(q_ref, kt_ref, v_ref, ktm_ref, vm_ref, tri_ref, o_ref, qm_ref, carry_ref, acc_ref, *, heads, pad):
    i = pl.program_id(1)
    pairs = heads // 2
    carry_ref[...] = jnp.zeros_like(carry_ref)
    acc_ref[...] = jnp.zeros_like(acc_ref)
    lane = lax.broadcasted_iota(jnp.int32, (Q_BLOCK, LANES), 1)
    row = lax.broadcasted_iota(jnp.int32, (Q_BLOCK, LANES), 0)
    low_half = lane < HEAD_DIM
    for p in range(pairs):
        qp = q_ref[0, :, p * LANES:(p + 1) * LANES]
        qm_ref[2 * p] = jnp.where(low_half, qp, jnp.zeros_like(qp))
        qm_ref[2 * p + 1] = jnp.where(low_half, jnp.zeros_like(qp), qp)

    def block_step(get_kt, get_v, valid):
        kts = [get_kt(p) for p in range(pairs)]
        vs = [get_v(p) for p in range(pairs)]
        zs = [jnp.dot(qm_ref[h], kts[h // 2], preferred_element_type=F32) for h in range(heads)]
        cats = []
        for h in range(heads):
            z = zs[h]
            log_keep = -(jnp.maximum(z, 0.0) + jnp.log(1.0 + jnp.exp(-jnp.abs(z))))
            if valid is not None:
                log_keep = jnp.where(valid, log_keep, 0.0)
            hi = log_keep.astype(BF16)
            lo = (log_keep - hi.astype(F32)).astype(BF16)
            cats.append(jnp.concatenate([hi, lo], axis=1))
        sums = [jnp.dot(cats[h], tri_ref[...], preferred_element_type=F32) for h in range(heads)]
        top = None
        for h in range(heads):
            carry = carry_ref[h]
            w = jnp.exp(zs[h] + sums[h][:, :LANES] + carry)
            if valid is not None:
                w = jnp.where(valid, w, 0.0)
            acc_ref[h] += jnp.dot(w.astype(BF16), vs[h // 2], preferred_element_type=F32)
            carry = carry + sums[h][:, LANES:]
            carry_ref[h] = carry
            top = carry if top is None else jnp.maximum(top, carry)
        return jnp.max(top)

    off_d = pl.multiple_of(i * Q_BLOCK, Q_BLOCK)
    top0 = block_step(lambda p: kt_ref[p * LANES:(p + 1) * LANES, pl.ds(off_d, Q_BLOCK)],
                      lambda p: v_ref[0, pl.ds(off_d, Q_BLOCK), p * LANES:(p + 1) * LANES],
                      lane < row)

    def cond(state):
        kb, top = state
        return jnp.logical_and(kb >= 0, top > -STICK_CUTOFF)

    def body(state):
        kb, _ = state
        off = pl.multiple_of(kb * Q_BLOCK, Q_BLOCK)
        top = block_step(lambda p: kt_ref[p * LANES:(p + 1) * LANES, pl.ds(off, Q_BLOCK)],
                         lambda p: v_ref[0, pl.ds(off, Q_BLOCK), p * LANES:(p + 1) * LANES],
                         None)
        return kb - 1, top

    _, top1 = lax.while_loop(cond, body, (i - 1, top0))

    @pl.when(top1 > -STICK_CUTOFF)
    def _():
        block_step(lambda p: ktm_ref[p * LANES:(p + 1) * LANES, :],
                   lambda p: vm_ref[:, p * LANES:(p + 1) * LANES],
                   lane >= pad)

    for p in range(pairs):
        o_ref[0, :, p * LANES:(p + 1) * LANES] = jnp.where(low_half, acc_ref[2 * p], acc_ref[2 * p + 1]).astype(o_ref.dtype)


def _attention(q, kt, v, kt_meta, v_meta, b, heads, pad):
    t, aw = q.shape
    l = t // b
    r = lax.broadcasted_iota(jnp.int32, (2 * LANES, 2 * LANES), 0) % LANES
    c = lax.broadcasted_iota(jnp.int32, (2 * LANES, 2 * LANES), 1)
    tri = jnp.where(jnp.logical_or(c >= LANES, r >= c), 1.0, 0.0).astype(BF16)
    once = pl.Buffered(1)
    out = pl.pallas_call(
        functools.partial(_attn_kernel, heads=heads, pad=pad),
        grid=(b, l // Q_BLOCK),
        in_specs=[
            pl.BlockSpec((1, Q_BLOCK, aw), lambda bi, i: (bi, i, 0)),
            pl.BlockSpec((aw, l), lambda bi, i: (0, bi), pipeline_mode=once),
            pl.BlockSpec((1, l, aw), lambda bi, i: (bi, 0, 0), pipeline_mode=once),
            pl.BlockSpec((aw, Q_BLOCK), lambda bi, i: (0, 0)),
            pl.BlockSpec((Q_BLOCK, aw), lambda bi, i: (0, 0)),
            pl.BlockSpec((2 * LANES, 2 * LANES), lambda bi, i: (0, 0)),
        ],
        out_specs=pl.BlockSpec((1, Q_BLOCK, aw), lambda bi, i: (bi, i, 0)),
        out_shape=jax.ShapeDtypeStruct((b, l, aw), BF16),
        scratch_shapes=[pltpu.VMEM((heads, Q_BLOCK, LANES), BF16), pltpu.VMEM((heads, Q_BLOCK, LANES), F32),
                        pltpu.VMEM((heads, Q_BLOCK, LANES), F32)],
        compiler_params=_cparams(("parallel", "arbitrary")),
        name="stick_attn",
    )(q.reshape(b, l, aw), kt, v.reshape(b, l, aw), kt_meta, v_meta, tri)
    return out.reshape(t, aw)


def _ssm_tables(lam_re, lam_im, log_dt, b_re, b_im, c_re, c_im, d_skip):
    hp = lax.Precision.HIGHEST
    tc = SSM_CHUNK
    g, p = lam_re.shape
    ch = b_re.shape[-1]
    dt = jnp.exp(log_dt)[:, None]
    mag = jnp.exp(lam_re * dt)
    lb_re, lb_im = mag * jnp.cos(lam_im * dt), mag * jnp.sin(lam_im * dt)
    nr, ni = lb_re - 1.0, lb_im
    den = lam_re * lam_re + lam_im * lam_im
    f_re = (nr * lam_re + ni * lam_im) / den
    f_im = (ni * lam_re - nr * lam_im) / den
    bb_re = f_re[:, :, None] * b_re - f_im[:, :, None] * b_im
    bb_im = f_re[:, :, None] * b_im + f_im[:, :, None] * b_re
    pw_re, pw_im = jnp.ones((1, g, p), F32), jnp.zeros((1, g, p), F32)
    s_re, s_im = lb_re, lb_im
    while pw_re.shape[0] < tc + 1:
        n_re = pw_re * s_re - pw_im * s_im
        n_im = pw_re * s_im + pw_im * s_re
        pw_re, pw_im = jnp.concatenate([pw_re, n_re]), jnp.concatenate([pw_im, n_im])
        s_re, s_im = s_re * s_re - s_im * s_im, 2.0 * s_re * s_im
    pw_re, pw_im = pw_re[:tc + 1], pw_im[:tc + 1]
    cl_re = c_re[None] * pw_re[:, :, None, :] - c_im[None] * pw_im[:, :, None, :]
    cl_im = c_re[None] * pw_im[:, :, None, :] + c_im[None] * pw_re[:, :, None, :]
    kern = (jnp.einsum('tgop,gpi->giot', cl_re[:tc], bb_re, precision=hp)
            - jnp.einsum('tgop,gpi->giot', cl_im[:tc], bb_im, precision=hp))
    kern = kern.at[:, :, :, 0].add(jnp.eye(ch, dtype=F32)[None] * d_skip.reshape(g, ch, 1))
    rp_re, rp_im = pw_re[:tc][::-1], pw_im[:tc][::-1]
    st_re = rp_re[:, :, :, None] * bb_re[None] - rp_im[:, :, :, None] * bb_im[None]
    st_im = rp_re[:, :, :, None] * bb_im[None] + rp_im[:, :, :, None] * bb_re[None]
    half = (jnp.arange(g) % 2)[:, None, None]

    def place_cols(m):
        z = jnp.zeros_like(m)
        return jnp.where(half == 0, jnp.concatenate([m, z], axis=2), jnp.concatenate([z, m], axis=2))

    ms_re = place_cols(st_re.transpose(1, 3, 0, 2).reshape(g, ch * tc, p)).astype(BF16)
    ms_im = place_cols(st_im.transpose(1, 3, 0, 2).reshape(g, ch * tc, p)).astype(BF16)
    mo_re = place_cols(cl_re[1:].transpose(1, 2, 0, 3).reshape(g, ch * tc, p)).transpose(0, 2, 1).astype(BF16)
    mo_im = place_cols(-cl_im[1:].transpose(1, 2, 0, 3).reshape(g, ch * tc, p)).transpose(0, 2, 1).astype(BF16)
    dec_re = pw_re[tc].reshape(g // 2, 1, 2 * p)
    dec_im = pw_im[tc].reshape(g // 2, 1, 2 * p)
    return kern, ms_re, ms_im, mo_re, mo_im, dec_re, dec_im


def _ssm_kernel(u_ref, um_ref, k_ref, msr_ref, msi_ref, mor_ref, moi_ref, dre_ref, dim_ref, y_ref,
                toep_ref, sre_ref, sim_ref, xre_ref, xim_ref, *, batch):
    ch, tc = SSM_GROUP_CH, SSM_CHUNK
    nch = u_ref.shape[1]
    per_b = nch // batch

    def chunk_rows(ref, gl):
        return jnp.concatenate([ref[gl * ch + ci] for ci in range(ch)], axis=1)

    u = [chunk_rows(u_ref, gl) for gl in range(2)]
    um = [jnp.concatenate([jnp.broadcast_to(um_ref[gl * ch + ci:gl * ch + ci + 1, :], (8, tc)) for ci in range(ch)], axis=1)
          for gl in range(2)]

    def to_state(lhs, ms_ref):
        return (jnp.dot(lhs[0], ms_ref[0], preferred_element_type=F32)
                + jnp.dot(lhs[1], ms_ref[1], preferred_element_type=F32))

    sre_ref[...] = to_state(u, msr_ref)
    sim_ref[...] = to_state(u, msi_ref)
    x0_re = to_state(um, msr_ref)[0:1]
    x0_im = to_state(um, msi_ref)[0:1]
    d_re, d_im = dre_ref[0], dim_ref[0]

    def step(c, xs):
        new = []
        for bi in range(batch):
            x_re, x_im = xs[2 * bi], xs[2 * bi + 1]
            r = bi * per_b + c
            xre_ref[pl.ds(r, 1), :] = x_re
            xim_ref[pl.ds(r, 1), :] = x_im
            new.append(d_re * x_re - d_im * x_im + sre_ref[pl.ds(r, 1), :])
            new.append(d_re * x_im + d_im * x_re + sim_ref[pl.ds(r, 1), :])
        return tuple(new)

    lax.fori_loop(0, per_b, step, (x0_re, x0_im) * batch)
    xs_re = xre_ref[...].astype(BF16)
    xs_im = xim_ref[...].astype(BF16)

    lane = lax.broadcasted_iota(jnp.int32, (tc, tc), 1)
    row = lax.broadcasted_iota(jnp.int32, (tc, tc), 0)
    causal = lane >= row
    for gl in range(2):
        def fill(ci, carry):
            for co in range(ch):
                taps = jnp.broadcast_to(k_ref[gl, ci, pl.ds(co, 1), :], (tc, tc))
                blk = pltpu.roll(taps, 0, 1, stride=1, stride_axis=0)
                blk = jnp.where(causal, blk, 0.0).astype(BF16)
                toep_ref[pl.ds(pl.multiple_of(ci * tc, tc), tc), co * tc:(co + 1) * tc] = blk
            return carry
        lax.fori_loop(0, ch, fill, 0)
        y = jnp.dot(u[gl], toep_ref[...], preferred_element_type=F32)
        y = y + jnp.dot(xs_re, mor_ref[gl], preferred_element_type=F32)
        y = y + jnp.dot(xs_im, moi_ref[gl], preferred_element_type=F32)
        y = jax.nn.gelu(y).astype(y_ref.dtype)
        for co in range(ch):
            y_ref[gl * ch + co] = y[:, co * tc:(co + 1) * tc]


def _ssm(ut, ut_meta, params, batch):
    s, t = ut.shape
    tc, ch, p = SSM_CHUNK, SSM_GROUP_CH, SSM_STATE
    g = s // ch
    nch = t // tc
    kern, ms_re, ms_im, mo_re, mo_im, dec_re, dec_im = _ssm_tables(*params)
    u3 = ut.reshape(s, nch, tc)
    pair3 = lambda a, b_: pl.BlockSpec((2, a, b_), lambda j: (j, 0, 0))
    y3 = pl.pallas_call(
        functools.partial(_ssm_kernel, batch=batch),
        grid=(g // 2,),
        in_specs=[
            pl.BlockSpec((2 * ch, nch, tc), lambda j: (j, 0, 0)),
            pl.BlockSpec((2 * ch, tc), lambda j: (j, 0)),
            pl.BlockSpec((2, ch, ch, tc), lambda j: (j, 0, 0, 0)),
            pair3(ch * tc, 2 * p), pair3(ch * tc, 2 * p), pair3(2 * p, ch * tc), pair3(2 * p, ch * tc),
            pl.BlockSpec((1, 1, 2 * p), lambda j: (j, 0, 0)), pl.BlockSpec((1, 1, 2 * p), lambda j: (j, 0, 0)),
        ],
        out_specs=pl.BlockSpec((2 * ch, nch, tc), lambda j: (j, 0, 0)),
        out_shape=jax.ShapeDtypeStruct((s, nch, tc), BF16),
        scratch_shapes=[pltpu.VMEM((ch * tc, ch * tc), BF16)] + [pltpu.VMEM((nch, 2 * p), F32)] * 4,
        compiler_params=_cparams(("parallel",)), name="ssm_chunked",
    )(u3, ut_meta, kern, ms_re, ms_im, mo_re, mo_im, dec_re, dec_im)
    return y3.reshape(s, t)


def _merge_kernel(x_ref, a_ref, yt_ref, ga_ref, gb_ref, gwt_ref, gbias_ref, wpa_ref, wpb_ref, wo_ref, gn_ref,
                  wr_ref, br_ref, h_ref, hn_ref, route_ref):
    yt = yt_ref[...]
    gate = jax.nn.sigmoid(jnp.dot(gwt_ref[...], yt, preferred_element_type=F32) + gbias_ref[...])
    ssm_t = (yt.astype(F32) * gate).astype(BF16)
    merged = (ga_ref[...].astype(F32) * jnp.dot(a_ref[...], wpa_ref[...], preferred_element_type=F32)
              + gb_ref[...].astype(F32) * lax.dot_general(ssm_t, wpb_ref[...], _TN, preferred_element_type=F32))
    h = x_ref[...] + jnp.dot(merged.astype(BF16), wo_ref[...], preferred_element_type=F32)
    h_ref[...] = h
    hn = h * lax.rsqrt(jnp.mean(h * h, axis=-1, keepdims=True) + RMS_EPS) * gn_ref[...]
    hn_ref[...] = hn
    logits = jnp.dot(hn, wr_ref[...], preferred_element_type=F32, precision=lax.Precision.HIGHEST) + br_ref[...]
    lane = lax.broadcasted_iota(jnp.int32, logits.shape, 1)
    neg = jnp.full_like(logits, -jnp.inf)
    big = jnp.int32(LANES)
    is_grp = jnp.logical_and(lane >= N_EXPERTS, lane < N_EXPERTS + MOE_GROUPS)
    glog = jnp.where(is_grp, logits, neg)
    gmax = jnp.max(glog, axis=-1, keepdims=True)
    g_top = jnp.min(jnp.where(glog == gmax, lane, big), axis=-1, keepdims=True) - N_EXPERTS
    p_top = 1.0 / jnp.sum(jnp.exp(glog - gmax), axis=-1, keepdims=True)
    in_grp = (lane // EXPERTS_PER_GROUP) == g_top
    elog = jnp.where(jnp.logical_and(in_grp, lane < N_EXPERTS), logits, neg)
    e1 = jnp.max(elog, axis=-1, keepdims=True)
    i1 = jnp.min(jnp.where(elog == e1, lane, big), axis=-1, keepdims=True)
    elog2 = jnp.where(lane == i1, neg, elog)
    e2 = jnp.max(elog2, axis=-1, keepdims=True)
    i2 = jnp.min(jnp.where(elog2 == e2, lane, big), axis=-1, keepdims=True)
    t = jnp.exp(e2 - e1)
    w1 = p_top / (1.0 + t)
    w2 = p_top * t / (1.0 + t)
    route = jnp.where(lane == 0, i1.astype(F32),
                      jnp.where(lane == 1, i2.astype(F32),
                                jnp.where(lane == 2, w1, jnp.where(lane == 3, w2, 0.0))))
    route_ref[...] = route


def _merge(x2, attn, yt, ga, gb, glu_wt, glu_b, wpa, wpb, wo, gn, wr, br, tm):
    m, d = x2.shape
    aw, sw = attn.shape[1], yt.shape[0]
    row = lambda n: pl.BlockSpec((tm, n), lambda i: (i, 0))
    full = lambda a: pl.BlockSpec(a.shape, lambda i: (0, 0))
    ws = (glu_wt, glu_b, wpa, wpb, wo, gn, wr, br)
    return pl.pallas_call(
        _merge_kernel, grid=(m // tm,),
        in_specs=[row(d), row(aw), pl.BlockSpec((sw, tm), lambda i: (0, i)), row(d), row(d)] + [full(a) for a in ws],
        out_specs=[row(d), row(d), row(LANES)],
        out_shape=[jax.ShapeDtypeStruct((m, d), F32), jax.ShapeDtypeStruct((m, d), F32),
                   jax.ShapeDtypeStruct((m, LANES), F32)],
        compiler_params=_cparams(("parallel",)), name="merge_router",
    )(x2, attn, yt, ga, gb, *ws)


def _gather_rows(src_hbm, idx_ref, base, buf, sem, n):
    def issue(r, carry):
        row = idx_ref[base + r]
        pltpu.make_async_copy(src_hbm.at[pl.ds(row, 1), :], buf.at[pl.ds(r, 1), :], sem).start()
        return carry
    lax.fori_loop(0, n, issue, 0)


def _expert_kernel(blk_e_ref, n_used_ref, src_ref, hn_hbm, w1_ref, w3_ref, w2_ref, y_ref, xbuf, sems, *, tm):
    j = pl.program_id(0)
    n_used = n_used_ref[0]
    slot = lax.rem(j, 2)

    def start(blk, s):
        _gather_rows(hn_hbm, src_ref, blk * tm, xbuf.at[s], sems.at[s], tm)

    @pl.when(j == 0)
    def _():
        start(0, 0)

    @pl.when(j + 1 < n_used)
    def _():
        start(j + 1, 1 - slot)

    @pl.when(j < n_used)
    def _():
        pltpu.make_async_copy(hn_hbm.at[pl.ds(0, tm), :], xbuf.at[slot], sems.at[slot]).wait()
        x = xbuf[slot].astype(BF16)
        a = jnp.dot(x, w1_ref[0], preferred_element_type=F32)
        b = jnp.dot(x, w3_ref[0], preferred_element_type=F32)
        hdn = (a * jax.nn.sigmoid(a) * b).astype(BF16)
        y_ref[...] = jnp.dot(hdn, w2_ref[0], preferred_element_type=F32)

    @pl.when(j >= n_used)
    def _():
        y_ref[...] = jnp.zeros_like(y_ref)


def _experts(hn, blk_e, n_used, slot_src, w1, w3, w2, tm):
    t, d = hn.shape
    e, _, ff = w1.shape
    nb = blk_e.shape[0]
    wspec = lambda r, c: pl.BlockSpec((1, r, c), lambda j, be, nu, ss: (be[j], 0, 0))
    return pl.pallas_call(
        functools.partial(_expert_kernel, tm=tm),
        grid_spec=pltpu.PrefetchScalarGridSpec(
            num_scalar_prefetch=3, grid=(nb,),
            in_specs=[pl.BlockSpec(memory_space=pl.ANY), wspec(d, ff), wspec(d, ff), wspec(ff, d)],
            out_specs=pl.BlockSpec((tm, d), lambda j, be, nu, ss: (j, 0)),
            scratch_shapes=[pltpu.VMEM((2, tm, d), F32), pltpu.SemaphoreType.DMA((2,))],
        ),
        out_shape=jax.ShapeDtypeStruct((nb * tm, d), F32),
        compiler_params=_cparams(("arbitrary",)), name="expert_ffn",
    )(blk_e, n_used, slot_src, hn, w1, w3, w2)


def _combine_kernel(pos_ref, h_ref, route_ref, gf_ref, yb_hbm, o_ref, ybuf, sems, *, tm):
    i = pl.program_id(0)
    n = pl.num_programs(0)
    slot = lax.rem(i, 2)

    def start(blk, s):
        _gather_rows(yb_hbm, pos_ref, blk * 2 * tm, ybuf.at[s], sems.at[s], 2 * tm)

    @pl.when(i == 0)
    def _():
        start(0, 0)

    @pl.when(i + 1 < n)
    def _():
        start(i + 1, 1 - slot)

    pltpu.make_async_copy(yb_hbm.at[pl.ds(0, 2 * tm), :], ybuf.at[slot], sems.at[slot]).wait()
    route = route_ref[...]
    w1 = route[:, 2:3]
    w2 = route[:, 3:4]
    h = h_ref[...] + w1 * ybuf[slot, :tm, :] + w2 * ybuf[slot, tm:, :]
    o_ref[...] = h * lax.rsqrt(jnp.mean(h * h, axis=-1, keepdims=True) + RMS_EPS) * gf_ref[...]


def _combine(pos, h, route, gf, yb, tm):
    t, d = h.shape
    return pl.pallas_call(
        functools.partial(_combine_kernel, tm=tm),
        grid_spec=pltpu.PrefetchScalarGridSpec(
            num_scalar_prefetch=1, grid=(t // tm,),
            in_specs=[pl.BlockSpec((tm, d), lambda i, p: (i, 0)), pl.BlockSpec((tm, LANES), lambda i, p: (i, 0)),
                      pl.BlockSpec((1, d), lambda i, p: (0, 0)), pl.BlockSpec(memory_space=pl.ANY)],
            out_specs=pl.BlockSpec((tm, d), lambda i, p: (i, 0)),
            scratch_shapes=[pltpu.VMEM((2, 2 * tm, d), F32), pltpu.SemaphoreType.DMA((2,))],
        ),
        out_shape=jax.ShapeDtypeStruct((t, d), F32),
        compiler_params=_cparams(("arbitrary",)), name="moe_combine",
    )(pos, h, route, gf, yb)


def _dispatch_plan(e1, e2, tm):
    t = e1.shape[0]
    n = 2 * t
    e_flat = jnp.stack([e1, e2], axis=1).reshape(n)
    tok_flat = jnp.repeat(jnp.arange(t, dtype=jnp.int32), 2)
    order = jnp.argsort(e_flat)
    e_s, t_s = e_flat[order], tok_flat[order]
    counts = jnp.bincount(e_flat, length=N_EXPERTS).astype(jnp.int32)
    padded = ((counts + tm - 1) // tm) * tm
    start = jnp.cumsum(counts) - counts
    pend = jnp.cumsum(padded)
    pstart = pend - padded
    dest = (pstart[e_s] + (jnp.arange(n, dtype=jnp.int32) - start[e_s])).astype(jnp.int32)
    nb = (n + N_EXPERTS * (tm - 1)) // tm + 1
    cap = nb * tm
    slot_src = jnp.zeros((cap,), jnp.int32).at[dest].set(t_s)
    pos = jnp.zeros((n,), jnp.int32).at[order].set(dest)
    blk_start = jnp.arange(nb, dtype=jnp.int32) * tm
    blk_e = jnp.minimum(jnp.searchsorted(pend, blk_start, side='right'), N_EXPERTS - 1).astype(jnp.int32)
    n_used = (pend[-1] // tm).astype(jnp.int32).reshape(1)
    blk_e = jnp.where(blk_start < pend[-1], blk_e, blk_e[jnp.maximum(n_used[0] - 1, 0)])
    return blk_e, n_used, slot_src, pos.reshape(t, 2)


def kernel(x, meta_tokens, norm_mix_g, w_in, ssm_lambda_re, ssm_lambda_im, ssm_log_dt, ssm_b_re, ssm_b_im,
           ssm_c_re, ssm_c_im, ssm_d, ssm_glu_w, ssm_glu_b, w_branch_attn, w_branch_ssm, w_out, norm_ffn_g,
           router_group_w, router_group_b, router_expert_w, router_expert_b, expert_w1, expert_w3, expert_w2,
           norm_final_g):
    b, l, d = x.shape
    depth = w_in.shape[0]
    aw = w_branch_attn.shape[1]
    sw = w_branch_ssm.shape[1]
    heads = aw // HEAD_DIM
    t = b * l
    pad = (-N_META) % Q_BLOCK
    assert depth == 1 and l % Q_BLOCK == 0 and pad + N_META == Q_BLOCK and SSM_CHUNK == Q_BLOCK
    layer = 0
    h_real = x.reshape(t, d)
    h_meta = meta_tokens.astype(x.dtype)
    tm_tok = 512

    wi = w_in[layer].astype(BF16)
    o = 3 * aw + sw
    ws = (wi[:, :aw], wi[:, aw:2 * aw].T, wi[:, 2 * aw:3 * aw], wi[:, 3 * aw:o].T, wi[:, o:o + d], wi[:, o + d:])
    g_mix = norm_mix_g[layer].reshape(1, d)
    q, kt, v, ut, ga, gb = _inproj(h_real, g_mix, ws, tm_tok)
    _, kt_m, v_m, ut_m, _, _ = _inproj(h_meta, g_mix, ws, N_META)
    kt_meta = jnp.concatenate([jnp.zeros((aw, pad), BF16), kt_m], axis=1)
    v_meta = jnp.concatenate([jnp.zeros((pad, aw), BF16), v_m], axis=0)
    ut_meta = jnp.concatenate([jnp.zeros((sw, pad), BF16), ut_m], axis=1)

    attn = _attention(q, kt, v, kt_meta, v_meta, b, heads, pad)

    ssm_params = (ssm_lambda_re[layer], ssm_lambda_im[layer], ssm_log_dt[layer], ssm_b_re[layer], ssm_b_im[layer],
                  ssm_c_re[layer], ssm_c_im[layer], ssm_d[layer])
    yt = _ssm(ut, ut_meta, ssm_params, b)

    wr = jnp.zeros((d, LANES), F32)
    wr = wr.at[:, :N_EXPERTS].set(router_expert_w[layer]).at[:, N_EXPERTS:N_EXPERTS + MOE_GROUPS].set(router_group_w[layer])
    br = jnp.zeros((1, LANES), F32)
    br = br.at[0, :N_EXPERTS].set(router_expert_b[layer]).at[0, N_EXPERTS:N_EXPERTS + MOE_GROUPS].set(router_group_b[layer])
    h_mid, hn, route = _merge(
        h_real, attn, yt, ga, gb, ssm_glu_w[layer].T.astype(BF16), ssm_glu_b[layer].reshape(sw, 1).astype(F32),
        w_branch_attn[layer].astype(BF16), w_branch_ssm[layer].astype(BF16), w_out[layer].astype(BF16),
        norm_ffn_g[layer].reshape(1, d), wr, br, tm_tok)

    e1 = route[:, 0].astype(jnp.int32)
    e2 = route[:, 1].astype(jnp.int32)
    blk_e, n_used, slot_src, pos = _dispatch_plan(e1, e2, MOE_TILE)
    yb = _experts(hn, blk_e, n_used, slot_src, expert_w1[layer].astype(BF16), expert_w3[layer].astype(BF16),
                  expert_w2[layer].astype(BF16), MOE_TILE)
    tm_c = 256
    pos_tiles = pos.reshape(t // tm_c, tm_c, 2).transpose(0, 2, 1).reshape(2 * t)
    out = _combine(pos_tiles, h_mid, route, norm_final_g.reshape(1, d), yb, tm_c)
    return out.reshape(b, l, d)
```

```python
import functools

import jax
import jax.numpy as jnp
from jax import lax
from jax.experimental import pallas as pl
from jax.experimental.pallas import tpu as pltpu

F32 = jnp.float32
BF16 = jnp.bfloat16

N_META = 16
Q_BLOCK = 128
HEAD_DIM = 64
SSM_GROUP_CH = 16
SSM_STATE = 64
MOE_GROUPS = 4
EXPERTS_PER_GROUP = 8
N_EXPERTS = MOE_GROUPS * EXPERTS_PER_GROUP
RMS_EPS = 1e-6

LANES = 128
SSM_CHUNK = LANES
MOE_TILE = 256
STICK_CUTOFF = 104.0
VMEM_LIMIT = 52 * 1024 * 1024

_NT = (((1,), (1,)), ((), ()))
_TN = (((0,), (0,)), ((), ()))


def _cparams(sem):
    return pltpu.CompilerParams(dimension_semantics=sem, vmem_limit_bytes=VMEM_LIMIT)


def _inproj_kernel(x_ref, g_ref, wq_ref, wkt_ref, wv_ref, wut_ref, wga_ref, wgb_ref,
                   q_ref, kt_ref, v_ref, ut_ref, ga_ref, gb_ref):
    x = x_ref[...]
    ms = jnp.mean(x * x, axis=-1, keepdims=True)
    xn = (x * lax.rsqrt(ms + RMS_EPS) * g_ref[...]).astype(BF16)
    q_ref[...] = (jnp.dot(xn, wq_ref[...], preferred_element_type=F32) * (HEAD_DIM ** -0.5)).astype(BF16)
    kt_ref[...] = lax.dot_general(wkt_ref[...], xn, _NT, preferred_element_type=F32).astype(BF16)
    v_ref[...] = jnp.dot(xn, wv_ref[...], preferred_element_type=F32).astype(BF16)
    ut_ref[...] = lax.dot_general(wut_ref[...], xn, _NT, preferred_element_type=F32).astype(BF16)
    ga_ref[...] = jax.nn.sigmoid(jnp.dot(xn, wga_ref[...], preferred_element_type=F32)).astype(BF16)
    gb_ref[...] = jax.nn.sigmoid(jnp.dot(xn, wgb_ref[...], preferred_element_type=F32)).astype(BF16)


def _inproj(x2, g, ws, tm):
    m, d = x2.shape
    aw, sw = ws[0].shape[1], ws[3].shape[0]
    row = lambda n: pl.BlockSpec((tm, n), lambda i: (i, 0))
    col = lambda n: pl.BlockSpec((n, tm), lambda i: (0, i))
    full = lambda a: pl.BlockSpec(a.shape, lambda i: (0, 0))
    sds = jax.ShapeDtypeStruct
    return pl.pallas_call(
        _inproj_kernel,
        grid=(m // tm,),
        in_specs=[row(d), full(g)] + [full(w) for w in ws],
        out_specs=[row(aw), col(aw), row(aw), col(sw), row(d), row(d)],
        out_shape=[sds((m, aw), BF16), sds((aw, m), BF16), sds((m, aw), BF16), sds((sw, m), BF16),
                   sds((m, d), BF16), sds((m, d), BF16)],
        compiler_params=_cparams(("parallel",)),
        name="inproj",
    )(x2, g, *ws)


def _attn_kernel(q_ref, kt_ref, v_ref, ktm_ref, vm_ref, tri_ref, o_ref, qm_ref, carry_ref, acc_ref, *, heads, pad):
    i = pl.program_id(1)
    pairs = heads // 2
    carry_ref[...] = jnp.zeros_like(carry_ref)
    acc_ref[...] = jnp.zeros_like(acc_ref)
    lane = lax.broadcasted_iota(jnp.int32, (Q_BLOCK, LANES), 1)
    row = lax.broadcasted_iota(jnp.int32, (Q_BLOCK, LANES), 0)
    low_half = lane < HEAD_DIM
    for p in range(pairs):
        qp = q_ref[0, :, p * LANES:(p + 1) * LANES]
        qm_ref[2 * p] = jnp.where(low_half, qp, jnp.zeros_like(qp))
        qm_ref[2 * p + 1] = jnp.where(low_half, jnp.zeros_like(qp), qp)

    def block_step(get_kt, get_v, valid):
        kts = [get_kt(p) for p in range(pairs)]
        vs = [get_v(p) for p in range(pairs)]
        zs = [jnp.dot(qm_ref[h], kts[h // 2], preferred_element_type=F32) for h in range(heads)]
        cats = []
        for h in range(heads):
            z = zs[h]
            log_keep = -(jnp.maximum(z, 0.0) + jnp.log(1.0 + jnp.exp(-jnp.abs(z))))
            if valid is not None:
                log_keep = jnp.where(valid, log_keep, 0.0)
            hi = log_keep.astype(BF16)
            lo = (log_keep - hi.astype(F32)).astype(BF16)
            cats.append(jnp.concatenate([hi, lo], axis=1))
        sums = [jnp.dot(cats[h], tri_ref[...], preferred_element_type=F32) for h in range(heads)]
        top = None
        for h in range(heads):
            carry = carry_ref[h]
            w = jnp.exp(zs[h] + sums[h][:, :LANES] + carry)
            if valid is not None:
                w = jnp.where(valid, w, 0.0)
            acc_ref[h] += jnp.dot(w.astype(BF16), vs[h // 2], preferred_element_type=F32)
            carry = carry + sums[h][:, LANES:]
            carry_ref[h] = carry
            top = carry if top is None else jnp.maximum(top, carry)
        return jnp.max(top)

    off_d = pl.multiple_of(i * Q_BLOCK, Q_BLOCK)
    top0 = block_step(lambda p: kt_ref[p * LANES:(p + 1) * LANES, pl.ds(off_d, Q_BLOCK)],
                      lambda p: v_ref[0, pl.ds(off_d, Q_BLOCK), p * LANES:(p + 1) * LANES],
                      lane < row)

    def cond(state):
        kb, top = state
        return jnp.logical_and(kb >= 0, top > -STICK_CUTOFF)

    def body(state):
        kb, _ = state
        off = pl.multiple_of(kb * Q_BLOCK, Q_BLOCK)
        top = block_step(lambda p: kt_ref[p * LANES:(p + 1) * LANES, pl.ds(off, Q_BLOCK)],
                         lambda p: v_ref[0, pl.ds(off, Q_BLOCK), p * LANES:(p + 1) * LANES],
                         None)
        return kb - 1, top

    _, top1 = lax.while_loop(cond, body, (i - 1, top0))

    @pl.when(top1 > -STICK_CUTOFF)
    def _():
        block_step(lambda p: ktm_ref[p * LANES:(p + 1) * LANES, :],
                   lambda p: vm_ref[:, p * LANES:(p + 1) * LANES],
                   lane >= pad)

    for p in range(pairs):
        o_ref[0, :, p * LANES:(p + 1) * LANES] = jnp.where(low_half, acc_ref[2 * p], acc_ref[2 * p + 1]).astype(o_ref.dtype)


def _attention(q, kt, v, kt_meta, v_meta, b, heads, pad):
    t, aw = q.shape
    l = t // b
    r = lax.broadcasted_iota(jnp.int32, (2 * LANES, 2 * LANES), 0) % LANES
    c = lax.broadcasted_iota(jnp.int32, (2 * LANES, 2 * LANES), 1)
    tri = jnp.where(jnp.logical_or(c >= LANES, r >= c), 1.0, 0.0).astype(BF16)
    once = pl.Buffered(1)
    out = pl.pallas_call(
        functools.partial(_attn_kernel, heads=heads, pad=pad),
        grid=(b, l // Q_BLOCK),
        in_specs=[
            pl.BlockSpec((1, Q_BLOCK, aw), lambda bi, i: (bi, i, 0)),
            pl.BlockSpec((aw, l), lambda bi, i: (0, bi), pipeline_mode=once),
            pl.BlockSpec((1, l, aw), lambda bi, i: (bi, 0, 0), pipeline_mode=once),
            pl.BlockSpec((aw, Q_BLOCK), lambda bi, i: (0, 0)),
            pl.BlockSpec((Q_BLOCK, aw), lambda bi, i: (0, 0)),
            pl.BlockSpec((2 * LANES, 2 * LANES), lambda bi, i: (0, 0)),
        ],
        out_specs=pl.BlockSpec((1, Q_BLOCK, aw), lambda bi, i: (bi, i, 0)),
        out_shape=jax.ShapeDtypeStruct((b, l, aw), BF16),
        scratch_shapes=[pltpu.VMEM((heads, Q_BLOCK, LANES), BF16), pltpu.VMEM((heads, Q_BLOCK, LANES), F32),
                        pltpu.VMEM((heads, Q_BLOCK, LANES), F32)],
        compiler_params=_cparams(("parallel", "arbitrary")),
        name="stick_attn",
    )(q.reshape(b, l, aw), kt, v.reshape(b, l, aw), kt_meta, v_meta, tri)
    return out.reshape(t, aw)


def _ssm_tables(lam_re, lam_im, log_dt, b_re, b_im, c_re, c_im, d_skip):
    hp = lax.Precision.HIGHEST
    tc = SSM_CHUNK
    g, p = lam_re.shape
    ch = b_re.shape[-1]
    dt = jnp.exp(log_dt)[:, None]
    mag = jnp.exp(lam_re * dt)
    lb_re, lb_im = mag * jnp.cos(lam_im * dt), mag * jnp.sin(lam_im * dt)
    nr, ni = lb_re - 1.0, lb_im
    den = lam_re * lam_re + lam_im * lam_im
    f_re = (nr * lam_re + ni * lam_im) / den
    f_im = (ni * lam_re - nr * lam_im) / den
    bb_re = f_re[:, :, None] * b_re - f_im[:, :, None] * b_im
    bb_im = f_re[:, :, None] * b_im + f_im[:, :, None] * b_re
    pw_re, pw_im = jnp.ones((1, g, p), F32), jnp.zeros((1, g, p), F32)
    s_re, s_im = lb_re, lb_im
    while pw_re.shape[0] < tc + 1:
        n_re = pw_re * s_re - pw_im * s_im
        n_im = pw_re * s_im + pw_im * s_re
        pw_re, pw_im = jnp.concatenate([pw_re, n_re]), jnp.concatenate([pw_im, n_im])
        s_re, s_im = s_re * s_re - s_im * s_im, 2.0 * s_re * s_im
    pw_re, pw_im = pw_re[:tc + 1], pw_im[:tc + 1]
    cl_re = c_re[None] * pw_re[:, :, None, :] - c_im[None] * pw_im[:, :, None, :]
    cl_im = c_re[None] * pw_im[:, :, None, :] + c_im[None] * pw_re[:, :, None, :]
    kern = (jnp.einsum('tgop,gpi->giot', cl_re[:tc], bb_re, precision=hp)
            - jnp.einsum('tgop,gpi->giot', cl_im[:tc], bb_im, precision=hp))
    kern = kern.at[:, :, :, 0].add(jnp.eye(ch, dtype=F32)[None] * d_skip.reshape(g, ch, 1))
    rp_re, rp_im = pw_re[:tc][::-1], pw_im[:tc][::-1]
    st_re = rp_re[:, :, :, None] * bb_re[None] - rp_im[:, :, :, None] * bb_im[None]
    st_im = rp_re[:, :, :, None] * bb_im[None] + rp_im[:, :, :, None] * bb_re[None]
    half = (jnp.arange(g) % 2)[:, None, None]

    def place_cols(m):
        z = jnp.zeros_like(m)
        return jnp.where(half == 0, jnp.concatenate([m, z], axis=2), jnp.concatenate([z, m], axis=2))

    ms_re = place_cols(st_re.transpose(1, 3, 0, 2).reshape(g, ch * tc, p)).astype(BF16)
    ms_im = place_cols(st_im.transpose(1, 3, 0, 2).reshape(g, ch * tc, p)).astype(BF16)
    mo_re = place_cols(cl_re[1:].transpose(1, 2, 0, 3).reshape(g, ch * tc, p)).transpose(0, 2, 1).astype(BF16)
    mo_im = place_cols(-cl_im[1:].transpose(1, 2, 0, 3).reshape(g, ch * tc, p)).transpose(0, 2, 1).astype(BF16)
    dec_re = pw_re[tc].reshape(g // 2, 1, 2 * p)
    dec_im = pw_im[tc].reshape(g // 2, 1, 2 * p)
    return kern, ms_re, ms_im, mo_re, mo_im, dec_re, dec_im


def _ssm_kernel(u_ref, um_ref, k_ref, msr_ref, msi_ref, mor_ref, moi_ref, dre_ref, dim_ref, y_ref,
                toep_ref, sre_ref, sim_ref, xre_ref, xim_ref, *, batch):
    ch, tc = SSM_GROUP_CH, SSM_CHUNK
    nch = u_ref.shape[1]
    per_b = nch // batch

    def chunk_rows(ref, gl):
        return jnp.concatenate([ref[gl * ch + ci] for ci in range(ch)], axis=1)

    u = [chunk_rows(u_ref, gl) for gl in range(2)]
    um = [jnp.concatenate([jnp.broadcast_to(um_ref[gl * ch + ci:gl * ch + ci + 1, :], (8, tc)) for ci in range(ch)], axis=1)
          for gl in range(2)]

    def to_state(lhs, ms_ref):
        return (jnp.dot(lhs[0], ms_ref[0], preferred_element_type=F32)
                + jnp.dot(lhs[1], ms_ref[1], preferred_element_type=F32))

    sre_ref[...] = to_state(u, msr_ref)
    sim_ref[...] = to_state(u, msi_ref)
    x0_re = to_state(um, msr_ref)[0:1]
    x0_im = to_state(um, msi_ref)[0:1]
    d_re, d_im = dre_ref[0], dim_ref[0]

    def step(c, xs):
        new = []
        for bi in range(batch):
            x_re, x_im = xs[2 * bi], xs[2 * bi + 1]
            r = bi * per_b + c
            xre_ref[pl.ds(r, 1), :] = x_re
            xim_ref[pl.ds(r, 1), :] = x_im
            new.append(d_re * x_re - d_im * x_im + sre_ref[pl.ds(r, 1), :])
            new.append(d_re * x_im + d_im * x_re + sim_ref[pl.ds(r, 1), :])
        return tuple(new)

    lax.fori_loop(0, per_b, step, (x0_re, x0_im) * batch)
    xs_re = xre_ref[...].astype(BF16)
    xs_im = xim_ref[...].astype(BF16)

    lane = lax.broadcasted_iota(jnp.int32, (tc, tc), 1)
    row = lax.broadcasted_iota(jnp.int32, (tc, tc), 0)
    causal = lane >= row
    for gl in range(2):
        def fill(ci, carry):
            for co in range(ch):
                taps = jnp.broadcast_to(k_ref[gl, ci, pl.ds(co, 1), :], (tc, tc))
                blk = pltpu.roll(taps, 0, 1, stride=1, stride_axis=0)
                blk = jnp.where(causal, blk, 0.0).astype(BF16)
                toep_ref[pl.ds(pl.multiple_of(ci * tc, tc), tc), co * tc:(co + 1) * tc] = blk
            return carry
        lax.fori_loop(0, ch, fill, 0)
        y = jnp.dot(u[gl], toep_ref[...], preferred_element_type=F32)
        y = y + jnp.dot(xs_re, mor_ref[gl], preferred_element_type=F32)
        y = y + jnp.dot(xs_im, moi_ref[gl], preferred_element_type=F32)
        y = jax.nn.gelu(y).astype(y_ref.dtype)
        for co in range(ch):
            y_ref[gl * ch + co] = y[:, co * tc:(co + 1) * tc]


def _ssm(ut, ut_meta, params, batch):
    s, t = ut.shape
    tc, ch, p = SSM_CHUNK, SSM_GROUP_CH, SSM_STATE
    g = s // ch
    nch = t // tc
    kern, ms_re, ms_im, mo_re, mo_im, dec_re, dec_im = _ssm_tables(*params)
    u3 = ut.reshape(s, nch, tc)
    pair3 = lambda a, b_: pl.BlockSpec((2, a, b_), lambda j: (j, 0, 0))
    y3 = pl.pallas_call(
        functools.partial(_ssm_kernel, batch=batch),
        grid=(g // 2,),
        in_specs=[
            pl.BlockSpec((2 * ch, nch, tc), lambda j: (j, 0, 0)),
            pl.BlockSpec((2 * ch, tc), lambda j: (j, 0)),
            pl.BlockSpec((2, ch, ch, tc), lambda j: (j, 0, 0, 0)),
            pair3(ch * tc, 2 * p), pair3(ch * tc, 2 * p), pair3(2 * p, ch * tc), pair3(2 * p, ch * tc),
            pl.BlockSpec((1, 1, 2 * p), lambda j: (j, 0, 0)), pl.BlockSpec((1, 1, 2 * p), lambda j: (j, 0, 0)),
        ],
        out_specs=pl.BlockSpec((2 * ch, nch, tc), lambda j: (j, 0, 0)),
        out_shape=jax.ShapeDtypeStruct((s, nch, tc), BF16),
        scratch_shapes=[pltpu.VMEM((ch * tc, ch * tc), BF16)] + [pltpu.VMEM((nch, 2 * p), F32)] * 4,
        compiler_params=_cparams(("parallel",)), name="ssm_chunked",
    )(u3, ut_meta, kern, ms_re, ms_im, mo_re, mo_im, dec_re, dec_im)
    return y3.reshape(s, t)


RANK_BITS = 16


def _merge_kernel(x_ref, a_ref, yt_ref, ga_ref, gb_ref, gwt_ref, gbias_ref, wpa_ref, wpb_ref, wo_ref, gn_ref,
                  wr_ref, br_ref, ltri_ref, h_ref, hn_ref, route_ref, count_ref, run_ref):
    @pl.when(pl.program_id(0) == 0)
    def _():
        run_ref[...] = jnp.zeros_like(run_ref)

    yt = yt_ref[...]
    gate = jax.nn.sigmoid(jnp.dot(gwt_ref[...], yt, preferred_element_type=F32) + gbias_ref[...])
    ssm_t = (yt.astype(F32) * gate).astype(BF16)
    merged = (ga_ref[...].astype(F32) * jnp.dot(a_ref[...], wpa_ref[...], preferred_element_type=F32)
              + gb_ref[...].astype(F32) * lax.dot_general(ssm_t, wpb_ref[...], _TN, preferred_element_type=F32))
    h = x_ref[...] + jnp.dot(merged.astype(BF16), wo_ref[...], preferred_element_type=F32)
    h_ref[...] = h
    hn = h * lax.rsqrt(jnp.mean(h * h, axis=-1, keepdims=True) + RMS_EPS) * gn_ref[...]
    hn_ref[...] = hn
    logits = jnp.dot(hn, wr_ref[...], preferred_element_type=F32, precision=lax.Precision.HIGHEST) + br_ref[...]
    lane = lax.broadcasted_iota(jnp.int32, logits.shape, 1)
    neg = jnp.full_like(logits, -jnp.inf)
    big = jnp.int32(LANES)
    is_grp = jnp.logical_and(lane >= N_EXPERTS, lane < N_EXPERTS + MOE_GROUPS)
    glog = jnp.where(is_grp, logits, neg)
    gmax = jnp.max(glog, axis=-1, keepdims=True)
    g_top = jnp.min(jnp.where(glog == gmax, lane, big), axis=-1, keepdims=True) - N_EXPERTS
    p_top = 1.0 / jnp.sum(jnp.exp(glog - gmax), axis=-1, keepdims=True)
    in_grp = (lane // EXPERTS_PER_GROUP) == g_top
    elog = jnp.where(jnp.logical_and(in_grp, lane < N_EXPERTS), logits, neg)
    e1 = jnp.max(elog, axis=-1, keepdims=True)
    i1 = jnp.min(jnp.where(elog == e1, lane, big), axis=-1, keepdims=True)
    elog2 = jnp.where(lane == i1, neg, elog)
    e2 = jnp.max(elog2, axis=-1, keepdims=True)
    i2 = jnp.min(jnp.where(elog2 == e2, lane, big), axis=-1, keepdims=True)
    t = jnp.exp(e2 - e1)
    w1 = p_top / (1.0 + t)
    w2 = p_top * t / (1.0 + t)
    hit1, hit2 = lane == i1, lane == i2
    chosen = jnp.where(jnp.logical_or(hit1, hit2), 1.0, 0.0)
    before = jnp.dot(ltri_ref[...], chosen.astype(BF16), preferred_element_type=F32) + run_ref[0:1, :]
    r1 = jnp.sum(jnp.where(hit1, before, 0.0), axis=-1, keepdims=True)
    r2 = jnp.sum(jnp.where(hit2, before, 0.0), axis=-1, keepdims=True)
    run_ref[...] = run_ref[...] + jnp.sum(chosen, axis=0, keepdims=True)
    count_ref[...] = run_ref[...]
    scale = float(2 ** RANK_BITS)
    route = jnp.where(lane == 0, i1.astype(F32) * scale + r1,
                      jnp.where(lane == 1, i2.astype(F32) * scale + r2,
                                jnp.where(lane == 2, w1, jnp.where(lane == 3, w2, 0.0))))
    route_ref[...] = route


def _merge(x2, attn, yt, ga, gb, glu_wt, glu_b, wpa, wpb, wo, gn, wr, br, tm):
    m, d = x2.shape
    aw, sw = attn.shape[1], yt.shape[0]
    row = lambda n: pl.BlockSpec((tm, n), lambda i: (i, 0))
    full = lambda a: pl.BlockSpec(a.shape, lambda i: (0, 0))
    ltri = jnp.where(lax.broadcasted_iota(jnp.int32, (tm, tm), 0) > lax.broadcasted_iota(jnp.int32, (tm, tm), 1),
                     1.0, 0.0).astype(BF16)
    ws = (glu_wt, glu_b, wpa, wpb, wo, gn, wr, br, ltri)
    return pl.pallas_call(
        _merge_kernel, grid=(m // tm,),
        in_specs=[row(d), row(aw), pl.BlockSpec((sw, tm), lambda i: (0, i)), row(d), row(d)] + [full(a) for a in ws],
        out_specs=[row(d), row(d), row(LANES), pl.BlockSpec((8, LANES), lambda i: (0, 0))],
        out_shape=[jax.ShapeDtypeStruct((m, d), F32), jax.ShapeDtypeStruct((m, d), F32),
                   jax.ShapeDtypeStruct((m, LANES), F32), jax.ShapeDtypeStruct((8, LANES), F32)],
        scratch_shapes=[pltpu.VMEM((8, LANES), F32)],
        compiler_params=_cparams(("arbitrary",)), name="merge_router",
    )(x2, attn, yt, ga, gb, *ws)


def _start_row_gather(src_hbm, idx_ref, base, stride, buf, row0, sem, n):
    for r in range(n):
        row = idx_ref[base + stride * r]
        pltpu.make_async_copy(src_hbm.at[pl.ds(row, 1), :], buf.at[pl.ds(row0 + r, 1), :], sem).start(priority=r % 2)


def _wait_rows(src_hbm, buf, sem):
    pltpu.make_async_copy(src_hbm.at[pl.ds(0, buf.shape[0]), :], buf, sem).wait()


def _expert_kernel(blk_e_ref, src_ref, hn_hbm, w1_ref, w3_ref, w2_ref, y_ref, xbuf, sems, *, tm):
    j = pl.program_id(0)
    last = pl.num_programs(0) - 1
    slot = lax.rem(j, 2)

    @pl.when(j == 0)
    def _():
        _start_row_gather(hn_hbm, src_ref, 0, 1, xbuf.at[0], 0, sems.at[0], tm)

    _wait_rows(hn_hbm, xbuf.at[slot], sems.at[slot])
    nxt = jnp.minimum(j + 1, last)
    for s in range(2):
        @pl.when(slot == s)
        def _():
            _start_row_gather(hn_hbm, src_ref, nxt * tm, 1, xbuf.at[1 - s], 0, sems.at[1 - s], tm)
            x = xbuf[s].astype(BF16)
            a = jnp.dot(x, w1_ref[0], preferred_element_type=F32)
            b = jnp.dot(x, w3_ref[0], preferred_element_type=F32)
            hdn = (a * jax.nn.sigmoid(a) * b).astype(BF16)
            y_ref[...] = jnp.dot(hdn, w2_ref[0], preferred_element_type=F32)

    @pl.when(j == last)
    def _():
        _wait_rows(hn_hbm, xbuf.at[1 - slot], sems.at[1 - slot])


def _experts(hn, blk_e, slot_src, w1, w3, w2, tm):
    t, d = hn.shape
    e, _, ff = w1.shape
    nb = blk_e.shape[0]
    wspec = lambda r, c: pl.BlockSpec((1, r, c), lambda j, be, ss: (be[j], 0, 0))
    return pl.pallas_call(
        functools.partial(_expert_kernel, tm=tm),
        grid_spec=pltpu.PrefetchScalarGridSpec(
            num_scalar_prefetch=2, grid=(nb,),
            in_specs=[pl.BlockSpec(memory_space=pl.ANY), wspec(d, ff), wspec(d, ff), wspec(ff, d)],
            out_specs=pl.BlockSpec((tm, d), lambda j, be, ss: (j, 0)),
            scratch_shapes=[pltpu.VMEM((2, tm, d), F32), pltpu.SemaphoreType.DMA((2,))],
        ),
        out_shape=jax.ShapeDtypeStruct((nb * tm, d), F32),
        compiler_params=_cparams(("arbitrary",)), name="expert_ffn",
    )(blk_e, slot_src, hn, w1, w3, w2)


def _combine_kernel(pos_ref, h_ref, route_ref, gf_ref, yb_hbm, o_ref, ybuf, sems, *, tm):
    i = pl.program_id(0)
    last = pl.num_programs(0) - 1
    slot = lax.rem(i, 2)

    def start(tile, s):
        for c in range(2):
            _start_row_gather(yb_hbm, pos_ref, tile * 2 * tm + c, 2, ybuf.at[s], c * tm, sems.at[s], tm)

    @pl.when(i == 0)
    def _():
        start(0, 0)

    _wait_rows(yb_hbm, ybuf.at[slot], sems.at[slot])
    nxt = jnp.minimum(i + 1, last)
    for s in range(2):
        @pl.when(slot == s)
        def _():
            start(nxt, 1 - s)
            route = route_ref[...]
            w1 = route[:, 2:3]
            w2 = route[:, 3:4]
            h = h_ref[...] + w1 * ybuf[s, :tm, :] + w2 * ybuf[s, tm:, :]
            o_ref[...] = h * lax.rsqrt(jnp.mean(h * h, axis=-1, keepdims=True) + RMS_EPS) * gf_ref[...]

    @pl.when(i == last)
    def _():
        _wait_rows(yb_hbm, ybuf.at[1 - slot], sems.at[1 - slot])


def _combine(pos, h, route, gf, yb, tm):
    t, d = h.shape
    return pl.pallas_call(
        functools.partial(_combine_kernel, tm=tm),
        grid_spec=pltpu.PrefetchScalarGridSpec(
            num_scalar_prefetch=1, grid=(t // tm,),
            in_specs=[pl.BlockSpec((tm, d), lambda i, p: (i, 0)), pl.BlockSpec((tm, LANES), lambda i, p: (i, 0)),
                      pl.BlockSpec((1, d), lambda i, p: (0, 0)), pl.BlockSpec(memory_space=pl.ANY)],
            out_specs=pl.BlockSpec((tm, d), lambda i, p: (i, 0)),
            scratch_shapes=[pltpu.VMEM((2, 2 * tm, d), F32), pltpu.SemaphoreType.DMA((2,))],
        ),
        out_shape=jax.ShapeDtypeStruct((t, d), F32),
        compiler_params=_cparams(("arbitrary",)), name="moe_combine",
    )(pos, h, route, gf, yb)


SLOT_UNROLL = 8


def _slot_kernel(code_ref, seg_ref, pos_ref, src_ref):
    cap = src_ref.shape[0]
    mask = (1 << RANK_BITS) - 1

    def fill(g, carry):
        for k in range(SLOT_UNROLL):
            i = g * SLOT_UNROLL + k
            code = code_ref[i]
            p = seg_ref[lax.shift_right_logical(code, RANK_BITS)] + jnp.bitwise_and(code, mask)
            pos_ref[i] = p
            src_ref[p] = lax.shift_right_logical(i, 1)
        return carry
    lax.fori_loop(0, seg_ref[3 * N_EXPERTS], fill, 0)

    def zero(s, carry):
        src_ref[s] = 0
        return carry

    def zero_run(e, carry):
        return lax.fori_loop(seg_ref[N_EXPERTS + e], seg_ref[2 * N_EXPERTS + e], zero, carry)
    lax.fori_loop(0, N_EXPERTS, zero_run, 0)
    lax.fori_loop(seg_ref[3 * N_EXPERTS - 1], cap, zero, 0)


def _dispatch_plan(route, counts, tm):
    t = route.shape[0]
    n = 2 * t
    nb = (n + N_EXPERTS * (tm - 1)) // tm + 1
    cap = nb * tm
    cnt = counts[0, :N_EXPERTS].astype(jnp.int32)
    padded = ((cnt + tm - 1) // tm) * tm
    pend = jnp.cumsum(padded)
    pstart = pend - padded
    seg = jnp.concatenate([pstart, pstart + cnt, pend, jnp.full((1,), n // SLOT_UNROLL)]).astype(jnp.int32)
    blk_start = jnp.arange(nb, dtype=jnp.int32) * tm
    blk_e = jnp.minimum(jnp.sum((blk_start[:, None] >= pend[None, :]).astype(jnp.int32), axis=1), N_EXPERTS - 1)
    code = route[:, :2].astype(jnp.int32).reshape(n)
    smem = pl.BlockSpec(memory_space=pltpu.SMEM)
    pos, slot_src = pl.pallas_call(
        _slot_kernel,
        grid_spec=pltpu.PrefetchScalarGridSpec(num_scalar_prefetch=2, grid=(1,), in_specs=[], out_specs=[smem, smem]),
        out_shape=[jax.ShapeDtypeStruct((n,), jnp.int32), jax.ShapeDtypeStruct((cap,), jnp.int32)],
        compiler_params=_cparams(("arbitrary",)), name="slot_index",
    )(code, seg)
    return blk_e.astype(jnp.int32), pos, slot_src


def kernel(x, meta_tokens, norm_mix_g, w_in, ssm_lambda_re, ssm_lambda_im, ssm_log_dt, ssm_b_re, ssm_b_im,
           ssm_c_re, ssm_c_im, ssm_d, ssm_glu_w, ssm_glu_b, w_branch_attn, w_branch_ssm, w_out, norm_ffn_g,
           router_group_w, router_group_b, router_expert_w, router_expert_b, expert_w1, expert_w3, expert_w2,
           norm_final_g):
    b, l, d = x.shape
    depth = w_in.shape[0]
    aw = w_branch_attn.shape[1]
    sw = w_branch_ssm.shape[1]
    heads = aw // HEAD_DIM
    t = b * l
    pad = (-N_META) % Q_BLOCK
    assert depth == 1 and l % Q_BLOCK == 0 and pad + N_META == Q_BLOCK and SSM_CHUNK == Q_BLOCK
    layer = 0
    h_real = x.reshape(t, d)
    h_meta = meta_tokens.astype(x.dtype)
    tm_tok = 512

    wi = w_in[layer].astype(BF16)
    o = 3 * aw + sw
    ws = (wi[:, :aw], wi[:, aw:2 * aw].T, wi[:, 2 * aw:3 * aw], wi[:, 3 * aw:o].T, wi[:, o:o + d], wi[:, o + d:])
    g_mix = norm_mix_g[layer].reshape(1, d)
    q, kt, v, ut, ga, gb = _inproj(h_real, g_mix, ws, tm_tok)
    _, kt_m, v_m, ut_m, _, _ = _inproj(h_meta, g_mix, ws, N_META)
    kt_meta = jnp.concatenate([jnp.zeros((aw, pad), BF16), kt_m], axis=1)
    v_meta = jnp.concatenate([jnp.zeros((pad, aw), BF16), v_m], axis=0)
    ut_meta = jnp.concatenate([jnp.zeros((sw, pad), BF16), ut_m], axis=1)

    attn = _attention(q, kt, v, kt_meta, v_meta, b, heads, pad)

    ssm_params = (ssm_lambda_re[layer], ssm_lambda_im[layer], ssm_log_dt[layer], ssm_b_re[layer], ssm_b_im[layer],
                  ssm_c_re[layer], ssm_c_im[layer], ssm_d[layer])
    yt = _ssm(ut, ut_meta, ssm_params, b)

    wr = jnp.zeros((d, LANES), F32)
    wr = wr.at[:, :N_EXPERTS].set(router_expert_w[layer]).at[:, N_EXPERTS:N_EXPERTS + MOE_GROUPS].set(router_group_w[layer])
    br = jnp.zeros((1, LANES), F32)
    br = br.at[0, :N_EXPERTS].set(router_expert_b[layer]).at[0, N_EXPERTS:N_EXPERTS + MOE_GROUPS].set(router_group_b[layer])
    h_mid, hn, route, counts = _merge(
        h_real, attn, yt, ga, gb, ssm_glu_w[layer].T.astype(BF16), ssm_glu_b[layer].reshape(sw, 1).astype(F32),
        w_branch_attn[layer].astype(BF16), w_branch_ssm[layer].astype(BF16), w_out[layer].astype(BF16),
        norm_ffn_g[layer].reshape(1, d), wr, br, tm_tok)

    blk_e, pos, slot_src = _dispatch_plan(route, counts, MOE_TILE)
    yb = _experts(hn, blk_e, slot_src, expert_w1[layer].astype(BF16), expert_w3[layer].astype(BF16),
                  expert_w2[layer].astype(BF16), MOE_TILE)
    out = _combine(pos, h_mid, route, norm_final_g.reshape(1, d), yb, 256)
    return out.reshape(b, l, d)
```

```python
import functools

import jax
import jax.numpy as jnp
from jax import lax
from jax.experimental import pallas as pl
from jax.experimental.pallas import tpu as pltpu

F32 = jnp.float32
BF16 = jnp.bfloat16

N_META = 16
Q_BLOCK = 128
HEAD_DIM = 64
SSM_GROUP_CH = 16
SSM_STATE = 64
MOE_GROUPS = 4
EXPERTS_PER_GROUP = 8
N_EXPERTS = MOE_GROUPS * EXPERTS_PER_GROUP
RMS_EPS = 1e-6

LANES = 128
SSM_CHUNK = LANES
MOE_TILE = 256
STICK_CUTOFF = 104.0
VMEM_LIMIT = 52 * 1024 * 1024

_NT = (((1,), (1,)), ((), ()))
_TN = (((0,), (0,)), ((), ()))


def _cparams(sem):
    return pltpu.CompilerParams(dimension_semantics=sem, vmem_limit_bytes=VMEM_LIMIT)


def _inproj_kernel(x_ref, g_ref, wq_ref, wkt_ref, wv_ref, wut_ref, wga_ref, wgb_ref,
                   q_ref, kt_ref, v_ref, ut_ref, ga_ref, gb_ref):
    x = x_ref[...]
    ms = jnp.mean(x * x, axis=-1, keepdims=True)
    xn = (x * lax.rsqrt(ms + RMS_EPS) * g_ref[...]).astype(BF16)
    q_ref[...] = (jnp.dot(xn, wq_ref[...], preferred_element_type=F32) * (HEAD_DIM ** -0.5)).astype(BF16)
    kt_ref[...] = lax.dot_general(wkt_ref[...], xn, _NT, preferred_element_type=F32).astype(BF16)
    v_ref[...] = jnp.dot(xn, wv_ref[...], preferred_element_type=F32).astype(BF16)
    ut_ref[...] = lax.dot_general(wut_ref[...], xn, _NT, preferred_element_type=F32).astype(BF16)
    ga_ref[...] = jax.nn.sigmoid(jnp.dot(xn, wga_ref[...], preferred_element_type=F32)).astype(BF16)
    gb_ref[...] = jax.nn.sigmoid(jnp.dot(xn, wgb_ref[...], preferred_element_type=F32)).astype(BF16)


def _inproj(x2, g, ws, tm):
    m, d = x2.shape
    aw, sw = ws[0].shape[1], ws[3].shape[0]
    row = lambda n: pl.BlockSpec((tm, n), lambda i: (i, 0))
    col = lambda n: pl.BlockSpec((n, tm), lambda i: (0, i))
    full = lambda a: pl.BlockSpec(a.shape, lambda i: (0, 0))
    sds = jax.ShapeDtypeStruct
    return pl.pallas_call(
        _inproj_kernel,
        grid=(m // tm,),
        in_specs=[row(d), full(g)] + [full(w) for w in ws],
        out_specs=[row(aw), col(aw), row(aw), col(sw), row(d), row(d)],
        out_shape=[sds((m, aw), BF16), sds((aw, m), BF16), sds((m, aw), BF16), sds((sw, m), BF16),
                   sds((m, d), BF16), sds((m, d), BF16)],
        compiler_params=_cparams(("parallel",)),
        name="inproj",
    )(x2, g, *ws)


def _attn_kernel(q_ref, kt_ref, v_ref, ktm_ref, vm_ref, tri_ref, o_ref, qm_ref, carry_ref, acc_ref, *, heads, pad):
    i = pl.program_id(1)
    pairs = heads // 2
    carry_ref[...] = jnp.zeros_like(carry_ref)
    acc_ref[...] = jnp.zeros_like(acc_ref)
    lane = lax.broadcasted_iota(jnp.int32, (Q_BLOCK, LANES), 1)
    row = lax.broadcasted_iota(jnp.int32, (Q_BLOCK, LANES), 0)
    low_half = lane < HEAD_DIM
    for p in range(pairs):
        qp = q_ref[0, :, p * LANES:(p + 1) * LANES]
        qm_ref[2 * p] = jnp.where(low_half, qp, jnp.zeros_like(qp))
        qm_ref[2 * p + 1] = jnp.where(low_half, jnp.zeros_like(qp), qp)

    def block_step(get_kt, get_v, valid):
        kts = [get_kt(p) for p in range(pairs)]
        vs = [get_v(p) for p in range(pairs)]
        zs = [jnp.dot(qm_ref[h], kts[h // 2], preferred_element_type=F32) for h in range(heads)]
        cats = []
        for h in range(heads):
            z = zs[h]
            log_keep = -(jnp.maximum(z, 0.0) + jnp.log(1.0 + jnp.exp(-jnp.abs(z))))
            if valid is not None:
                log_keep = jnp.where(valid, log_keep, 0.0)
            hi = log_keep.astype(BF16)
            lo = (log_keep - hi.astype(F32)).astype(BF16)
            cats.append(jnp.concatenate([hi, lo], axis=1))
        sums = [jnp.dot(cats[h], tri_ref[...], preferred_element_type=F32) for h in range(heads)]
        top = None
        for h in range(heads):
            carry = carry_ref[h]
            w = jnp.exp(zs[h] + sums[h][:, :LANES] + carry)
            if valid is not None:
                w = jnp.where(valid, w, 0.0)
            acc_ref[h] += jnp.dot(w.astype(BF16), vs[h // 2], preferred_element_type=F32)
            carry = carry + sums[h][:, LANES:]
            carry_ref[h] = carry
            top = carry if top is None else jnp.maximum(top, carry)
        return jnp.max(top)

    off_d = pl.multiple_of(i * Q_BLOCK, Q_BLOCK)
    top0 = block_step(lambda p: kt_ref[p * LANES:(p + 1) * LANES, pl.ds(off_d, Q_BLOCK)],
                      lambda p: v_ref[0, pl.ds(off_d, Q_BLOCK), p * LANES:(p + 1) * LANES],
                      lane < row)

    def cond(state):
        kb, top = state
        return jnp.logical_and(kb >= 0, top > -STICK_CUTOFF)

    def body(state):
        kb, _ = state
        off = pl.multiple_of(kb * Q_BLOCK, Q_BLOCK)
        top = block_step(lambda p: kt_ref[p * LANES:(p + 1) * LANES, pl.ds(off, Q_BLOCK)],
                         lambda p: v_ref[0, pl.ds(off, Q_BLOCK), p * LANES:(p + 1) * LANES],
                         None)
        return kb - 1, top

    _, top1 = lax.while_loop(cond, body, (i - 1, top0))

    @pl.when(top1 > -STICK_CUTOFF)
    def _():
        block_step(lambda p: ktm_ref[p * LANES:(p + 1) * LANES, :],
                   lambda p: vm_ref[:, p * LANES:(p + 1) * LANES],
                   lane >= pad)

    for p in range(pairs):
        o_ref[0, :, p * LANES:(p + 1) * LANES] = jnp.where(low_half, acc_ref[2 * p], acc_ref[2 * p + 1]).astype(o_ref.dtype)


def _attention(q, kt, v, kt_meta, v_meta, b, heads, pad):
    t, aw = q.shape
    l = t // b
    r = lax.broadcasted_iota(jnp.int32, (2 * LANES, 2 * LANES), 0) % LANES
    c = lax.broadcasted_iota(jnp.int32, (2 * LANES, 2 * LANES), 1)
    tri = jnp.where(jnp.logical_or(c >= LANES, r >= c), 1.0, 0.0).astype(BF16)
    once = pl.Buffered(1)
    out = pl.pallas_call(
        functools.partial(_attn_kernel, heads=heads, pad=pad),
        grid=(b, l // Q_BLOCK),
        in_specs=[
            pl.BlockSpec((1, Q_BLOCK, aw), lambda bi, i: (bi, i, 0)),
            pl.BlockSpec((aw, l), lambda bi, i: (0, bi), pipeline_mode=once),
            pl.BlockSpec((1, l, aw), lambda bi, i: (bi, 0, 0), pipeline_mode=once),
            pl.BlockSpec((aw, Q_BLOCK), lambda bi, i: (0, 0)),
            pl.BlockSpec((Q_BLOCK, aw), lambda bi, i: (0, 0)),
            pl.BlockSpec((2 * LANES, 2 * LANES), lambda bi, i: (0, 0)),
        ],
        out_specs=pl.BlockSpec((1, Q_BLOCK, aw), lambda bi, i: (bi, i, 0)),
        out_shape=jax.ShapeDtypeStruct((b, l, aw), BF16),
        scratch_shapes=[pltpu.VMEM((heads, Q_BLOCK, LANES), BF16), pltpu.VMEM((heads, Q_BLOCK, LANES), F32),
                        pltpu.VMEM((heads, Q_BLOCK, LANES), F32)],
        compiler_params=_cparams(("parallel", "arbitrary")),
        name="stick_attn",
    )(q.reshape(b, l, aw), kt, v.reshape(b, l, aw), kt_meta, v_meta, tri)
    return out.reshape(t, aw)


def _ssm_tables(lam_re, lam_im, log_dt, b_re, b_im, c_re, c_im, d_skip):
    hp = lax.Precision.HIGHEST
    tc = SSM_CHUNK
    g, p = lam_re.shape
    ch = b_re.shape[-1]
    dt = jnp.exp(log_dt)[:, None]
    mag = jnp.exp(lam_re * dt)
    lb_re, lb_im = mag * jnp.cos(lam_im * dt), mag * jnp.sin(lam_im * dt)
    nr, ni = lb_re - 1.0, lb_im
    den = lam_re * lam_re + lam_im * lam_im
    f_re = (nr * lam_re + ni * lam_im) / den
    f_im = (ni * lam_re - nr * lam_im) / den
    bb_re = f_re[:, :, None] * b_re - f_im[:, :, None] * b_im
    bb_im = f_re[:, :, None] * b_im + f_im[:, :, None] * b_re
    pw_re, pw_im = jnp.ones((1, g, p), F32), jnp.zeros((1, g, p), F32)
    s_re, s_im = lb_re, lb_im
    while pw_re.shape[0] < tc + 1:
        n_re = pw_re * s_re - pw_im * s_im
        n_im = pw_re * s_im + pw_im * s_re
        pw_re, pw_im = jnp.concatenate([pw_re, n_re]), jnp.concatenate([pw_im, n_im])
        s_re, s_im = s_re * s_re - s_im * s_im, 2.0 * s_re * s_im
    pw_re, pw_im = pw_re[:tc + 1], pw_im[:tc + 1]
    cl_re = c_re[None] * pw_re[:, :, None, :] - c_im[None] * pw_im[:, :, None, :]
    cl_im = c_re[None] * pw_im[:, :, None, :] + c_im[None] * pw_re[:, :, None, :]
    kern = (jnp.einsum('tgop,gpi->giot', cl_re[:tc], bb_re, precision=hp)
            - jnp.einsum('tgop,gpi->giot', cl_im[:tc], bb_im, precision=hp))
    kern = kern.at[:, :, :, 0].add(jnp.eye(ch, dtype=F32)[None] * d_skip.reshape(g, ch, 1))
    rp_re, rp_im = pw_re[:tc][::-1], pw_im[:tc][::-1]
    st_re = rp_re[:, :, :, None] * bb_re[None] - rp_im[:, :, :, None] * bb_im[None]
    st_im = rp_re[:, :, :, None] * bb_im[None] + rp_im[:, :, :, None] * bb_re[None]
    half = (jnp.arange(g) % 2)[:, None, None]

    def place_cols(m):
        z = jnp.zeros_like(m)
        return jnp.where(half == 0, jnp.concatenate([m, z], axis=2), jnp.concatenate([z, m], axis=2))

    ms_re = place_cols(st_re.transpose(1, 3, 0, 2).reshape(g, ch * tc, p)).astype(BF16)
    ms_im = place_cols(st_im.transpose(1, 3, 0, 2).reshape(g, ch * tc, p)).astype(BF16)
    mo_re = place_cols(cl_re[1:].transpose(1, 2, 0, 3).reshape(g, ch * tc, p)).transpose(0, 2, 1).astype(BF16)
    mo_im = place_cols(-cl_im[1:].transpose(1, 2, 0, 3).reshape(g, ch * tc, p)).transpose(0, 2, 1).astype(BF16)
    dec_re = pw_re[tc].reshape(g // 2, 1, 2 * p)
    dec_im = pw_im[tc].reshape(g // 2, 1, 2 * p)
    return kern, ms_re, ms_im, mo_re, mo_im, dec_re, dec_im


def _ssm_kernel(u_ref, um_ref, k_ref, msr_ref, msi_ref, mor_ref, moi_ref, dre_ref, dim_ref, y_ref,
                toep_ref, sre_ref, sim_ref, xre_ref, xim_ref, *, batch):
    ch, tc = SSM_GROUP_CH, SSM_CHUNK
    nch = u_ref.shape[1]
    per_b = nch // batch

    def chunk_rows(ref, gl):
        return jnp.concatenate([ref[gl * ch + ci] for ci in range(ch)], axis=1)

    u = [chunk_rows(u_ref, gl) for gl in range(2)]
    um = [jnp.concatenate([jnp.broadcast_to(um_ref[gl * ch + ci:gl * ch + ci + 1, :], (8, tc)) for ci in range(ch)], axis=1)
          for gl in range(2)]

    def to_state(lhs, ms_ref):
        return (jnp.dot(lhs[0], ms_ref[0], preferred_element_type=F32)
                + jnp.dot(lhs[1], ms_ref[1], preferred_element_type=F32))

    sre_ref[...] = to_state(u, msr_ref)
    sim_ref[...] = to_state(u, msi_ref)
    x0_re = to_state(um, msr_ref)[0:1]
    x0_im = to_state(um, msi_ref)[0:1]
    d_re, d_im = dre_ref[0], dim_ref[0]

    def step(c, xs):
        new = []
        for bi in range(batch):
            x_re, x_im = xs[2 * bi], xs[2 * bi + 1]
            r = bi * per_b + c
            xre_ref[pl.ds(r, 1), :] = x_re
            xim_ref[pl.ds(r, 1), :] = x_im
            new.append(d_re * x_re - d_im * x_im + sre_ref[pl.ds(r, 1), :])
            new.append(d_re * x_im + d_im * x_re + sim_ref[pl.ds(r, 1), :])
        return tuple(new)

    lax.fori_loop(0, per_b, step, (x0_re, x0_im) * batch)
    xs_re = xre_ref[...].astype(BF16)
    xs_im = xim_ref[...].astype(BF16)

    lane = lax.broadcasted_iota(jnp.int32, (tc, tc), 1)
    row = lax.broadcasted_iota(jnp.int32, (tc, tc), 0)
    causal = lane >= row
    for gl in range(2):
        def fill(ci, carry):
            for co in range(ch):
                taps = jnp.broadcast_to(k_ref[gl, ci, pl.ds(co, 1), :], (tc, tc))
                blk = pltpu.roll(taps, 0, 1, stride=1, stride_axis=0)
                blk = jnp.where(causal, blk, 0.0).astype(BF16)
                toep_ref[pl.ds(pl.multiple_of(ci * tc, tc), tc), co * tc:(co + 1) * tc] = blk
            return carry
        lax.fori_loop(0, ch, fill, 0)
        y = jnp.dot(u[gl], toep_ref[...], preferred_element_type=F32)
        y = y + jnp.dot(xs_re, mor_ref[gl], preferred_element_type=F32)
        y = y + jnp.dot(xs_im, moi_ref[gl], preferred_element_type=F32)
        y = jax.nn.gelu(y).astype(y_ref.dtype)
        for co in range(ch):
            y_ref[gl * ch + co] = y[:, co * tc:(co + 1) * tc]


def _ssm(ut, ut_meta, params, batch):
    s, t = ut.shape
    tc, ch, p = SSM_CHUNK, SSM_GROUP_CH, SSM_STATE
    g = s // ch
    nch = t // tc
    kern, ms_re, ms_im, mo_re, mo_im, dec_re, dec_im = _ssm_tables(*params)
    u3 = ut.reshape(s, nch, tc)
    pair3 = lambda a, b_: pl.BlockSpec((2, a, b_), lambda j: (j, 0, 0))
    y3 = pl.pallas_call(
        functools.partial(_ssm_kernel, batch=batch),
        grid=(g // 2,),
        in_specs=[
            pl.BlockSpec((2 * ch, nch, tc), lambda j: (j, 0, 0)),
            pl.BlockSpec((2 * ch, tc), lambda j: (j, 0)),
            pl.BlockSpec((2, ch, ch, tc), lambda j: (j, 0, 0, 0)),
            pair3(ch * tc, 2 * p), pair3(ch * tc, 2 * p), pair3(2 * p, ch * tc), pair3(2 * p, ch * tc),
            pl.BlockSpec((1, 1, 2 * p), lambda j: (j, 0, 0)), pl.BlockSpec((1, 1, 2 * p), lambda j: (j, 0, 0)),
        ],
        out_specs=pl.BlockSpec((2 * ch, nch, tc), lambda j: (j, 0, 0)),
        out_shape=jax.ShapeDtypeStruct((s, nch, tc), BF16),
        scratch_shapes=[pltpu.VMEM((ch * tc, ch * tc), BF16)] + [pltpu.VMEM((nch, 2 * p), F32)] * 4,
        compiler_params=_cparams(("parallel",)), name="ssm_chunked",
    )(u3, ut_meta, kern, ms_re, ms_im, mo_re, mo_im, dec_re, dec_im)
    return y3.reshape(s, t)


RANK_BITS = 16


def _merge_kernel(x_ref, a_ref, yt_ref, ga_ref, gb_ref, gwt_ref, gbias_ref, wpa_ref, wpb_ref, wo_ref, gn_ref,
                  wr_ref, br_ref, ltri_ref, h_ref, hn_ref, route_ref, count_ref, run_ref):
    @pl.when(pl.program_id(0) == 0)
    def _():
        run_ref[...] = jnp.zeros_like(run_ref)

    yt = yt_ref[...]
    gate = jax.nn.sigmoid(jnp.dot(gwt_ref[...], yt, preferred_element_type=F32) + gbias_ref[...])
    ssm_t = (yt.astype(F32) * gate).astype(BF16)
    merged = (ga_ref[...].astype(F32) * jnp.dot(a_ref[...], wpa_ref[...], preferred_element_type=F32)
              + gb_ref[...].astype(F32) * lax.dot_general(ssm_t, wpb_ref[...], _TN, preferred_element_type=F32))
    h = x_ref[...] + jnp.dot(merged.astype(BF16), wo_ref[...], preferred_element_type=F32)
    h_ref[...] = h
    hn = h * lax.rsqrt(jnp.mean(h * h, axis=-1, keepdims=True) + RMS_EPS) * gn_ref[...]
    hn_ref[...] = hn
    logits = jnp.dot(hn, wr_ref[...], preferred_element_type=F32, precision=lax.Precision.HIGHEST) + br_ref[...]
    lane = lax.broadcasted_iota(jnp.int32, logits.shape, 1)
    neg = jnp.full_like(logits, -jnp.inf)
    big = jnp.int32(LANES)
    is_grp = jnp.logical_and(lane >= N_EXPERTS, lane < N_EXPERTS + MOE_GROUPS)
    glog = jnp.where(is_grp, logits, neg)
    gmax = jnp.max(glog, axis=-1, keepdims=True)
    g_top = jnp.min(jnp.where(glog == gmax, lane, big), axis=-1, keepdims=True) - N_EXPERTS
    p_top = 1.0 / jnp.sum(jnp.exp(glog - gmax), axis=-1, keepdims=True)
    in_grp = (lane // EXPERTS_PER_GROUP) == g_top
    elog = jnp.where(jnp.logical_and(in_grp, lane < N_EXPERTS), logits, neg)
    e1 = jnp.max(elog, axis=-1, keepdims=True)
    i1 = jnp.min(jnp.where(elog == e1, lane, big), axis=-1, keepdims=True)
    elog2 = jnp.where(lane == i1, neg, elog)
    e2 = jnp.max(elog2, axis=-1, keepdims=True)
    i2 = jnp.min(jnp.where(elog2 == e2, lane, big), axis=-1, keepdims=True)
    t = jnp.exp(e2 - e1)
    w1 = p_top / (1.0 + t)
    w2 = p_top * t / (1.0 + t)
    hit1, hit2 = lane == i1, lane == i2
    chosen = jnp.where(jnp.logical_or(hit1, hit2), 1.0, 0.0)
    before = jnp.dot(ltri_ref[...], chosen.astype(BF16), preferred_element_type=F32) + run_ref[0:1, :]
    r1 = jnp.sum(jnp.where(hit1, before, 0.0), axis=-1, keepdims=True)
    r2 = jnp.sum(jnp.where(hit2, before, 0.0), axis=-1, keepdims=True)
    run_ref[...] = run_ref[...] + jnp.sum(chosen, axis=0, keepdims=True)
    count_ref[...] = run_ref[...]
    scale = float(2 ** RANK_BITS)
    route = jnp.where(lane == 0, i1.astype(F32) * scale + r1,
                      jnp.where(lane == 1, i2.astype(F32) * scale + r2,
                                jnp.where(lane == 2, w1, jnp.where(lane == 3, w2, 0.0))))
    route_ref[...] = route


def _merge(x2, attn, yt, ga, gb, glu_wt, glu_b, wpa, wpb, wo, gn, wr, br, tm):
    m, d = x2.shape
    aw, sw = attn.shape[1], yt.shape[0]
    row = lambda n: pl.BlockSpec((tm, n), lambda i: (i, 0))
    full = lambda a: pl.BlockSpec(a.shape, lambda i: (0, 0))
    ltri = jnp.where(lax.broadcasted_iota(jnp.int32, (tm, tm), 0) > lax.broadcasted_iota(jnp.int32, (tm, tm), 1),
                     1.0, 0.0).astype(BF16)
    ws = (glu_wt, glu_b, wpa, wpb, wo, gn, wr, br, ltri)
    return pl.pallas_call(
        _merge_kernel, grid=(m // tm,),
        in_specs=[row(d), row(aw), pl.BlockSpec((sw, tm), lambda i: (0, i)), row(d), row(d)] + [full(a) for a in ws],
        out_specs=[row(d), row(d), row(LANES), pl.BlockSpec((8, LANES), lambda i: (0, 0))],
        out_shape=[jax.ShapeDtypeStruct((m, d), F32), jax.ShapeDtypeStruct((m, d), F32),
                   jax.ShapeDtypeStruct((m, LANES), F32), jax.ShapeDtypeStruct((8, LANES), F32)],
        scratch_shapes=[pltpu.VMEM((8, LANES), F32)],
        compiler_params=_cparams(("arbitrary",)), name="merge_router",
    )(x2, attn, yt, ga, gb, *ws)


def _start_row_gather(src_hbm, row_of, base, stride, buf, row0, sem, n):
    for r in range(n):
        row = row_of(base + stride * r)
        pltpu.make_async_copy(src_hbm.at[pl.ds(row, 1), :], buf.at[pl.ds(row0 + r, 1), :], sem).start(priority=r % 2)


def _wait_rows(src_hbm, buf, sem):
    pltpu.make_async_copy(src_hbm.at[pl.ds(0, buf.shape[0]), :], buf, sem).wait()


def _expert_kernel(blk_e_ref, src_ref, hn_hbm, w1_ref, w3_ref, w2_ref, y_ref, xbuf, sems, *, tm):
    j = pl.program_id(0)
    last = pl.num_programs(0) - 1
    slot = lax.rem(j, 2)

    @pl.when(j == 0)
    def _():
        _start_row_gather(hn_hbm, lambda i: src_ref[i], 0, 1, xbuf.at[0], 0, sems.at[0], tm)

    _wait_rows(hn_hbm, xbuf.at[slot], sems.at[slot])
    nxt = jnp.minimum(j + 1, last)
    for s in range(2):
        @pl.when(slot == s)
        def _():
            _start_row_gather(hn_hbm, lambda i: src_ref[i], nxt * tm, 1, xbuf.at[1 - s], 0, sems.at[1 - s], tm)
            x = xbuf[s].astype(BF16)
            a = jnp.dot(x, w1_ref[0], preferred_element_type=F32)
            b = jnp.dot(x, w3_ref[0], preferred_element_type=F32)
            hdn = (a * jax.nn.sigmoid(a) * b).astype(BF16)
            y_ref[...] = jnp.dot(hdn, w2_ref[0], preferred_element_type=F32)

    @pl.when(j == last)
    def _():
        _wait_rows(hn_hbm, xbuf.at[1 - slot], sems.at[1 - slot])


def _experts(hn, blk_e, slot_src, w1, w3, w2, tm):
    t, d = hn.shape
    e, _, ff = w1.shape
    nb = blk_e.shape[0]
    wspec = lambda r, c: pl.BlockSpec((1, r, c), lambda j, be, ss: (be[j], 0, 0))
    return pl.pallas_call(
        functools.partial(_expert_kernel, tm=tm),
        grid_spec=pltpu.PrefetchScalarGridSpec(
            num_scalar_prefetch=2, grid=(nb,),
            in_specs=[pl.BlockSpec(memory_space=pl.ANY), wspec(d, ff), wspec(d, ff), wspec(ff, d)],
            out_specs=pl.BlockSpec((tm, d), lambda j, be, ss: (j, 0)),
            scratch_shapes=[pltpu.VMEM((2, tm, d), F32), pltpu.SemaphoreType.DMA((2,))],
        ),
        out_shape=jax.ShapeDtypeStruct((nb * tm, d), F32),
        compiler_params=_cparams(("arbitrary",)), name="expert_ffn",
    )(blk_e, slot_src, hn, w1, w3, w2)


def _combine_kernel(code_ref, seg_ref, h_ref, route_ref, gf_ref, yb_hbm, o_ref, ybuf, sems, *, tm):
    i = pl.program_id(0)
    last = pl.num_programs(0) - 1
    slot = lax.rem(i, 2)

    def start(tile, s):
        for c in range(2):
            _start_row_gather(yb_hbm, lambda k: _slot_of(code_ref[k], seg_ref), tile * 2 * tm + c, 2,
                              ybuf.at[s], c * tm, sems.at[s], tm)

    @pl.when(i == 0)
    def _():
        start(0, 0)

    _wait_rows(yb_hbm, ybuf.at[slot], sems.at[slot])
    nxt = jnp.minimum(i + 1, last)
    for s in range(2):
        @pl.when(slot == s)
        def _():
            start(nxt, 1 - s)
            route = route_ref[...]
            w1 = route[:, 2:3]
            w2 = route[:, 3:4]
            h = h_ref[...] + w1 * ybuf[s, :tm, :] + w2 * ybuf[s, tm:, :]
            o_ref[...] = h * lax.rsqrt(jnp.mean(h * h, axis=-1, keepdims=True) + RMS_EPS) * gf_ref[...]

    @pl.when(i == last)
    def _():
        _wait_rows(yb_hbm, ybuf.at[1 - slot], sems.at[1 - slot])


def _combine(code, seg, h, route, gf, yb, tm):
    t, d = h.shape
    return pl.pallas_call(
        functools.partial(_combine_kernel, tm=tm),
        grid_spec=pltpu.PrefetchScalarGridSpec(
            num_scalar_prefetch=2, grid=(t // tm,),
            in_specs=[pl.BlockSpec((tm, d), lambda i, c, s: (i, 0)), pl.BlockSpec((tm, LANES), lambda i, c, s: (i, 0)),
                      pl.BlockSpec((1, d), lambda i, c, s: (0, 0)), pl.BlockSpec(memory_space=pl.ANY)],
            out_specs=pl.BlockSpec((tm, d), lambda i, c, s: (i, 0)),
            scratch_shapes=[pltpu.VMEM((2, 2 * tm, d), F32), pltpu.SemaphoreType.DMA((2,))],
        ),
        out_shape=jax.ShapeDtypeStruct((t, d), F32),
        compiler_params=_cparams(("arbitrary",)), name="moe_combine",
    )(code, seg, h, route, gf, yb)


SLOT_UNROLL = 8


def _slot_of(code, seg_ref):
    return seg_ref[lax.shift_right_logical(code, RANK_BITS)] + jnp.bitwise_and(code, (1 << RANK_BITS) - 1)


def _slot_kernel(code_ref, seg_ref, src_ref):
    cap = src_ref.shape[0]

    def fill(g, carry):
        for k in range(SLOT_UNROLL):
            i = g * SLOT_UNROLL + k
            src_ref[_slot_of(code_ref[i], seg_ref)] = lax.shift_right_logical(i, 1)
        return carry
    lax.fori_loop(0, seg_ref[3 * N_EXPERTS], fill, 0)

    def zero(s, carry):
        src_ref[s] = 0
        return carry

    def zero_run(e, carry):
        return lax.fori_loop(seg_ref[N_EXPERTS + e], seg_ref[2 * N_EXPERTS + e], zero, carry)
    lax.fori_loop(0, N_EXPERTS, zero_run, 0)
    lax.fori_loop(seg_ref[3 * N_EXPERTS - 1], cap, zero, 0)


def _dispatch_plan(route, counts, tm):
    t = route.shape[0]
    n = 2 * t
    nb = (n + N_EXPERTS * (tm - 1)) // tm + 1
    cap = nb * tm
    cnt = counts[0, :N_EXPERTS].astype(jnp.int32)
    padded = ((cnt + tm - 1) // tm) * tm
    pend = jnp.cumsum(padded)
    pstart = pend - padded
    seg = jnp.concatenate([pstart, pstart + cnt, pend, jnp.full((1,), n // SLOT_UNROLL)]).astype(jnp.int32)
    blk_start = jnp.arange(nb, dtype=jnp.int32) * tm
    blk_e = jnp.minimum(jnp.sum((blk_start[:, None] >= pend[None, :]).astype(jnp.int32), axis=1), N_EXPERTS - 1)
    code = route[:, :2].astype(jnp.int32).reshape(n)
    smem = pl.BlockSpec(memory_space=pltpu.SMEM)
    slot_src = pl.pallas_call(
        _slot_kernel,
        grid_spec=pltpu.PrefetchScalarGridSpec(num_scalar_prefetch=2, grid=(1,), in_specs=[], out_specs=smem),
        out_shape=jax.ShapeDtypeStruct((cap,), jnp.int32),
        compiler_params=_cparams(("arbitrary",)), name="slot_index",
    )(code, seg)
    return blk_e.astype(jnp.int32), code, seg, slot_src


def kernel(x, meta_tokens, norm_mix_g, w_in, ssm_lambda_re, ssm_lambda_im, ssm_log_dt, ssm_b_re, ssm_b_im,
           ssm_c_re, ssm_c_im, ssm_d, ssm_glu_w, ssm_glu_b, w_branch_attn, w_branch_ssm, w_out, norm_ffn_g,
           router_group_w, router_group_b, router_expert_w, router_expert_b, expert_w1, expert_w3, expert_w2,
           norm_final_g):
    b, l, d = x.shape
    depth = w_in.shape[0]
    aw = w_branch_attn.shape[1]
    sw = w_branch_ssm.shape[1]
    heads = aw // HEAD_DIM
    t = b * l
    pad = (-N_META) % Q_BLOCK
    assert depth == 1 and l % Q_BLOCK == 0 and pad + N_META == Q_BLOCK and SSM_CHUNK == Q_BLOCK
    layer = 0
    h_real = x.reshape(t, d)
    h_meta = meta_tokens.astype(x.dtype)
    tm_tok = 512

    wi = w_in[layer].astype(BF16)
    o = 3 * aw + sw
    ws = (wi[:, :aw], wi[:, aw:2 * aw].T, wi[:, 2 * aw:3 * aw], wi[:, 3 * aw:o].T, wi[:, o:o + d], wi[:, o + d:])
    g_mix = norm_mix_g[layer].reshape(1, d)
    q, kt, v, ut, ga, gb = _inproj(h_real, g_mix, ws, tm_tok)
    _, kt_m, v_m, ut_m, _, _ = _inproj(h_meta, g_mix, ws, N_META)
    kt_meta = jnp.concatenate([jnp.zeros((aw, pad), BF16), kt_m], axis=1)
    v_meta = jnp.concatenate([jnp.zeros((pad, aw), BF16), v_m], axis=0)
    ut_meta = jnp.concatenate([jnp.zeros((sw, pad), BF16), ut_m], axis=1)

    attn = _attention(q, kt, v, kt_meta, v_meta, b, heads, pad)

    ssm_params = (ssm_lambda_re[layer], ssm_lambda_im[layer], ssm_log_dt[layer], ssm_b_re[layer], ssm_b_im[layer],
                  ssm_c_re[layer], ssm_c_im[layer], ssm_d[layer])
    yt = _ssm(ut, ut_meta, ssm_params, b)

    wr = jnp.zeros((d, LANES), F32)
    wr = wr.at[:, :N_EXPERTS].set(router_expert_w[layer]).at[:, N_EXPERTS:N_EXPERTS + MOE_GROUPS].set(router_group_w[layer])
    br = jnp.zeros((1, LANES), F32)
    br = br.at[0, :N_EXPERTS].set(router_expert_b[layer]).at[0, N_EXPERTS:N_EXPERTS + MOE_GROUPS].set(router_group_b[layer])
    h_mid, hn, route, counts = _merge(
        h_real, attn, yt, ga, gb, ssm_glu_w[layer].T.astype(BF16), ssm_glu_b[layer].reshape(sw, 1).astype(F32),
        w_branch_attn[layer].astype(BF16), w_branch_ssm[layer].astype(BF16), w_out[layer].astype(BF16),
        norm_ffn_g[layer].reshape(1, d), wr, br, tm_tok)

    blk_e, code, seg, slot_src = _dispatch_plan(route, counts, MOE_TILE)
    yb = _experts(hn, blk_e, slot_src, expert_w1[layer].astype(BF16), expert_w3[layer].astype(BF16),
                  expert_w2[layer].astype(BF16), MOE_TILE)
    out = _combine(code, seg, h_mid, route, norm_final_g.reshape(1, d), yb, 256)
    return out.reshape(b, l, d)
```

```python
import functools

import jax
import jax.numpy as jnp
from jax import lax
from jax.experimental import pallas as pl
from jax.experimental.pallas import tpu as pltpu

F32 = jnp.float32
BF16 = jnp.bfloat16

N_META = 16
Q_BLOCK = 128
HEAD_DIM = 64
SSM_GROUP_CH = 16
SSM_STATE = 64
MOE_GROUPS = 4
EXPERTS_PER_GROUP = 8
N_EXPERTS = MOE_GROUPS * EXPERTS_PER_GROUP
RMS_EPS = 1e-6

LANES = 128
SSM_CHUNK = LANES
MOE_TILE = 256
STICK_CUTOFF = 104.0
VMEM_LIMIT = 52 * 1024 * 1024

_NT = (((1,), (1,)), ((), ()))
_TN = (((0,), (0,)), ((), ()))


def _cparams(sem):
    return pltpu.CompilerParams(dimension_semantics=sem, vmem_limit_bytes=VMEM_LIMIT)


def _inproj_kernel(x_ref, g_ref, wq_ref, wkt_ref, wv_ref, wut_ref, wga_ref, wgb_ref,
                   q_ref, kt_ref, v_ref, ut_ref, ga_ref, gb_ref):
    x = x_ref[...]
    ms = jnp.mean(x * x, axis=-1, keepdims=True)
    xn = (x * lax.rsqrt(ms + RMS_EPS) * g_ref[...]).astype(BF16)
    q_ref[...] = (jnp.dot(xn, wq_ref[...], preferred_element_type=F32) * (HEAD_DIM ** -0.5)).astype(BF16)
    kt_ref[...] = lax.dot_general(wkt_ref[...], xn, _NT, preferred_element_type=F32).astype(BF16)
    v_ref[...] = jnp.dot(xn, wv_ref[...], preferred_element_type=F32).astype(BF16)
    ut_ref[...] = lax.dot_general(wut_ref[...], xn, _NT, preferred_element_type=F32).astype(BF16)
    ga_ref[...] = jax.nn.sigmoid(jnp.dot(xn, wga_ref[...], preferred_element_type=F32)).astype(BF16)
    gb_ref[...] = jax.nn.sigmoid(jnp.dot(xn, wgb_ref[...], preferred_element_type=F32)).astype(BF16)


def _inproj(x2, g, ws, tm):
    m, d = x2.shape
    aw, sw = ws[0].shape[1], ws[3].shape[0]
    row = lambda n: pl.BlockSpec((tm, n), lambda i: (i, 0))
    col = lambda n: pl.BlockSpec((n, tm), lambda i: (0, i))
    full = lambda a: pl.BlockSpec(a.shape, lambda i: (0, 0))
    sds = jax.ShapeDtypeStruct
    return pl.pallas_call(
        _inproj_kernel,
        grid=(m // tm,),
        in_specs=[row(d), full(g)] + [full(w) for w in ws],
        out_specs=[row(aw), col(aw), row(aw), col(sw), row(d), row(d)],
        out_shape=[sds((m, aw), BF16), sds((aw, m), BF16), sds((m, aw), BF16), sds((sw, m), BF16),
                   sds((m, d), BF16), sds((m, d), BF16)],
        compiler_params=_cparams(("parallel",)),
        name="inproj",
    )(x2, g, *ws)


def _attn_kernel(q_ref, kt_ref, v_ref, ktm_ref, vm_ref, tri_ref, o_ref, qm_ref, carry_ref, acc_ref, *, heads, pad):
    i = pl.program_id(1)
    pairs = heads // 2
    carry_ref[...] = jnp.zeros_like(carry_ref)
    acc_ref[...] = jnp.zeros_like(acc_ref)
    lane = lax.broadcasted_iota(jnp.int32, (Q_BLOCK, LANES), 1)
    row = lax.broadcasted_iota(jnp.int32, (Q_BLOCK, LANES), 0)
    low_half = lane < HEAD_DIM
    for p in range(pairs):
        qp = q_ref[0, :, p * LANES:(p + 1) * LANES]
        qm_ref[2 * p] = jnp.where(low_half, qp, jnp.zeros_like(qp))
        qm_ref[2 * p + 1] = jnp.where(low_half, jnp.zeros_like(qp), qp)

    def block_step(get_kt, get_v, valid):
        kts = [get_kt(p) for p in range(pairs)]
        vs = [get_v(p) for p in range(pairs)]
        zs = [jnp.dot(qm_ref[h], kts[h // 2], preferred_element_type=F32) for h in range(heads)]
        cats = []
        for h in range(heads):
            z = zs[h]
            log_keep = -(jnp.maximum(z, 0.0) + jnp.log(1.0 + jnp.exp(-jnp.abs(z))))
            if valid is not None:
                log_keep = jnp.where(valid, log_keep, 0.0)
            hi = log_keep.astype(BF16)
            lo = (log_keep - hi.astype(F32)).astype(BF16)
            cats.append(jnp.concatenate([hi, lo], axis=1))
        sums = [jnp.dot(cats[h], tri_ref[...], preferred_element_type=F32) for h in range(heads)]
        top = None
        for h in range(heads):
            carry = carry_ref[h]
            w = jnp.exp(zs[h] + sums[h][:, :LANES] + carry)
            if valid is not None:
                w = jnp.where(valid, w, 0.0)
            acc_ref[h] += jnp.dot(w.astype(BF16), vs[h // 2], preferred_element_type=F32)
            carry = carry + sums[h][:, LANES:]
            carry_ref[h] = carry
            top = carry if top is None else jnp.maximum(top, carry)
        return jnp.max(top)

    off_d = pl.multiple_of(i * Q_BLOCK, Q_BLOCK)
    top0 = block_step(lambda p: kt_ref[p * LANES:(p + 1) * LANES, pl.ds(off_d, Q_BLOCK)],
                      lambda p: v_ref[0, pl.ds(off_d, Q_BLOCK), p * LANES:(p + 1) * LANES],
                      lane < row)

    def cond(state):
        kb, top = state
        return jnp.logical_and(kb >= 0, top > -STICK_CUTOFF)

    def body(state):
        kb, _ = state
        off = pl.multiple_of(kb * Q_BLOCK, Q_BLOCK)
        top = block_step(lambda p: kt_ref[p * LANES:(p + 1) * LANES, pl.ds(off, Q_BLOCK)],
                         lambda p: v_ref[0, pl.ds(off, Q_BLOCK), p * LANES:(p + 1) * LANES],
                         None)
        return kb - 1, top

    _, top1 = lax.while_loop(cond, body, (i - 1, top0))

    @pl.when(top1 > -STICK_CUTOFF)
    def _():
        block_step(lambda p: ktm_ref[p * LANES:(p + 1) * LANES, :],
                   lambda p: vm_ref[:, p * LANES:(p + 1) * LANES],
                   lane >= pad)

    for p in range(pairs):
        o_ref[0, :, p * LANES:(p + 1) * LANES] = jnp.where(low_half, acc_ref[2 * p], acc_ref[2 * p + 1]).astype(o_ref.dtype)


def _attention(q, kt, v, kt_meta, v_meta, b, heads, pad):
    t, aw = q.shape
    l = t // b
    r = lax.broadcasted_iota(jnp.int32, (2 * LANES, 2 * LANES), 0) % LANES
    c = lax.broadcasted_iota(jnp.int32, (2 * LANES, 2 * LANES), 1)
    tri = jnp.where(jnp.logical_or(c >= LANES, r >= c), 1.0, 0.0).astype(BF16)
    once = pl.Buffered(1)
    out = pl.pallas_call(
        functools.partial(_attn_kernel, heads=heads, pad=pad),
        grid=(b, l // Q_BLOCK),
        in_specs=[
            pl.BlockSpec((1, Q_BLOCK, aw), lambda bi, i: (bi, i, 0)),
            pl.BlockSpec((aw, l), lambda bi, i: (0, bi), pipeline_mode=once),
            pl.BlockSpec((1, l, aw), lambda bi, i: (bi, 0, 0), pipeline_mode=once),
            pl.BlockSpec((aw, Q_BLOCK), lambda bi, i: (0, 0)),
            pl.BlockSpec((Q_BLOCK, aw), lambda bi, i: (0, 0)),
            pl.BlockSpec((2 * LANES, 2 * LANES), lambda bi, i: (0, 0)),
        ],
        out_specs=pl.BlockSpec((1, Q_BLOCK, aw), lambda bi, i: (bi, i, 0)),
        out_shape=jax.ShapeDtypeStruct((b, l, aw), BF16),
        scratch_shapes=[pltpu.VMEM((heads, Q_BLOCK, LANES), BF16), pltpu.VMEM((heads, Q_BLOCK, LANES), F32),
                        pltpu.VMEM((heads, Q_BLOCK, LANES), F32)],
        compiler_params=_cparams(("parallel", "arbitrary")),
        name="stick_attn",
    )(q.reshape(b, l, aw), kt, v.reshape(b, l, aw), kt_meta, v_meta, tri)
    return out.reshape(t, aw)


def _ssm_tables(lam_re, lam_im, log_dt, b_re, b_im, c_re, c_im, d_skip):
    hp = lax.Precision.HIGHEST
    tc = SSM_CHUNK
    g, p = lam_re.shape
    ch = b_re.shape[-1]
    dt = jnp.exp(log_dt)[:, None]
    mag = jnp.exp(lam_re * dt)
    lb_re, lb_im = mag * jnp.cos(lam_im * dt), mag * jnp.sin(lam_im * dt)
    nr, ni = lb_re - 1.0, lb_im
    den = lam_re * lam_re + lam_im * lam_im
    f_re = (nr * lam_re + ni * lam_im) / den
    f_im = (ni * lam_re - nr * lam_im) / den
    bb_re = f_re[:, :, None] * b_re - f_im[:, :, None] * b_im
    bb_im = f_re[:, :, None] * b_im + f_im[:, :, None] * b_re
    pw_re, pw_im = jnp.ones((1, g, p), F32), jnp.zeros((1, g, p), F32)
    s_re, s_im = lb_re, lb_im
    while pw_re.shape[0] < tc + 1:
        n_re = pw_re * s_re - pw_im * s_im
        n_im = pw_re * s_im + pw_im * s_re
        pw_re, pw_im = jnp.concatenate([pw_re, n_re]), jnp.concatenate([pw_im, n_im])
        s_re, s_im = s_re * s_re - s_im * s_im, 2.0 * s_re * s_im
    pw_re, pw_im = pw_re[:tc + 1], pw_im[:tc + 1]
    cl_re = c_re[None] * pw_re[:, :, None, :] - c_im[None] * pw_im[:, :, None, :]
    cl_im = c_re[None] * pw_im[:, :, None, :] + c_im[None] * pw_re[:, :, None, :]
    kern = (jnp.einsum('tgop,gpi->giot', cl_re[:tc], bb_re, precision=hp)
            - jnp.einsum('tgop,gpi->giot', cl_im[:tc], bb_im, precision=hp))
    kern = kern.at[:, :, :, 0].add(jnp.eye(ch, dtype=F32)[None] * d_skip.reshape(g, ch, 1))
    rp_re, rp_im = pw_re[:tc][::-1], pw_im[:tc][::-1]
    st_re = rp_re[:, :, :, None] * bb_re[None] - rp_im[:, :, :, None] * bb_im[None]
    st_im = rp_re[:, :, :, None] * bb_im[None] + rp_im[:, :, :, None] * bb_re[None]
    half = (jnp.arange(g) % 2)[:, None, None]

    def place_cols(m):
        z = jnp.zeros_like(m)
        return jnp.where(half == 0, jnp.concatenate([m, z], axis=2), jnp.concatenate([z, m], axis=2))

    ms_re = place_cols(st_re.transpose(1, 3, 0, 2).reshape(g, ch * tc, p)).astype(BF16)
    ms_im = place_cols(st_im.transpose(1, 3, 0, 2).reshape(g, ch * tc, p)).astype(BF16)
    mo_re = place_cols(cl_re[1:].transpose(1, 2, 0, 3).reshape(g, ch * tc, p)).transpose(0, 2, 1).astype(BF16)
    mo_im = place_cols(-cl_im[1:].transpose(1, 2, 0, 3).reshape(g, ch * tc, p)).transpose(0, 2, 1).astype(BF16)
    dec_re = pw_re[tc].reshape(g // 2, 1, 2 * p)
    dec_im = pw_im[tc].reshape(g // 2, 1, 2 * p)
    return kern, ms_re, ms_im, mo_re, mo_im, dec_re, dec_im


def _ssm_kernel(u_ref, um_ref, k_ref, msr_ref, msi_ref, mor_ref, moi_ref, dre_ref, dim_ref, y_ref,
                toep_ref, sre_ref, sim_ref, xre_ref, xim_ref, *, batch):
    ch, tc = SSM_GROUP_CH, SSM_CHUNK
    nch = u_ref.shape[1]
    per_b = nch // batch

    def chunk_rows(ref, gl):
        return jnp.concatenate([ref[gl * ch + ci] for ci in range(ch)], axis=1)

    u = [chunk_rows(u_ref, gl) for gl in range(2)]
    um = [jnp.concatenate([jnp.broadcast_to(um_ref[gl * ch + ci:gl * ch + ci + 1, :], (8, tc)) for ci in range(ch)], axis=1)
          for gl in range(2)]

    def to_state(lhs, ms_ref):
        return (jnp.dot(lhs[0], ms_ref[0], preferred_element_type=F32)
                + jnp.dot(lhs[1], ms_ref[1], preferred_element_type=F32))

    sre_ref[...] = to_state(u, msr_ref)
    sim_ref[...] = to_state(u, msi_ref)
    x0_re = to_state(um, msr_ref)[0:1]
    x0_im = to_state(um, msi_ref)[0:1]
    d_re, d_im = dre_ref[0], dim_ref[0]

    def step(c, xs):
        new = []
        for bi in range(batch):
            x_re, x_im = xs[2 * bi], xs[2 * bi + 1]
            r = bi * per_b + c
            xre_ref[pl.ds(r, 1), :] = x_re
            xim_ref[pl.ds(r, 1), :] = x_im
            new.append(d_re * x_re - d_im * x_im + sre_ref[pl.ds(r, 1), :])
            new.append(d_re * x_im + d_im * x_re + sim_ref[pl.ds(r, 1), :])
        return tuple(new)

    lax.fori_loop(0, per_b, step, (x0_re, x0_im) * batch)
    xs_re = xre_ref[...].astype(BF16)
    xs_im = xim_ref[...].astype(BF16)

    lane = lax.broadcasted_iota(jnp.int32, (tc, tc), 1)
    row = lax.broadcasted_iota(jnp.int32, (tc, tc), 0)
    causal = lane >= row
    for gl in range(2):
        def fill(ci, carry):
            for co in range(ch):
                taps = jnp.broadcast_to(k_ref[gl, ci, pl.ds(co, 1), :], (tc, tc))
                blk = pltpu.roll(taps, 0, 1, stride=1, stride_axis=0)
                blk = jnp.where(causal, blk, 0.0).astype(BF16)
                toep_ref[pl.ds(pl.multiple_of(ci * tc, tc), tc), co * tc:(co + 1) * tc] = blk
            return carry
        lax.fori_loop(0, ch, fill, 0)
        y = jnp.dot(u[gl], toep_ref[...], preferred_element_type=F32)
        y = y + jnp.dot(xs_re, mor_ref[gl], preferred_element_type=F32)
        y = y + jnp.dot(xs_im, moi_ref[gl], preferred_element_type=F32)
        y = jax.nn.gelu(y).astype(y_ref.dtype)
        for co in range(ch):
            y_ref[gl * ch + co] = y[:, co * tc:(co + 1) * tc]


def _ssm(ut, ut_meta, params, batch):
    s, t = ut.shape
    tc, ch, p = SSM_CHUNK, SSM_GROUP_CH, SSM_STATE
    g = s // ch
    nch = t // tc
    kern, ms_re, ms_im, mo_re, mo_im, dec_re, dec_im = _ssm_tables(*params)
    u3 = ut.reshape(s, nch, tc)
    pair3 = lambda a, b_: pl.BlockSpec((2, a, b_), lambda j: (j, 0, 0))
    y3 = pl.pallas_call(
        functools.partial(_ssm_kernel, batch=batch),
        grid=(g // 2,),
        in_specs=[
            pl.BlockSpec((2 * ch, nch, tc), lambda j: (j, 0, 0)),
            pl.BlockSpec((2 * ch, tc), lambda j: (j, 0)),
            pl.BlockSpec((2, ch, ch, tc), lambda j: (j, 0, 0, 0)),
            pair3(ch * tc, 2 * p), pair3(ch * tc, 2 * p), pair3(2 * p, ch * tc), pair3(2 * p, ch * tc),
            pl.BlockSpec((1, 1, 2 * p), lambda j: (j, 0, 0)), pl.BlockSpec((1, 1, 2 * p), lambda j: (j, 0, 0)),
        ],
        out_specs=pl.BlockSpec((2 * ch, nch, tc), lambda j: (j, 0, 0)),
        out_shape=jax.ShapeDtypeStruct((s, nch, tc), BF16),
        scratch_shapes=[pltpu.VMEM((ch * tc, ch * tc), BF16)] + [pltpu.VMEM((nch, 2 * p), F32)] * 4,
        compiler_params=_cparams(("parallel",)), name="ssm_chunked",
    )(u3, ut_meta, kern, ms_re, ms_im, mo_re, mo_im, dec_re, dec_im)
    return y3.reshape(s, t)


RANK_BITS = 16
SUBLANES = 8


def _store_token_tiles(ref, val):
    for c in range(SUBLANES):
        ref[:, c, :] = val[:, c * LANES:(c + 1) * LANES]


def _load_token_tiles(ref):
    return jnp.concatenate([ref[:, c, :] for c in range(SUBLANES)], axis=1)


def _merge_kernel(x_ref, a_ref, yt_ref, ga_ref, gb_ref, gwt_ref, gbias_ref, wpa_ref, wpb_ref, wo_ref, gn_ref,
                  wr_ref, br_ref, ltri_ref, h_ref, hn_ref, route_ref, count_ref, run_ref):
    @pl.when(pl.program_id(0) == 0)
    def _():
        run_ref[...] = jnp.zeros_like(run_ref)

    yt = yt_ref[...]
    gate = jax.nn.sigmoid(jnp.dot(gwt_ref[...], yt, preferred_element_type=F32) + gbias_ref[...])
    ssm_t = (yt.astype(F32) * gate).astype(BF16)
    merged = (ga_ref[...].astype(F32) * jnp.dot(a_ref[...], wpa_ref[...], preferred_element_type=F32)
              + gb_ref[...].astype(F32) * lax.dot_general(ssm_t, wpb_ref[...], _TN, preferred_element_type=F32))
    h = x_ref[...] + jnp.dot(merged.astype(BF16), wo_ref[...], preferred_element_type=F32)
    h_ref[...] = h
    hn = h * lax.rsqrt(jnp.mean(h * h, axis=-1, keepdims=True) + RMS_EPS) * gn_ref[...]
    _store_token_tiles(hn_ref, hn)
    logits = jnp.dot(hn, wr_ref[...], preferred_element_type=F32, precision=lax.Precision.HIGHEST) + br_ref[...]
    lane = lax.broadcasted_iota(jnp.int32, logits.shape, 1)
    neg = jnp.full_like(logits, -jnp.inf)
    big = jnp.int32(LANES)
    is_grp = jnp.logical_and(lane >= N_EXPERTS, lane < N_EXPERTS + MOE_GROUPS)
    glog = jnp.where(is_grp, logits, neg)
    gmax = jnp.max(glog, axis=-1, keepdims=True)
    g_top = jnp.min(jnp.where(glog == gmax, lane, big), axis=-1, keepdims=True) - N_EXPERTS
    p_top = 1.0 / jnp.sum(jnp.exp(glog - gmax), axis=-1, keepdims=True)
    in_grp = (lane // EXPERTS_PER_GROUP) == g_top
    elog = jnp.where(jnp.logical_and(in_grp, lane < N_EXPERTS), logits, neg)
    e1 = jnp.max(elog, axis=-1, keepdims=True)
    i1 = jnp.min(jnp.where(elog == e1, lane, big), axis=-1, keepdims=True)
    elog2 = jnp.where(lane == i1, neg, elog)
    e2 = jnp.max(elog2, axis=-1, keepdims=True)
    i2 = jnp.min(jnp.where(elog2 == e2, lane, big), axis=-1, keepdims=True)
    t = jnp.exp(e2 - e1)
    w1 = p_top / (1.0 + t)
    w2 = p_top * t / (1.0 + t)
    hit1, hit2 = lane == i1, lane == i2
    chosen = jnp.where(jnp.logical_or(hit1, hit2), 1.0, 0.0)
    before = jnp.dot(ltri_ref[...], chosen.astype(BF16), preferred_element_type=F32) + run_ref[0:1, :]
    r1 = jnp.sum(jnp.where(hit1, before, 0.0), axis=-1, keepdims=True)
    r2 = jnp.sum(jnp.where(hit2, before, 0.0), axis=-1, keepdims=True)
    run_ref[...] = run_ref[...] + jnp.sum(chosen, axis=0, keepdims=True)
    count_ref[...] = run_ref[...]
    scale = float(2 ** RANK_BITS)
    route = jnp.where(lane == 0, i1.astype(F32) * scale + r1,
                      jnp.where(lane == 1, i2.astype(F32) * scale + r2,
                                jnp.where(lane == 2, w1, jnp.where(lane == 3, w2, 0.0))))
    route_ref[...] = route


def _merge(x2, attn, yt, ga, gb, glu_wt, glu_b, wpa, wpb, wo, gn, wr, br, tm):
    m, d = x2.shape
    aw, sw = attn.shape[1], yt.shape[0]
    row = lambda n: pl.BlockSpec((tm, n), lambda i: (i, 0))
    full = lambda a: pl.BlockSpec(a.shape, lambda i: (0, 0))
    ltri = jnp.where(lax.broadcasted_iota(jnp.int32, (tm, tm), 0) > lax.broadcasted_iota(jnp.int32, (tm, tm), 1),
                     1.0, 0.0).astype(BF16)
    ws = (glu_wt, glu_b, wpa, wpb, wo, gn, wr, br, ltri)
    return pl.pallas_call(
        _merge_kernel, grid=(m // tm,),
        in_specs=[row(d), row(aw), pl.BlockSpec((sw, tm), lambda i: (0, i)), row(d), row(d)] + [full(a) for a in ws],
        out_specs=[row(d), pl.BlockSpec((tm, d // LANES, LANES), lambda i: (i, 0, 0)), row(LANES),
                   pl.BlockSpec((8, LANES), lambda i: (0, 0))],
        out_shape=[jax.ShapeDtypeStruct((m, d), F32), jax.ShapeDtypeStruct((m, d // LANES, LANES), F32),
                   jax.ShapeDtypeStruct((m, LANES), F32), jax.ShapeDtypeStruct((8, LANES), F32)],
        scratch_shapes=[pltpu.VMEM((8, LANES), F32)],
        compiler_params=_cparams(("arbitrary",)), name="merge_router",
    )(x2, attn, yt, ga, gb, *ws)


def _start_row_gather(src_hbm, row_of, base, stride, buf, row0, sem, n):
    for r in range(n):
        row = row_of(base + stride * r)
        pltpu.make_async_copy(src_hbm.at[pl.ds(row, 1)], buf.at[pl.ds(row0 + r, 1)], sem).start(priority=r % 2)


def _wait_rows(src_hbm, buf, sem):
    pltpu.make_async_copy(src_hbm.at[pl.ds(0, buf.shape[0])], buf, sem).wait()


def _expert_kernel(blk_e_ref, src_ref, hn_hbm, w1_ref, w3_ref, w2_ref, y_ref, xbuf, sems, *, tm):
    j = pl.program_id(0)
    last = pl.num_programs(0) - 1
    slot = lax.rem(j, 2)

    @pl.when(j == 0)
    def _():
        _start_row_gather(hn_hbm, lambda i: src_ref[i], 0, 1, xbuf.at[0], 0, sems.at[0], tm)

    _wait_rows(hn_hbm, xbuf.at[slot], sems.at[slot])
    nxt = jnp.minimum(j + 1, last)
    for s in range(2):
        @pl.when(slot == s)
        def _():
            _start_row_gather(hn_hbm, lambda i: src_ref[i], nxt * tm, 1, xbuf.at[1 - s], 0, sems.at[1 - s], tm)
            x = _load_token_tiles(xbuf.at[s]).astype(BF16)
            a = jnp.dot(x, w1_ref[0], preferred_element_type=F32)
            b = jnp.dot(x, w3_ref[0], preferred_element_type=F32)
            hdn = (a * jax.nn.sigmoid(a) * b).astype(BF16)
            _store_token_tiles(y_ref, jnp.dot(hdn, w2_ref[0], preferred_element_type=F32))

    @pl.when(j == last)
    def _():
        _wait_rows(hn_hbm, xbuf.at[1 - slot], sems.at[1 - slot])


def _experts(hn, blk_e, slot_src, w1, w3, w2, tm):
    t, sl, ln = hn.shape
    d = sl * ln
    e, _, ff = w1.shape
    nb = blk_e.shape[0]
    wspec = lambda r, c: pl.BlockSpec((1, r, c), lambda j, be, ss: (be[j], 0, 0))
    return pl.pallas_call(
        functools.partial(_expert_kernel, tm=tm),
        grid_spec=pltpu.PrefetchScalarGridSpec(
            num_scalar_prefetch=2, grid=(nb,),
            in_specs=[pl.BlockSpec(memory_space=pl.ANY), wspec(d, ff), wspec(d, ff), wspec(ff, d)],
            out_specs=pl.BlockSpec((tm, sl, ln), lambda j, be, ss: (j, 0, 0)),
            scratch_shapes=[pltpu.VMEM((2, tm, sl, ln), F32), pltpu.SemaphoreType.DMA((2,))],
        ),
        out_shape=jax.ShapeDtypeStruct((nb * tm, sl, ln), F32),
        compiler_params=_cparams(("arbitrary",)), name="expert_ffn",
    )(blk_e, slot_src, hn, w1, w3, w2)


def _combine_kernel(code_ref, seg_ref, h_ref, route_ref, gf_ref, yb_hbm, o_ref, ybuf, sems, *, tm):
    i = pl.program_id(0)
    last = pl.num_programs(0) - 1
    slot = lax.rem(i, 2)

    def start(tile, s):
        for c in range(2):
            _start_row_gather(yb_hbm, lambda k: _slot_of(code_ref[k], seg_ref), tile * 2 * tm + c, 2,
                              ybuf.at[s], c * tm, sems.at[s], tm)

    @pl.when(i == 0)
    def _():
        start(0, 0)

    _wait_rows(yb_hbm, ybuf.at[slot], sems.at[slot])
    nxt = jnp.minimum(i + 1, last)
    for s in range(2):
        @pl.when(slot == s)
        def _():
            start(nxt, 1 - s)
            route = route_ref[...]
            w1 = route[:, 2:3]
            w2 = route[:, 3:4]
            y1 = _load_token_tiles(ybuf.at[s, pl.ds(0, tm)])
            y2 = _load_token_tiles(ybuf.at[s, pl.ds(tm, tm)])
            h = h_ref[...] + w1 * y1 + w2 * y2
            o_ref[...] = h * lax.rsqrt(jnp.mean(h * h, axis=-1, keepdims=True) + RMS_EPS) * gf_ref[...]

    @pl.when(i == last)
    def _():
        _wait_rows(yb_hbm, ybuf.at[1 - slot], sems.at[1 - slot])


def _combine(code, seg, h, route, gf, yb, tm):
    t, d = h.shape
    return pl.pallas_call(
        functools.partial(_combine_kernel, tm=tm),
        grid_spec=pltpu.PrefetchScalarGridSpec(
            num_scalar_prefetch=2, grid=(t // tm,),
            in_specs=[pl.BlockSpec((tm, d), lambda i, c, s: (i, 0)), pl.BlockSpec((tm, LANES), lambda i, c, s: (i, 0)),
                      pl.BlockSpec((1, d), lambda i, c, s: (0, 0)), pl.BlockSpec(memory_space=pl.ANY)],
            out_specs=pl.BlockSpec((tm, d), lambda i, c, s: (i, 0)),
            scratch_shapes=[pltpu.VMEM((2, 2 * tm) + yb.shape[1:], F32), pltpu.SemaphoreType.DMA((2,))],
        ),
        out_shape=jax.ShapeDtypeStruct((t, d), F32),
        compiler_params=_cparams(("arbitrary",)), name="moe_combine",
    )(code, seg, h, route, gf, yb)


SLOT_UNROLL = 8


def _slot_of(code, seg_ref):
    return seg_ref[lax.shift_right_logical(code, RANK_BITS)] + jnp.bitwise_and(code, (1 << RANK_BITS) - 1)


def _slot_kernel(code_ref, seg_ref, src_ref):
    cap = src_ref.shape[0]

    def fill(g, carry):
        for k in range(SLOT_UNROLL):
            i = g * SLOT_UNROLL + k
            src_ref[_slot_of(code_ref[i], seg_ref)] = lax.shift_right_logical(i, 1)
        return carry
    lax.fori_loop(0, seg_ref[3 * N_EXPERTS], fill, 0)

    def zero(s, carry):
        src_ref[s] = 0
        return carry

    def zero_run(e, carry):
        return lax.fori_loop(seg_ref[N_EXPERTS + e], seg_ref[2 * N_EXPERTS + e], zero, carry)
    lax.fori_loop(0, N_EXPERTS, zero_run, 0)
    lax.fori_loop(seg_ref[3 * N_EXPERTS - 1], cap, zero, 0)


def _dispatch_plan(route, counts, tm):
    t = route.shape[0]
    n = 2 * t
    nb = (n + N_EXPERTS * (tm - 1)) // tm + 1
    cap = nb * tm
    cnt = counts[0, :N_EXPERTS].astype(jnp.int32)
    padded = ((cnt + tm - 1) // tm) * tm
    pend = jnp.cumsum(padded)
    pstart = pend - padded
    seg = jnp.concatenate([pstart, pstart + cnt, pend, jnp.full((1,), n // SLOT_UNROLL)]).astype(jnp.int32)
    blk_start = jnp.arange(nb, dtype=jnp.int32) * tm
    blk_e = jnp.minimum(jnp.sum((blk_start[:, None] >= pend[None, :]).astype(jnp.int32), axis=1), N_EXPERTS - 1)
    code = route[:, :2].astype(jnp.int32).reshape(n)
    smem = pl.BlockSpec(memory_space=pltpu.SMEM)
    slot_src = pl.pallas_call(
        _slot_kernel,
        grid_spec=pltpu.PrefetchScalarGridSpec(num_scalar_prefetch=2, grid=(1,), in_specs=[], out_specs=smem),
        out_shape=jax.ShapeDtypeStruct((cap,), jnp.int32),
        compiler_params=_cparams(("arbitrary",)), name="slot_index",
    )(code, seg)
    return blk_e.astype(jnp.int32), code, seg, slot_src


def kernel(x, meta_tokens, norm_mix_g, w_in, ssm_lambda_re, ssm_lambda_im, ssm_log_dt, ssm_b_re, ssm_b_im,
           ssm_c_re, ssm_c_im, ssm_d, ssm_glu_w, ssm_glu_b, w_branch_attn, w_branch_ssm, w_out, norm_ffn_g,
           router_group_w, router_group_b, router_expert_w, router_expert_b, expert_w1, expert_w3, expert_w2,
           norm_final_g):
    b, l, d = x.shape
    depth = w_in.shape[0]
    aw = w_branch_attn.shape[1]
    sw = w_branch_ssm.shape[1]
    heads = aw // HEAD_DIM
    t = b * l
    pad = (-N_META) % Q_BLOCK
    assert depth == 1 and l % Q_BLOCK == 0 and pad + N_META == Q_BLOCK and SSM_CHUNK == Q_BLOCK
    layer = 0
    h_real = x.reshape(t, d)
    h_meta = meta_tokens.astype(x.dtype)
    tm_tok = 512

    wi = w_in[layer].astype(BF16)
    o = 3 * aw + sw
    ws = (wi[:, :aw], wi[:, aw:2 * aw].T, wi[:, 2 * aw:3 * aw], wi[:, 3 * aw:o].T, wi[:, o:o + d], wi[:, o + d:])
    g_mix = norm_mix_g[layer].reshape(1, d)
    q, kt, v, ut, ga, gb = _inproj(h_real, g_mix, ws, tm_tok)
    _, kt_m, v_m, ut_m, _, _ = _inproj(h_meta, g_mix, ws, N_META)
    kt_meta = jnp.concatenate([jnp.zeros((aw, pad), BF16), kt_m], axis=1)
    v_meta = jnp.concatenate([jnp.zeros((pad, aw), BF16), v_m], axis=0)
    ut_meta = jnp.concatenate([jnp.zeros((sw, pad), BF16), ut_m], axis=1)

    attn = _attention(q, kt, v, kt_meta, v_meta, b, heads, pad)

    ssm_params = (ssm_lambda_re[layer], ssm_lambda_im[layer], ssm_log_dt[layer], ssm_b_re[layer], ssm_b_im[layer],
                  ssm_c_re[layer], ssm_c_im[layer], ssm_d[layer])
    yt = _ssm(ut, ut_meta, ssm_params, b)

    wr = jnp.zeros((d, LANES), F32)
    wr = wr.at[:, :N_EXPERTS].set(router_expert_w[layer]).at[:, N_EXPERTS:N_EXPERTS + MOE_GROUPS].set(router_group_w[layer])
    br = jnp.zeros((1, LANES), F32)
    br = br.at[0, :N_EXPERTS].set(router_expert_b[layer]).at[0, N_EXPERTS:N_EXPERTS + MOE_GROUPS].set(router_group_b[layer])
    h_mid, hn, route, counts = _merge(
        h_real, attn, yt, ga, gb, ssm_glu_w[layer].T.astype(BF16), ssm_glu_b[layer].reshape(sw, 1).astype(F32),
        w_branch_attn[layer].astype(BF16), w_branch_ssm[layer].astype(BF16), w_out[layer].astype(BF16),
        norm_ffn_g[layer].reshape(1, d), wr, br, tm_tok)

    blk_e, code, seg, slot_src = _dispatch_plan(route, counts, MOE_TILE)
    yb = _experts(hn, blk_e, slot_src, expert_w1[layer].astype(BF16), expert_w3[layer].astype(BF16),
                  expert_w2[layer].astype(BF16), MOE_TILE)
    out = _combine(code, seg, h_mid, route, norm_final_g.reshape(1, d), yb, 256)
    return out.reshape(b, l, d)
```

```python
import functools

import jax
import jax.numpy as jnp
from jax import lax
from jax.experimental import pallas as pl
from jax.experimental.pallas import tpu as pltpu

F32 = jnp.float32
BF16 = jnp.bfloat16

N_META = 16
Q_BLOCK = 128
HEAD_DIM = 64
SSM_GROUP_CH = 16
SSM_STATE = 64
MOE_GROUPS = 4
EXPERTS_PER_GROUP = 8
N_EXPERTS = MOE_GROUPS * EXPERTS_PER_GROUP
RMS_EPS = 1e-6

LANES = 128
SUBLANES = 8
SSM_CHUNK = LANES
MOE_TILE = 256
STICK_CUTOFF = 104.0
VMEM_LIMIT = 52 * 1024 * 1024

_NT = (((1,), (1,)), ((), ()))
_TN = (((0,), (0,)), ((), ()))


def _cparams(sem):
    return pltpu.CompilerParams(dimension_semantics=sem, vmem_limit_bytes=VMEM_LIMIT)


def _inproj_kernel(x_ref, g_ref, wq_ref, wkt_ref, wv_ref, wut_ref, wga_ref, wgb_ref,
                   q_ref, kt_ref, v_ref, ut_ref, ga_ref, gb_ref):
    x = x_ref[...]
    ms = jnp.mean(x * x, axis=-1, keepdims=True)
    xn = (x * lax.rsqrt(ms + RMS_EPS) * g_ref[...]).astype(BF16)
    q_ref[...] = (jnp.dot(xn, wq_ref[...], preferred_element_type=F32) * (HEAD_DIM ** -0.5)).astype(BF16)
    kt_ref[...] = lax.dot_general(wkt_ref[...], xn, _NT, preferred_element_type=F32).astype(BF16)
    v_ref[...] = jnp.dot(xn, wv_ref[...], preferred_element_type=F32).astype(BF16)
    ut_ref[...] = lax.dot_general(wut_ref[...], xn, _NT, preferred_element_type=F32).astype(BF16)
    ga_ref[...] = jax.nn.sigmoid(jnp.dot(xn, wga_ref[...], preferred_element_type=F32)).astype(BF16)
    gb_ref[...] = jax.nn.sigmoid(jnp.dot(xn, wgb_ref[...], preferred_element_type=F32)).astype(BF16)


def _inproj(x2, g, ws, tm):
    m, d = x2.shape
    aw, sw = ws[0].shape[1], ws[3].shape[0]
    row = lambda n: pl.BlockSpec((tm, n), lambda i: (i, 0))
    col = lambda n: pl.BlockSpec((n, tm), lambda i: (0, i))
    full = lambda a: pl.BlockSpec(a.shape, lambda i: (0, 0))
    sds = jax.ShapeDtypeStruct
    return pl.pallas_call(
        _inproj_kernel,
        grid=(m // tm,),
        in_specs=[row(d), full(g)] + [full(w) for w in ws],
        out_specs=[row(aw), col(aw), row(aw), col(sw), row(d), row(d)],
        out_shape=[sds((m, aw), BF16), sds((aw, m), BF16), sds((m, aw), BF16), sds((sw, m), BF16),
                   sds((m, d), BF16), sds((m, d), BF16)],
        compiler_params=_cparams(("parallel",)),
        name="inproj",
    )(x2, g, *ws)


def _attn_kernel(q_ref, kt_ref, v_ref, ktm_ref, vm_ref, tri_ref, o_ref, qm_ref, carry_ref, acc_ref, *, heads, pad):
    i = pl.program_id(1)
    pairs = heads // 2
    carry_ref[...] = jnp.zeros_like(carry_ref)
    acc_ref[...] = jnp.zeros_like(acc_ref)
    lane = lax.broadcasted_iota(jnp.int32, (heads * Q_BLOCK, LANES), 1)
    row = jnp.bitwise_and(lax.broadcasted_iota(jnp.int32, (heads * Q_BLOCK, LANES), 0), Q_BLOCK - 1)
    low_half = lax.broadcasted_iota(jnp.int32, (Q_BLOCK, LANES), 1) < HEAD_DIM
    pair_rows = lambda p: slice(2 * p * Q_BLOCK, (2 * p + 2) * Q_BLOCK)
    for p in range(pairs):
        qp = q_ref[0, :, p * LANES:(p + 1) * LANES]
        qm_ref[pair_rows(p)] = jnp.concatenate([jnp.where(low_half, qp, jnp.zeros_like(qp)),
                                                jnp.where(low_half, jnp.zeros_like(qp), qp)], axis=0)

    def block_step(get_kt, get_v, valid):
        z = jnp.concatenate([jnp.dot(qm_ref[pair_rows(p)], get_kt(p), preferred_element_type=F32)
                             for p in range(pairs)], axis=0)
        log_keep = -(jnp.maximum(z, 0.0) + jnp.log(1.0 + jnp.exp(-jnp.abs(z))))
        if valid is not None:
            log_keep = jnp.where(valid, log_keep, 0.0)
        hi = log_keep.astype(BF16)
        lo = (log_keep - hi.astype(F32)).astype(BF16)
        sums = jnp.dot(jnp.concatenate([hi, lo], axis=1), tri_ref[...], preferred_element_type=F32)
        carry = carry_ref[...]
        w = jnp.exp(z + sums[:, :LANES] + carry)
        if valid is not None:
            w = jnp.where(valid, w, 0.0)
        w = w.astype(BF16)
        for p in range(pairs):
            acc_ref[pair_rows(p)] += jnp.dot(w[pair_rows(p)], get_v(p), preferred_element_type=F32)
        carry = carry + sums[:, LANES:]
        carry_ref[...] = carry
        return jnp.max(carry)

    off_d = pl.multiple_of(i * Q_BLOCK, Q_BLOCK)
    top0 = block_step(lambda p: kt_ref[p * LANES:(p + 1) * LANES, pl.ds(off_d, Q_BLOCK)],
                      lambda p: v_ref[0, pl.ds(off_d, Q_BLOCK), p * LANES:(p + 1) * LANES],
                      lane < row)

    def cond(state):
        kb, top = state
        return jnp.logical_and(kb >= 0, top > -STICK_CUTOFF)

    def body(state):
        kb, _ = state
        off = pl.multiple_of(kb * Q_BLOCK, Q_BLOCK)
        top = block_step(lambda p: kt_ref[p * LANES:(p + 1) * LANES, pl.ds(off, Q_BLOCK)],
                         lambda p: v_ref[0, pl.ds(off, Q_BLOCK), p * LANES:(p + 1) * LANES],
                         None)
        return kb - 1, top

    _, top1 = lax.while_loop(cond, body, (i - 1, top0))

    @pl.when(top1 > -STICK_CUTOFF)
    def _():
        block_step(lambda p: ktm_ref[p * LANES:(p + 1) * LANES, :],
                   lambda p: vm_ref[:, p * LANES:(p + 1) * LANES],
                   lane >= pad)

    for p in range(pairs):
        even = acc_ref[2 * p * Q_BLOCK:(2 * p + 1) * Q_BLOCK]
        odd = acc_ref[(2 * p + 1) * Q_BLOCK:(2 * p + 2) * Q_BLOCK]
        o_ref[0, :, p * LANES:(p + 1) * LANES] = jnp.where(low_half, even, odd).astype(o_ref.dtype)


def _attention(q, kt, v, kt_meta, v_meta, b, heads, pad):
    t, aw = q.shape
    l = t // b
    r = lax.broadcasted_iota(jnp.int32, (2 * LANES, 2 * LANES), 0) % LANES
    c = lax.broadcasted_iota(jnp.int32, (2 * LANES, 2 * LANES), 1)
    tri = jnp.where(jnp.logical_or(c >= LANES, r >= c), 1.0, 0.0).astype(BF16)
    once = pl.Buffered(1)
    out = pl.pallas_call(
        functools.partial(_attn_kernel, heads=heads, pad=pad),
        grid=(b, l // Q_BLOCK),
        in_specs=[
            pl.BlockSpec((1, Q_BLOCK, aw), lambda bi, i: (bi, i, 0)),
            pl.BlockSpec((aw, l), lambda bi, i: (0, bi), pipeline_mode=once),
            pl.BlockSpec((1, l, aw), lambda bi, i: (bi, 0, 0), pipeline_mode=once),
            pl.BlockSpec((aw, Q_BLOCK), lambda bi, i: (0, 0)),
            pl.BlockSpec((Q_BLOCK, aw), lambda bi, i: (0, 0)),
            pl.BlockSpec((2 * LANES, 2 * LANES), lambda bi, i: (0, 0)),
        ],
        out_specs=pl.BlockSpec((1, Q_BLOCK, aw), lambda bi, i: (bi, i, 0)),
        out_shape=jax.ShapeDtypeStruct((b, l, aw), BF16),
        scratch_shapes=[pltpu.VMEM((heads * Q_BLOCK, LANES), BF16), pltpu.VMEM((heads * Q_BLOCK, LANES), F32),
                        pltpu.VMEM((heads * Q_BLOCK, LANES), F32)],
        compiler_params=_cparams(("parallel", "arbitrary")),
        name="stick_attn",
    )(q.reshape(b, l, aw), kt, v.reshape(b, l, aw), kt_meta, v_meta, tri)
    return out.reshape(t, aw)


def _ssm_tables(lam_re, lam_im, log_dt, b_re, b_im, c_re, c_im, d_skip):
    hp = lax.Precision.HIGHEST
    tc = SSM_CHUNK
    g, p = lam_re.shape
    ch = b_re.shape[-1]
    dt = jnp.exp(log_dt)[:, None]
    mag = jnp.exp(lam_re * dt)
    lb_re, lb_im = mag * jnp.cos(lam_im * dt), mag * jnp.sin(lam_im * dt)
    nr, ni = lb_re - 1.0, lb_im
    den = lam_re * lam_re + lam_im * lam_im
    f_re = (nr * lam_re + ni * lam_im) / den
    f_im = (ni * lam_re - nr * lam_im) / den
    bb_re = f_re[:, :, None] * b_re - f_im[:, :, None] * b_im
    bb_im = f_re[:, :, None] * b_im + f_im[:, :, None] * b_re
    pw_re, pw_im = jnp.ones((1, g, p), F32), jnp.zeros((1, g, p), F32)
    s_re, s_im = lb_re, lb_im
    while pw_re.shape[0] < tc + 1:
        n_re = pw_re * s_re - pw_im * s_im
        n_im = pw_re * s_im + pw_im * s_re
        pw_re, pw_im = jnp.concatenate([pw_re, n_re]), jnp.concatenate([pw_im, n_im])
        s_re, s_im = s_re * s_re - s_im * s_im, 2.0 * s_re * s_im
    pw_re, pw_im = pw_re[:tc + 1], pw_im[:tc + 1]
    cl_re = c_re[None] * pw_re[:, :, None, :] - c_im[None] * pw_im[:, :, None, :]
    cl_im = c_re[None] * pw_im[:, :, None, :] + c_im[None] * pw_re[:, :, None, :]
    kern = (jnp.einsum('tgop,gpi->giot', cl_re[:tc], bb_re, precision=hp)
            - jnp.einsum('tgop,gpi->giot', cl_im[:tc], bb_im, precision=hp))
    kern = kern.at[:, :, :, 0].add(jnp.eye(ch, dtype=F32)[None] * d_skip.reshape(g, ch, 1))
    rp_re, rp_im = pw_re[:tc][::-1], pw_im[:tc][::-1]
    st_re = rp_re[:, :, :, None] * bb_re[None] - rp_im[:, :, :, None] * bb_im[None]
    st_im = rp_re[:, :, :, None] * bb_im[None] + rp_im[:, :, :, None] * bb_re[None]
    half = (jnp.arange(g) % 2)[:, None, None]

    def place_cols(m):
        z = jnp.zeros_like(m)
        return jnp.where(half == 0, jnp.concatenate([m, z], axis=2), jnp.concatenate([z, m], axis=2))

    ms_re = place_cols(st_re.transpose(1, 3, 0, 2).reshape(g, ch * tc, p)).astype(BF16)
    ms_im = place_cols(st_im.transpose(1, 3, 0, 2).reshape(g, ch * tc, p)).astype(BF16)
    mo_re = place_cols(cl_re[1:].transpose(1, 2, 0, 3).reshape(g, ch * tc, p)).transpose(0, 2, 1).astype(BF16)
    mo_im = place_cols(-cl_im[1:].transpose(1, 2, 0, 3).reshape(g, ch * tc, p)).transpose(0, 2, 1).astype(BF16)
    dec_re = pw_re[tc].reshape(g // 2, 1, 2 * p)
    dec_im = pw_im[tc].reshape(g // 2, 1, 2 * p)
    return kern, ms_re, ms_im, mo_re, mo_im, dec_re, dec_im


def _ssm_kernel(u_ref, um_ref, k_ref, msr_ref, msi_ref, mor_ref, moi_ref, dre_ref, dim_ref, y_ref,
                toep_ref, sre_ref, sim_ref, xre_ref, xim_ref, *, batch):
    ch, tc = SSM_GROUP_CH, SSM_CHUNK
    nch = u_ref.shape[1]
    per_b = nch // batch

    def chunk_rows(ref, gl):
        return jnp.concatenate([ref[gl * ch + ci] for ci in range(ch)], axis=1)

    u = [chunk_rows(u_ref, gl) for gl in range(2)]
    um = [jnp.concatenate([jnp.broadcast_to(um_ref[gl * ch + ci:gl * ch + ci + 1, :], (8, tc)) for ci in range(ch)], axis=1)
          for gl in range(2)]

    def to_state(lhs, ms_ref):
        return (jnp.dot(lhs[0], ms_ref[0], preferred_element_type=F32)
                + jnp.dot(lhs[1], ms_ref[1], preferred_element_type=F32))

    sre_ref[...] = to_state(u, msr_ref)
    sim_ref[...] = to_state(u, msi_ref)
    x0_re = to_state(um, msr_ref)[0:1]
    x0_im = to_state(um, msi_ref)[0:1]
    d_re, d_im = dre_ref[0], dim_ref[0]

    def step(c, xs):
        new = []
        for bi in range(batch):
            x_re, x_im = xs[2 * bi], xs[2 * bi + 1]
            r = bi * per_b + c
            xre_ref[pl.ds(r, 1), :] = x_re
            xim_ref[pl.ds(r, 1), :] = x_im
            new.append(d_re * x_re - d_im * x_im + sre_ref[pl.ds(r, 1), :])
            new.append(d_re * x_im + d_im * x_re + sim_ref[pl.ds(r, 1), :])
        return tuple(new)

    lax.fori_loop(0, per_b, step, (x0_re, x0_im) * batch)
    xs_re = xre_ref[...].astype(BF16)
    xs_im = xim_ref[...].astype(BF16)

    lane = lax.broadcasted_iota(jnp.int32, (tc, tc), 1)
    row = lax.broadcasted_iota(jnp.int32, (tc, tc), 0)
    causal = lane >= row
    for gl in range(2):
        def fill(ci, carry):
            for co in range(ch):
                taps = jnp.broadcast_to(k_ref[gl, ci, pl.ds(co, 1), :], (tc, tc))
                blk = pltpu.roll(taps, 0, 1, stride=1, stride_axis=0)
                blk = jnp.where(causal, blk, 0.0).astype(BF16)
                toep_ref[pl.ds(pl.multiple_of(ci * tc, tc), tc), co * tc:(co + 1) * tc] = blk
            return carry
        lax.fori_loop(0, ch, fill, 0)
        y = jnp.dot(u[gl], toep_ref[...], preferred_element_type=F32)
        y = y + jnp.dot(xs_re, mor_ref[gl], preferred_element_type=F32)
        y = y + jnp.dot(xs_im, moi_ref[gl], preferred_element_type=F32)
        y = jax.nn.gelu(y).astype(y_ref.dtype)
        for co in range(ch):
            y_ref[gl * ch + co] = y[:, co * tc:(co + 1) * tc]


def _ssm(ut, ut_meta, params, batch):
    s, t = ut.shape
    tc, ch, p = SSM_CHUNK, SSM_GROUP_CH, SSM_STATE
    g = s // ch
    nch = t // tc
    kern, ms_re, ms_im, mo_re, mo_im, dec_re, dec_im = _ssm_tables(*params)
    u3 = ut.reshape(s, nch, tc)
    pair3 = lambda a, b_: pl.BlockSpec((2, a, b_), lambda j: (j, 0, 0))
    y3 = pl.pallas_call(
        functools.partial(_ssm_kernel, batch=batch),
        grid=(g // 2,),
        in_specs=[
            pl.BlockSpec((2 * ch, nch, tc), lambda j: (j, 0, 0)),
            pl.BlockSpec((2 * ch, tc), lambda j: (j, 0)),
            pl.BlockSpec((2, ch, ch, tc), lambda j: (j, 0, 0, 0)),
            pair3(ch * tc, 2 * p), pair3(ch * tc, 2 * p), pair3(2 * p, ch * tc), pair3(2 * p, ch * tc),
            pl.BlockSpec((1, 1, 2 * p), lambda j: (j, 0, 0)), pl.BlockSpec((1, 1, 2 * p), lambda j: (j, 0, 0)),
        ],
        out_specs=pl.BlockSpec((2 * ch, nch, tc), lambda j: (j, 0, 0)),
        out_shape=jax.ShapeDtypeStruct((s, nch, tc), BF16),
        scratch_shapes=[pltpu.VMEM((ch * tc, ch * tc), BF16)] + [pltpu.VMEM((nch, 2 * p), F32)] * 4,
        compiler_params=_cparams(("parallel",)), name="ssm_chunked",
    )(u3, ut_meta, kern, ms_re, ms_im, mo_re, mo_im, dec_re, dec_im)
    return y3.reshape(s, t)


RANK_BITS = 16


def _merge_kernel(x_ref, a_ref, yt_ref, ga_ref, gb_ref, gwt_ref, gbias_ref, wpa_ref, wpb_ref, wo_ref, gn_ref,
                  wr_ref, br_ref, ltri_ref, h_ref, hn_ref, route_ref, count_ref, run_ref):
    @pl.when(pl.program_id(0) == 0)
    def _():
        run_ref[...] = jnp.zeros_like(run_ref)

    yt = yt_ref[...]
    gate = jax.nn.sigmoid(jnp.dot(gwt_ref[...], yt, preferred_element_type=F32) + gbias_ref[...])
    ssm = (yt.astype(F32) * gate).T.astype(BF16)
    merged = (ga_ref[...].astype(F32) * jnp.dot(a_ref[...], wpa_ref[...], preferred_element_type=F32)
              + gb_ref[...].astype(F32) * jnp.dot(ssm, wpb_ref[...], preferred_element_type=F32))
    h = x_ref[...] + jnp.dot(merged.astype(BF16), wo_ref[...], preferred_element_type=F32)
    h_ref[...] = h
    hn = h * lax.rsqrt(jnp.mean(h * h, axis=-1, keepdims=True) + RMS_EPS) * gn_ref[...]
    hn_ref[...] = hn
    hn_hi = hn.astype(BF16)
    hn_lo = (hn - hn_hi.astype(F32)).astype(BF16)
    logits = (jnp.dot(hn_hi, wr_ref[0], preferred_element_type=F32)
              + jnp.dot(hn_hi, wr_ref[1], preferred_element_type=F32)
              + jnp.dot(hn_lo, wr_ref[0], preferred_element_type=F32)) + br_ref[...]
    lane = lax.broadcasted_iota(jnp.int32, logits.shape, 1)
    neg = jnp.full_like(logits, -jnp.inf)
    big = jnp.int32(LANES)
    is_grp = jnp.logical_and(lane >= N_EXPERTS, lane < N_EXPERTS + MOE_GROUPS)
    glog = jnp.where(is_grp, logits, neg)
    gmax = jnp.max(glog, axis=-1, keepdims=True)
    g_top = jnp.min(jnp.where(glog == gmax, lane, big), axis=-1, keepdims=True) - N_EXPERTS
    p_top = 1.0 / jnp.sum(jnp.exp(glog - gmax), axis=-1, keepdims=True)
    in_grp = (lane // EXPERTS_PER_GROUP) == g_top
    elog = jnp.where(jnp.logical_and(in_grp, lane < N_EXPERTS), logits, neg)
    e1 = jnp.max(elog, axis=-1, keepdims=True)
    i1 = jnp.min(jnp.where(elog == e1, lane, big), axis=-1, keepdims=True)
    elog2 = jnp.where(lane == i1, neg, elog)
    e2 = jnp.max(elog2, axis=-1, keepdims=True)
    i2 = jnp.min(jnp.where(elog2 == e2, lane, big), axis=-1, keepdims=True)
    t = jnp.exp(e2 - e1)
    w1 = p_top / (1.0 + t)
    w2 = p_top * t / (1.0 + t)
    hit1, hit2 = lane == i1, lane == i2
    chosen = jnp.where(jnp.logical_or(hit1, hit2), 1.0, 0.0)
    before = jnp.dot(ltri_ref[...], chosen.astype(BF16), preferred_element_type=F32) + run_ref[0:1, :]
    r1 = jnp.sum(jnp.where(hit1, before, 0.0), axis=-1, keepdims=True)
    r2 = jnp.sum(jnp.where(hit2, before, 0.0), axis=-1, keepdims=True)
    run_ref[...] = run_ref[...] + jnp.sum(chosen, axis=0, keepdims=True)
    count_ref[...] = run_ref[...]
    scale = float(2 ** RANK_BITS)
    route = jnp.where(lane == 0, i1.astype(F32) * scale + r1,
                      jnp.where(lane == 1, i2.astype(F32) * scale + r2,
                                jnp.where(lane == 2, w1, jnp.where(lane == 3, w2, 0.0))))
    route_ref[...] = route


def _merge(x2, attn, yt, ga, gb, glu_wt, glu_b, wpa, wpb, wo, gn, wr, br, tm):
    m, d = x2.shape
    aw, sw = attn.shape[1], yt.shape[0]
    row = lambda n: pl.BlockSpec((tm, n), lambda i: (i, 0))
    full = lambda a: pl.BlockSpec(a.shape, lambda i: (0,) * a.ndim)
    wr_hi = wr.astype(BF16)
    wr = jnp.stack([wr_hi, (wr - wr_hi.astype(F32)).astype(BF16)])
    ltri = jnp.where(lax.broadcasted_iota(jnp.int32, (tm, tm), 0) > lax.broadcasted_iota(jnp.int32, (tm, tm), 1),
                     1.0, 0.0).astype(BF16)
    ws = (glu_wt, glu_b, wpa, wpb, wo, gn, wr, br, ltri)
    return pl.pallas_call(
        _merge_kernel, grid=(m // tm,),
        in_specs=[row(d), row(aw), pl.BlockSpec((sw, tm), lambda i: (0, i)), row(d), row(d)] + [full(a) for a in ws],
        out_specs=[row(d), row(d), row(LANES), pl.BlockSpec((8, LANES), lambda i: (0, 0))],
        out_shape=[jax.ShapeDtypeStruct((m, d), F32), jax.ShapeDtypeStruct((m, d), F32),
                   jax.ShapeDtypeStruct((m, LANES), F32), jax.ShapeDtypeStruct((8, LANES), F32)],
        scratch_shapes=[pltpu.VMEM((8, LANES), F32)],
        compiler_params=_cparams(("arbitrary",)), name="merge_router",
    )(x2, attn, yt, ga, gb, *ws)


def _start_row_gather(src_hbm, row_of, base, stride, buf, row0, sem, n):
    for r in range(n):
        row = row_of(base + stride * r)
        pltpu.make_async_copy(src_hbm.at[pl.ds(row, 1)], buf.at[pl.ds(row0 + r, 1)], sem).start(priority=r % 2)


def _wait_rows(src_hbm, buf, sem):
    pltpu.make_async_copy(src_hbm.at[pl.ds(0, buf.shape[0])], buf, sem).wait()


def _dispatch_kernel(code_ref, seg_ref, hn_ref, xs_hbm, zero_ref, sem, zsem, *, tm):
    i = pl.program_id(0)

    @pl.when(i == 0)
    def _():
        zero_ref[...] = jnp.zeros_like(zero_ref)

        def pad_copy(s):
            return pltpu.make_async_copy(zero_ref.at[pl.ds(0, 1)], xs_hbm.at[pl.ds(s, 1)], zsem)

        def each_pad(fn):
            body = lambda s, c: (fn(s), c)[1]

            def run(e, carry):
                return lax.fori_loop(seg_ref[N_EXPERTS + e], seg_ref[2 * N_EXPERTS + e], body, carry)
            lax.fori_loop(0, N_EXPERTS, run, 0)
            lax.fori_loop(seg_ref[3 * N_EXPERTS - 1], xs_hbm.shape[0], body, 0)
        each_pad(lambda s: pad_copy(s).start())
        each_pad(lambda s: pad_copy(s).wait())

    for r in range(tm):
        for c in range(2):
            slot = _slot_of(code_ref[2 * (i * tm + r) + c], seg_ref)
            pltpu.make_async_copy(hn_ref.at[pl.ds(r, 1)], xs_hbm.at[pl.ds(slot, 1)], sem).start(priority=c)
    pltpu.make_async_copy(xs_hbm.at[pl.ds(0, 2 * tm)], xs_hbm.at[pl.ds(0, 2 * tm)], sem).wait()


def _dispatch(hn, code, seg, cap, tm):
    t, d = hn.shape
    return pl.pallas_call(
        functools.partial(_dispatch_kernel, tm=tm),
        grid_spec=pltpu.PrefetchScalarGridSpec(
            num_scalar_prefetch=2, grid=(t // tm,),
            in_specs=[pl.BlockSpec((tm, d), lambda i, c, s: (i, 0))],
            out_specs=pl.BlockSpec(memory_space=pl.ANY),
            scratch_shapes=[pltpu.VMEM((SUBLANES, d), F32), pltpu.SemaphoreType.DMA(()), pltpu.SemaphoreType.DMA(())],
        ),
        out_shape=jax.ShapeDtypeStruct((cap, d), F32),
        compiler_params=_cparams(("arbitrary",)), name="moe_dispatch",
    )(code, seg, hn)


def _expert_kernel(blk_e_ref, n_used_ref, x_ref, w1_ref, w3_ref, w2_ref, y_ref):
    j = pl.program_id(0)

    @pl.when(j < n_used_ref[0])
    def _():
        x = x_ref[...].astype(BF16)
        a = jnp.dot(x, w1_ref[0], preferred_element_type=F32)
        b = jnp.dot(x, w3_ref[0], preferred_element_type=F32)
        hdn = (a * jax.nn.sigmoid(a) * b).astype(BF16)
        y_ref[...] = jnp.dot(hdn, w2_ref[0], preferred_element_type=F32)

    @pl.when(j >= n_used_ref[0])
    def _():
        y_ref[...] = jnp.zeros_like(y_ref)


def _experts(xs, blk_e, n_used, w1, w3, w2, tm):
    cap, d = xs.shape
    e, _, ff = w1.shape
    nb = cap // tm
    wspec = lambda r, c: pl.BlockSpec((1, r, c), lambda j, be, nu: (be[j], 0, 0))
    xrow = lambda j, be, nu: (jnp.minimum(j, nu[0] - 1), 0)
    return pl.pallas_call(
        _expert_kernel,
        grid_spec=pltpu.PrefetchScalarGridSpec(
            num_scalar_prefetch=2, grid=(nb,),
            in_specs=[pl.BlockSpec((tm, d), xrow), wspec(d, ff), wspec(d, ff), wspec(ff, d)],
            out_specs=pl.BlockSpec((tm, d), lambda j, be, nu: (j, 0)),
        ),
        out_shape=jax.ShapeDtypeStruct((cap, d), F32),
        compiler_params=_cparams(("arbitrary",)), name="expert_ffn",
    )(blk_e, n_used, xs, w1, w3, w2)


def _combine_kernel(code_ref, seg_ref, h_ref, route_ref, gf_ref, yb_hbm, o_ref, ybuf, sems, *, tm):
    i = pl.program_id(0)
    last = pl.num_programs(0) - 1
    slot = lax.rem(i, 2)

    def start(tile, s):
        for c in range(2):
            _start_row_gather(yb_hbm, lambda k: _slot_of(code_ref[k], seg_ref), tile * 2 * tm + c, 2,
                              ybuf.at[s], c * tm, sems.at[s], tm)

    @pl.when(i == 0)
    def _():
        start(0, 0)

    _wait_rows(yb_hbm, ybuf.at[slot], sems.at[slot])
    nxt = jnp.minimum(i + 1, last)
    for s in range(2):
        @pl.when(slot == s)
        def _():
            start(nxt, 1 - s)
            route = route_ref[...]
            w1 = route[:, 2:3]
            w2 = route[:, 3:4]
            h = h_ref[...] + w1 * ybuf[s, :tm, :] + w2 * ybuf[s, tm:, :]
            o_ref[...] = h * lax.rsqrt(jnp.mean(h * h, axis=-1, keepdims=True) + RMS_EPS) * gf_ref[...]

    @pl.when(i == last)
    def _():
        _wait_rows(yb_hbm, ybuf.at[1 - slot], sems.at[1 - slot])


def _combine(code, seg, h, route, gf, yb, tm):
    t, d = h.shape
    return pl.pallas_call(
        functools.partial(_combine_kernel, tm=tm),
        grid_spec=pltpu.PrefetchScalarGridSpec(
            num_scalar_prefetch=2, grid=(t // tm,),
            in_specs=[pl.BlockSpec((tm, d), lambda i, c, s: (i, 0)), pl.BlockSpec((tm, LANES), lambda i, c, s: (i, 0)),
                      pl.BlockSpec((1, d), lambda i, c, s: (0, 0)), pl.BlockSpec(memory_space=pl.ANY)],
            out_specs=pl.BlockSpec((tm, d), lambda i, c, s: (i, 0)),
            scratch_shapes=[pltpu.VMEM((2, 2 * tm) + yb.shape[1:], F32), pltpu.SemaphoreType.DMA((2,))],
        ),
        out_shape=jax.ShapeDtypeStruct((t, d), F32),
        compiler_params=_cparams(("arbitrary",)), name="moe_combine",
    )(code, seg, h, route, gf, yb)


def _slot_of(code, seg_ref):
    return seg_ref[lax.shift_right_logical(code, RANK_BITS)] + jnp.bitwise_and(code, (1 << RANK_BITS) - 1)


def _dispatch_plan(route, counts, tm):
    t = route.shape[0]
    n = 2 * t
    nb = (n + N_EXPERTS * (tm - 1)) // tm + 1
    cnt = counts[0, :N_EXPERTS].astype(jnp.int32)
    padded = ((cnt + tm - 1) // tm) * tm
    pend = jnp.cumsum(padded)
    pstart = pend - padded
    seg = jnp.concatenate([pstart, pstart + cnt, pend]).astype(jnp.int32)
    blk_start = jnp.arange(nb, dtype=jnp.int32) * tm
    blk_e = jnp.minimum(jnp.sum((blk_start[:, None] >= pend[None, :]).astype(jnp.int32), axis=1), N_EXPERTS - 1)
    n_used = (pend[-1:] // tm).astype(jnp.int32)
    code = route[:, :2].astype(jnp.int32).reshape(n)
    return nb * tm, blk_e.astype(jnp.int32), n_used, code, seg


def kernel(x, meta_tokens, norm_mix_g, w_in, ssm_lambda_re, ssm_lambda_im, ssm_log_dt, ssm_b_re, ssm_b_im,
           ssm_c_re, ssm_c_im, ssm_d, ssm_glu_w, ssm_glu_b, w_branch_attn, w_branch_ssm, w_out, norm_ffn_g,
           router_group_w, router_group_b, router_expert_w, router_expert_b, expert_w1, expert_w3, expert_w2,
           norm_final_g):
    b, l, d = x.shape
    depth = w_in.shape[0]
    aw = w_branch_attn.shape[1]
    sw = w_branch_ssm.shape[1]
    heads = aw // HEAD_DIM
    t = b * l
    pad = (-N_META) % Q_BLOCK
    assert depth == 1 and l % Q_BLOCK == 0 and pad + N_META == Q_BLOCK and SSM_CHUNK == Q_BLOCK
    layer = 0
    h_real = x.reshape(t, d)
    h_meta = meta_tokens.astype(x.dtype)
    tm_tok = 512

    wi = w_in[layer].astype(BF16)
    o = 3 * aw + sw
    ws = (wi[:, :aw], wi[:, aw:2 * aw].T, wi[:, 2 * aw:3 * aw], wi[:, 3 * aw:o].T, wi[:, o:o + d], wi[:, o + d:])
    g_mix = norm_mix_g[layer].reshape(1, d)
    q, kt, v, ut, ga, gb = _inproj(h_real, g_mix, ws, tm_tok)
    _, kt_m, v_m, ut_m, _, _ = _inproj(h_meta, g_mix, ws, N_META)
    kt_meta = jnp.concatenate([jnp.zeros((aw, pad), BF16), kt_m], axis=1)
    v_meta = jnp.concatenate([jnp.zeros((pad, aw), BF16), v_m], axis=0)
    ut_meta = jnp.concatenate([jnp.zeros((sw, pad), BF16), ut_m], axis=1)

    attn = _attention(q, kt, v, kt_meta, v_meta, b, heads, pad)

    ssm_params = (ssm_lambda_re[layer], ssm_lambda_im[layer], ssm_log_dt[layer], ssm_b_re[layer], ssm_b_im[layer],
                  ssm_c_re[layer], ssm_c_im[layer], ssm_d[layer])
    yt = _ssm(ut, ut_meta, ssm_params, b)

    wr = jnp.zeros((d, LANES), F32)
    wr = wr.at[:, :N_EXPERTS].set(router_expert_w[layer]).at[:, N_EXPERTS:N_EXPERTS + MOE_GROUPS].set(router_group_w[layer])
    br = jnp.zeros((1, LANES), F32)
    br = br.at[0, :N_EXPERTS].set(router_expert_b[layer]).at[0, N_EXPERTS:N_EXPERTS + MOE_GROUPS].set(router_group_b[layer])
    h_mid, hn, route, counts = _merge(
        h_real, attn, yt, ga, gb, ssm_glu_w[layer].T.astype(BF16), ssm_glu_b[layer].reshape(sw, 1).astype(F32),
        w_branch_attn[layer].astype(BF16), w_branch_ssm[layer].astype(BF16), w_out[layer].astype(BF16),
        norm_ffn_g[layer].reshape(1, d), wr, br, tm_tok)

    cap, blk_e, n_used, code, seg = _dispatch_plan(route, counts, MOE_TILE)
    xs = _dispatch(hn, code, seg, cap, 256)
    yb = _experts(xs, blk_e, n_used, expert_w1[layer].astype(BF16), expert_w3[layer].astype(BF16),
                  expert_w2[layer].astype(BF16), MOE_TILE)
    out = _combine(code, seg, h_mid, route, norm_final_g.reshape(1, d), yb, 256)
    return out.reshape(b, l, d)
```

```python
import functools

import jax
import jax.numpy as jnp
from jax import lax
from jax.experimental import pallas as pl
from jax.experimental.pallas import tpu as pltpu

F32 = jnp.float32
BF16 = jnp.bfloat16

N_META = 16
Q_BLOCK = 128
HEAD_DIM = 64
SSM_GROUP_CH = 16
SSM_STATE = 64
MOE_GROUPS = 4
EXPERTS_PER_GROUP = 8
N_EXPERTS = MOE_GROUPS * EXPERTS_PER_GROUP
RMS_EPS = 1e-6

LANES = 128
SUBLANES = 8
SSM_CHUNK = LANES
MOE_TILE = 512
STICK_CUTOFF = 104.0
VMEM_LIMIT = 52 * 1024 * 1024

_NT = (((1,), (1,)), ((), ()))
_TN = (((0,), (0,)), ((), ()))


def _cparams(sem):
    return pltpu.CompilerParams(dimension_semantics=sem, vmem_limit_bytes=VMEM_LIMIT)


def _inproj_kernel(x_ref, g_ref, wq_ref, wkt_ref, wv_ref, wut_ref, wga_ref, wgb_ref,
                   q_ref, kt_ref, v_ref, ut_ref, ga_ref, gb_ref):
    x = x_ref[...]
    ms = jnp.mean(x * x, axis=-1, keepdims=True)
    xn = (x * lax.rsqrt(ms + RMS_EPS) * g_ref[...]).astype(BF16)
    q_ref[...] = (jnp.dot(xn, wq_ref[...], preferred_element_type=F32) * (HEAD_DIM ** -0.5)).astype(BF16)
    kt_ref[...] = lax.dot_general(wkt_ref[...], xn, _NT, preferred_element_type=F32).astype(BF16)
    v_ref[...] = jnp.dot(xn, wv_ref[...], preferred_element_type=F32).astype(BF16)
    ut_ref[...] = lax.dot_general(wut_ref[...], xn, _NT, preferred_element_type=F32).astype(BF16)
    ga_ref[...] = jax.nn.sigmoid(jnp.dot(xn, wga_ref[...], preferred_element_type=F32)).astype(BF16)
    gb_ref[...] = jax.nn.sigmoid(jnp.dot(xn, wgb_ref[...], preferred_element_type=F32)).astype(BF16)


def _inproj(x2, g, ws, tm):
    m, d = x2.shape
    aw, sw = ws[0].shape[1], ws[3].shape[0]
    row = lambda n: pl.BlockSpec((tm, n), lambda i: (i, 0))
    col = lambda n: pl.BlockSpec((n, tm), lambda i: (0, i))
    full = lambda a: pl.BlockSpec(a.shape, lambda i: (0, 0))
    sds = jax.ShapeDtypeStruct
    return pl.pallas_call(
        _inproj_kernel,
        grid=(m // tm,),
        in_specs=[row(d), full(g)] + [full(w) for w in ws],
        out_specs=[row(aw), col(aw), row(aw), col(sw), row(d), row(d)],
        out_shape=[sds((m, aw), BF16), sds((aw, m), BF16), sds((m, aw), BF16), sds((sw, m), BF16),
                   sds((m, d), BF16), sds((m, d), BF16)],
        compiler_params=_cparams(("parallel",)),
        name="inproj",
    )(x2, g, *ws)


def _attn_kernel(q_ref, kt_ref, v_ref, ktm_ref, vm_ref, tri_ref, o_ref, qm_ref, carry_ref, acc_ref, *, heads, pad):
    i = pl.program_id(1)
    pairs = heads // 2
    carry_ref[...] = jnp.zeros_like(carry_ref)
    acc_ref[...] = jnp.zeros_like(acc_ref)
    lane = lax.broadcasted_iota(jnp.int32, (Q_BLOCK, LANES), 1)
    row = lax.broadcasted_iota(jnp.int32, (Q_BLOCK, LANES), 0)
    low_half = lane < HEAD_DIM
    rows = lambda h: slice(h * Q_BLOCK, (h + 1) * Q_BLOCK)
    for p in range(pairs):
        qp = q_ref[0, :, p * LANES:(p + 1) * LANES]
        qm_ref[rows(2 * p)] = jnp.where(low_half, qp, jnp.zeros_like(qp))
        qm_ref[rows(2 * p + 1)] = jnp.where(low_half, jnp.zeros_like(qp), qp)

    def block_step(get_kt, get_v, valid):
        kts = [get_kt(p) for p in range(pairs)]
        vs = [get_v(p) for p in range(pairs)]
        zs = [jnp.dot(qm_ref[rows(h)], kts[h // 2], preferred_element_type=F32) for h in range(heads)]
        cats = []
        for h in range(heads):
            z = zs[h]
            log_keep = -(jnp.maximum(z, 0.0) + jnp.log(1.0 + jnp.exp(-jnp.abs(z))))
            if valid is not None:
                log_keep = jnp.where(valid, log_keep, 0.0)
            hi = log_keep.astype(BF16)
            lo = (log_keep - hi.astype(F32)).astype(BF16)
            cats.append(jnp.concatenate([hi, lo], axis=1))
        sums = [jnp.dot(cats[h], tri_ref[...], preferred_element_type=F32) for h in range(heads)]
        top = None
        for h in range(heads):
            carry = carry_ref[rows(h)]
            w = jnp.exp(zs[h] + sums[h][:, :LANES] + carry)
            if valid is not None:
                w = jnp.where(valid, w, 0.0)
            acc_ref[rows(h)] += jnp.dot(w.astype(BF16), vs[h // 2], preferred_element_type=F32)
            carry = carry + sums[h][:, LANES:]
            carry_ref[rows(h)] = carry
            top = carry if top is None else jnp.maximum(top, carry)
        return jnp.max(top)

    off_d = pl.multiple_of(i * Q_BLOCK, Q_BLOCK)
    top0 = block_step(lambda p: kt_ref[p * LANES:(p + 1) * LANES, pl.ds(off_d, Q_BLOCK)],
                      lambda p: v_ref[0, pl.ds(off_d, Q_BLOCK), p * LANES:(p + 1) * LANES],
                      lane < row)

    def cond(state):
        kb, top = state
        return jnp.logical_and(kb >= 0, top > -STICK_CUTOFF)

    def body(state):
        kb, _ = state
        off = pl.multiple_of(kb * Q_BLOCK, Q_BLOCK)
        top = block_step(lambda p: kt_ref[p * LANES:(p + 1) * LANES, pl.ds(off, Q_BLOCK)],
                         lambda p: v_ref[0, pl.ds(off, Q_BLOCK), p * LANES:(p + 1) * LANES],
                         None)
        return kb - 1, top

    _, top1 = lax.while_loop(cond, body, (i - 1, top0))

    @pl.when(top1 > -STICK_CUTOFF)
    def _():
        block_step(lambda p: ktm_ref[p * LANES:(p + 1) * LANES, :],
                   lambda p: vm_ref[:, p * LANES:(p + 1) * LANES],
                   lane >= pad)

    for p in range(pairs):
        even = acc_ref[2 * p * Q_BLOCK:(2 * p + 1) * Q_BLOCK]
        odd = acc_ref[(2 * p + 1) * Q_BLOCK:(2 * p + 2) * Q_BLOCK]
        o_ref[0, :, p * LANES:(p + 1) * LANES] = jnp.where(low_half, even, odd).astype(o_ref.dtype)


def _attention(q, kt, v, kt_meta, v_meta, b, heads, pad):
    t, aw = q.shape
    l = t // b
    r = lax.broadcasted_iota(jnp.int32, (2 * LANES, 2 * LANES), 0) % LANES
    c = lax.broadcasted_iota(jnp.int32, (2 * LANES, 2 * LANES), 1)
    tri = jnp.where(jnp.logical_or(c >= LANES, r >= c), 1.0, 0.0).astype(BF16)
    once = pl.Buffered(1)
    out = pl.pallas_call(
        functools.partial(_attn_kernel, heads=heads, pad=pad),
        grid=(b, l // Q_BLOCK),
        in_specs=[
            pl.BlockSpec((1, Q_BLOCK, aw), lambda bi, i: (bi, i, 0)),
            pl.BlockSpec((aw, l), lambda bi, i: (0, bi), pipeline_mode=once),
            pl.BlockSpec((1, l, aw), lambda bi, i: (bi, 0, 0), pipeline_mode=once),
            pl.BlockSpec((aw, Q_BLOCK), lambda bi, i: (0, 0)),
            pl.BlockSpec((Q_BLOCK, aw), lambda bi, i: (0, 0)),
            pl.BlockSpec((2 * LANES, 2 * LANES), lambda bi, i: (0, 0)),
        ],
        out_specs=pl.BlockSpec((1, Q_BLOCK, aw), lambda bi, i: (bi, i, 0)),
        out_shape=jax.ShapeDtypeStruct((b, l, aw), BF16),
        scratch_shapes=[pltpu.VMEM((heads * Q_BLOCK, LANES), BF16), pltpu.VMEM((heads * Q_BLOCK, LANES), F32),
                        pltpu.VMEM((heads * Q_BLOCK, LANES), F32)],
        compiler_params=_cparams(("parallel", "arbitrary")),
        name="stick_attn",
    )(q.reshape(b, l, aw), kt, v.reshape(b, l, aw), kt_meta, v_meta, tri)
    return out.reshape(t, aw)


def _ssm_tables(lam_re, lam_im, log_dt, b_re, b_im, c_re, c_im, d_skip):
    hp = lax.Precision.HIGHEST
    tc = SSM_CHUNK
    g, p = lam_re.shape
    ch = b_re.shape[-1]
    dt = jnp.exp(log_dt)[:, None]
    mag = jnp.exp(lam_re * dt)
    lb_re, lb_im = mag * jnp.cos(lam_im * dt), mag * jnp.sin(lam_im * dt)
    nr, ni = lb_re - 1.0, lb_im
    den = lam_re * lam_re + lam_im * lam_im
    f_re = (nr * lam_re + ni * lam_im) / den
    f_im = (ni * lam_re - nr * lam_im) / den
    bb_re = f_re[:, :, None] * b_re - f_im[:, :, None] * b_im
    bb_im = f_re[:, :, None] * b_im + f_im[:, :, None] * b_re
    pw_re, pw_im = jnp.ones((1, g, p), F32), jnp.zeros((1, g, p), F32)
    s_re, s_im = lb_re, lb_im
    while pw_re.shape[0] < tc + 1:
        n_re = pw_re * s_re - pw_im * s_im
        n_im = pw_re * s_im + pw_im * s_re
        pw_re, pw_im = jnp.concatenate([pw_re, n_re]), jnp.concatenate([pw_im, n_im])
        s_re, s_im = s_re * s_re - s_im * s_im, 2.0 * s_re * s_im
    dec_re = pw_re[tc].reshape(g // 2, 1, 2 * p)
    dec_im = pw_im[tc].reshape(g // 2, 1, 2 * p)
    pw_re, pw_im = pw_re[:tc + 1].transpose(1, 2, 0), pw_im[:tc + 1].transpose(1, 2, 0)
    ct_re, ct_im = c_re.transpose(0, 2, 1)[..., None], c_im.transpose(0, 2, 1)[..., None]
    cl_re = ct_re * pw_re[:, :, None, :] - ct_im * pw_im[:, :, None, :]
    cl_im = ct_re * pw_im[:, :, None, :] + ct_im * pw_re[:, :, None, :]
    kern = (jnp.einsum('gpot,gpi->giot', cl_re[..., :tc], bb_re, precision=hp)
            - jnp.einsum('gpot,gpi->giot', cl_im[..., :tc], bb_im, precision=hp))
    kern = kern.at[:, :, :, 0].add(jnp.eye(ch, dtype=F32)[None] * d_skip.reshape(g, ch, 1))
    half = (jnp.arange(g) % 2)[:, None, None]

    def place_rows(m):
        m = m.reshape(g, p, ch * tc).astype(BF16)
        z = jnp.zeros_like(m)
        return jnp.where(half == 0, jnp.concatenate([m, z], axis=1), jnp.concatenate([z, m], axis=1))

    rp_re, rp_im = pw_re[:, :, None, tc - 1::-1], pw_im[:, :, None, tc - 1::-1]
    bq_re, bq_im = bb_re[..., None], bb_im[..., None]
    ms_re = place_rows(rp_re * bq_re - rp_im * bq_im)
    ms_im = place_rows(rp_re * bq_im + rp_im * bq_re)
    mo_re = place_rows(cl_re[..., 1:])
    mo_im = place_rows(-cl_im[..., 1:])
    return kern, ms_re, ms_im, mo_re, mo_im, dec_re, dec_im


def _ssm_kernel(u_ref, um_ref, k_ref, msr_ref, msi_ref, mor_ref, moi_ref, dre_ref, dim_ref, y_ref,
                toep_ref, sre_ref, sim_ref, xre_ref, xim_ref, *, batch):
    ch, tc = SSM_GROUP_CH, SSM_CHUNK
    nch = u_ref.shape[1]
    per_b = nch // batch

    def chunk_rows(ref, gl):
        return jnp.concatenate([ref[gl * ch + ci] for ci in range(ch)], axis=1)

    u = [chunk_rows(u_ref, gl) for gl in range(2)]
    um = [jnp.concatenate([jnp.broadcast_to(um_ref[gl * ch + ci:gl * ch + ci + 1, :], (8, tc)) for ci in range(ch)], axis=1)
          for gl in range(2)]

    def to_state(lhs, ms_ref):
        return (lax.dot_general(lhs[0], ms_ref[0], _NT, preferred_element_type=F32)
                + lax.dot_general(lhs[1], ms_ref[1], _NT, preferred_element_type=F32))

    sre_ref[...] = to_state(u, msr_ref)
    sim_ref[...] = to_state(u, msi_ref)
    x0_re = to_state(um, msr_ref)[0:1]
    x0_im = to_state(um, msi_ref)[0:1]
    d_re, d_im = dre_ref[0], dim_ref[0]

    def step(c, xs):
        new = []
        for bi in range(batch):
            x_re, x_im = xs[2 * bi], xs[2 * bi + 1]
            r = bi * per_b + c
            xre_ref[pl.ds(r, 1), :] = x_re
            xim_ref[pl.ds(r, 1), :] = x_im
            new.append(d_re * x_re - d_im * x_im + sre_ref[pl.ds(r, 1), :])
            new.append(d_re * x_im + d_im * x_re + sim_ref[pl.ds(r, 1), :])
        return tuple(new)

    lax.fori_loop(0, per_b, step, (x0_re, x0_im) * batch)
    xs_re = xre_ref[...].astype(BF16)
    xs_im = xim_ref[...].astype(BF16)

    lane = lax.broadcasted_iota(jnp.int32, (tc, tc), 1)
    row = lax.broadcasted_iota(jnp.int32, (tc, tc), 0)
    causal = lane >= row
    for gl in range(2):
        def fill(ci, carry):
            for co in range(ch):
                taps = jnp.broadcast_to(k_ref[gl, ci, pl.ds(co, 1), :], (tc, tc))
                blk = pltpu.roll(taps, 0, 1, stride=1, stride_axis=0)
                blk = jnp.where(causal, blk, 0.0).astype(BF16)
                toep_ref[pl.ds(pl.multiple_of(ci * tc, tc), tc), co * tc:(co + 1) * tc] = blk
            return carry
        lax.fori_loop(0, ch, fill, 0)
        y = jnp.dot(u[gl], toep_ref[...], preferred_element_type=F32)
        y = y + jnp.dot(xs_re, mor_ref[gl], preferred_element_type=F32)
        y = y + jnp.dot(xs_im, moi_ref[gl], preferred_element_type=F32)
        y = jax.nn.gelu(y).astype(y_ref.dtype)
        for co in range(ch):
            y_ref[gl * ch + co] = y[:, co * tc:(co + 1) * tc]


def _ssm(ut, ut_meta, params, batch):
    s, t = ut.shape
    tc, ch, p = SSM_CHUNK, SSM_GROUP_CH, SSM_STATE
    g = s // ch
    nch = t // tc
    kern, ms_re, ms_im, mo_re, mo_im, dec_re, dec_im = _ssm_tables(*params)
    u3 = ut.reshape(s, nch, tc)
    pair3 = lambda a, b_: pl.BlockSpec((2, a, b_), lambda j: (j, 0, 0))
    y3 = pl.pallas_call(
        functools.partial(_ssm_kernel, batch=batch),
        grid=(g // 2,),
        in_specs=[
            pl.BlockSpec((2 * ch, nch, tc), lambda j: (j, 0, 0)),
            pl.BlockSpec((2 * ch, tc), lambda j: (j, 0)),
            pl.BlockSpec((2, ch, ch, tc), lambda j: (j, 0, 0, 0)),
            pair3(2 * p, ch * tc), pair3(2 * p, ch * tc), pair3(2 * p, ch * tc), pair3(2 * p, ch * tc),
            pl.BlockSpec((1, 1, 2 * p), lambda j: (j, 0, 0)), pl.BlockSpec((1, 1, 2 * p), lambda j: (j, 0, 0)),
        ],
        out_specs=pl.BlockSpec((2 * ch, nch, tc), lambda j: (j, 0, 0)),
        out_shape=jax.ShapeDtypeStruct((s, nch, tc), BF16),
        scratch_shapes=[pltpu.VMEM((ch * tc, ch * tc), BF16)] + [pltpu.VMEM((nch, 2 * p), F32)] * 4,
        compiler_params=_cparams(("parallel",)), name="ssm_chunked",
    )(u3, ut_meta, kern, ms_re, ms_im, mo_re, mo_im, dec_re, dec_im)
    return y3.reshape(s, t)


RANK_BITS = 16


def _merge_kernel(x_ref, a_ref, yt_ref, ga_ref, gb_ref, gwt_ref, gbias_ref, wpa_ref, wpb_ref, wo_ref, gn_ref,
                  wr_ref, br_ref, ltri_ref, h_ref, hn_ref, route_ref, count_ref, run_ref):
    @pl.when(pl.program_id(0) == 0)
    def _():
        run_ref[...] = jnp.zeros_like(run_ref)

    yt = yt_ref[...]
    gate = jax.nn.sigmoid(jnp.dot(gwt_ref[...], yt, preferred_element_type=F32) + gbias_ref[...])
    ssm = (yt.astype(F32) * gate).T.astype(BF16)
    merged = (ga_ref[...].astype(F32) * jnp.dot(a_ref[...], wpa_ref[...], preferred_element_type=F32)
              + gb_ref[...].astype(F32) * jnp.dot(ssm, wpb_ref[...], preferred_element_type=F32))
    h = x_ref[...] + jnp.dot(merged.astype(BF16), wo_ref[...], preferred_element_type=F32)
    h_ref[...] = h
    hn = h * lax.rsqrt(jnp.mean(h * h, axis=-1, keepdims=True) + RMS_EPS) * gn_ref[...]
    hn_ref[...] = hn
    hn_hi = hn.astype(BF16)
    hn_lo = (hn - hn_hi.astype(F32)).astype(BF16)
    logits = (jnp.dot(hn_hi, wr_ref[0], preferred_element_type=F32)
              + jnp.dot(hn_hi, wr_ref[1], preferred_element_type=F32)
              + jnp.dot(hn_lo, wr_ref[0], preferred_element_type=F32)) + br_ref[...]
    lane = lax.broadcasted_iota(jnp.int32, logits.shape, 1)
    neg = jnp.full_like(logits, -jnp.inf)
    big = jnp.int32(LANES)
    is_grp = jnp.logical_and(lane >= N_EXPERTS, lane < N_EXPERTS + MOE_GROUPS)
    glog = jnp.where(is_grp, logits, neg)
    gmax = jnp.max(glog, axis=-1, keepdims=True)
    g_top = jnp.min(jnp.where(glog == gmax, lane, big), axis=-1, keepdims=True) - N_EXPERTS
    p_top = 1.0 / jnp.sum(jnp.exp(glog - gmax), axis=-1, keepdims=True)
    in_grp = (lane // EXPERTS_PER_GROUP) == g_top
    elog = jnp.where(jnp.logical_and(in_grp, lane < N_EXPERTS), logits, neg)
    e1 = jnp.max(elog, axis=-1, keepdims=True)
    i1 = jnp.min(jnp.where(elog == e1, lane, big), axis=-1, keepdims=True)
    elog2 = jnp.where(lane == i1, neg, elog)
    e2 = jnp.max(elog2, axis=-1, keepdims=True)
    i2 = jnp.min(jnp.where(elog2 == e2, lane, big), axis=-1, keepdims=True)
    t = jnp.exp(e2 - e1)
    w1 = p_top / (1.0 + t)
    w2 = p_top * t / (1.0 + t)
    hit1, hit2 = lane == i1, lane == i2
    chosen = jnp.where(jnp.logical_or(hit1, hit2), 1.0, 0.0)
    before = jnp.dot(ltri_ref[...], chosen.astype(BF16), preferred_element_type=F32) + run_ref[0:1, :]
    r1 = jnp.sum(jnp.where(hit1, before, 0.0), axis=-1, keepdims=True)
    r2 = jnp.sum(jnp.where(hit2, before, 0.0), axis=-1, keepdims=True)
    run_ref[...] = run_ref[...] + jnp.sum(chosen, axis=0, keepdims=True)
    count_ref[...] = run_ref[...]
    scale = float(2 ** RANK_BITS)
    route = jnp.where(lane == 0, i1.astype(F32) * scale + r1,
                      jnp.where(lane == 1, i2.astype(F32) * scale + r2,
                                jnp.where(lane == 2, w1, jnp.where(lane == 3, w2, 0.0))))
    route_ref[...] = route


def _merge(x2, attn, yt, ga, gb, glu_wt, glu_b, wpa, wpb, wo, gn, wr, br, tm):
    m, d = x2.shape
    aw, sw = attn.shape[1], yt.shape[0]
    row = lambda n: pl.BlockSpec((tm, n), lambda i: (i, 0))
    full = lambda a: pl.BlockSpec(a.shape, lambda i: (0,) * a.ndim)
    wr_hi = wr.astype(BF16)
    wr = jnp.stack([wr_hi, (wr - wr_hi.astype(F32)).astype(BF16)])
    ltri = jnp.where(lax.broadcasted_iota(jnp.int32, (tm, tm), 0) > lax.broadcasted_iota(jnp.int32, (tm, tm), 1),
                     1.0, 0.0).astype(BF16)
    ws = (glu_wt, glu_b, wpa, wpb, wo, gn, wr, br, ltri)
    return pl.pallas_call(
        _merge_kernel, grid=(m // tm,),
        in_specs=[row(d), row(aw), pl.BlockSpec((sw, tm), lambda i: (0, i)), row(d), row(d)] + [full(a) for a in ws],
        out_specs=[row(d), row(d), row(LANES), pl.BlockSpec((8, LANES), lambda i: (0, 0))],
        out_shape=[jax.ShapeDtypeStruct((m, d), F32), jax.ShapeDtypeStruct((m, d), F32),
                   jax.ShapeDtypeStruct((m, LANES), F32), jax.ShapeDtypeStruct((8, LANES), F32)],
        scratch_shapes=[pltpu.VMEM((8, LANES), F32)],
        compiler_params=_cparams(("arbitrary",)), name="merge_router",
    )(x2, attn, yt, ga, gb, *ws)


def _start_row_gather(src_hbm, row_of, base, stride, buf, row0, sem, n):
    for r in range(n):
        row = row_of(base + stride * r)
        pltpu.make_async_copy(src_hbm.at[pl.ds(row, 1)], buf.at[pl.ds(row0 + r, 1)], sem).start(priority=r % 2)


def _wait_rows(src_hbm, buf, sem):
    pltpu.make_async_copy(src_hbm.at[pl.ds(0, buf.shape[0])], buf, sem).wait()


def _dispatch_kernel(code_ref, seg_ref, hn_ref, xs_hbm, zero_ref, sem, zsem, *, tm):
    i = pl.program_id(0)

    @pl.when(i == 0)
    def _():
        zero_ref[...] = jnp.zeros_like(zero_ref)

        def pad_copy(s):
            return pltpu.make_async_copy(zero_ref.at[pl.ds(0, 1)], xs_hbm.at[pl.ds(s, 1)], zsem)

        def each_pad(fn):
            body = lambda s, c: (fn(s), c)[1]

            def run(e, carry):
                return lax.fori_loop(seg_ref[N_EXPERTS + e], seg_ref[2 * N_EXPERTS + e], body, carry)
            lax.fori_loop(0, N_EXPERTS, run, 0)
            lax.fori_loop(seg_ref[3 * N_EXPERTS - 1], xs_hbm.shape[0], body, 0)
        each_pad(lambda s: pad_copy(s).start())
        each_pad(lambda s: pad_copy(s).wait())

    for r in range(tm):
        for c in range(2):
            slot = _slot_of(code_ref[2 * (i * tm + r) + c], seg_ref)
            pltpu.make_async_copy(hn_ref.at[pl.ds(r, 1)], xs_hbm.at[pl.ds(slot, 1)], sem).start(priority=c)
    pltpu.make_async_copy(xs_hbm.at[pl.ds(0, 2 * tm)], xs_hbm.at[pl.ds(0, 2 * tm)], sem).wait()


def _dispatch(hn, code, seg, cap, tm):
    t, d = hn.shape
    return pl.pallas_call(
        functools.partial(_dispatch_kernel, tm=tm),
        grid_spec=pltpu.PrefetchScalarGridSpec(
            num_scalar_prefetch=2, grid=(t // tm,),
            in_specs=[pl.BlockSpec((tm, d), lambda i, c, s: (i, 0))],
            out_specs=pl.BlockSpec(memory_space=pl.ANY),
            scratch_shapes=[pltpu.VMEM((SUBLANES, d), F32), pltpu.SemaphoreType.DMA(()), pltpu.SemaphoreType.DMA(())],
        ),
        out_shape=jax.ShapeDtypeStruct((cap, d), F32),
        compiler_params=_cparams(("arbitrary",)), name="moe_dispatch",
    )(code, seg, hn)


def _expert_kernel(blk_e_ref, n_used_ref, x_ref, w1_ref, w3_ref, w2_ref, y_ref):
    j = pl.program_id(0)

    @pl.when(j < n_used_ref[0])
    def _():
        x = x_ref[...].astype(BF16)
        a = jnp.dot(x, w1_ref[0].astype(BF16), preferred_element_type=F32)
        b = jnp.dot(x, w3_ref[0].astype(BF16), preferred_element_type=F32)
        hdn = (a * jax.nn.sigmoid(a) * b).astype(BF16)
        y_ref[...] = jnp.dot(hdn, w2_ref[0].astype(BF16), preferred_element_type=F32)

    @pl.when(j >= n_used_ref[0])
    def _():
        y_ref[...] = jnp.zeros_like(y_ref)


def _experts(xs, blk_e, n_used, w1, w3, w2, tm):
    cap, d = xs.shape
    e, _, ff = w1.shape
    nb = cap // tm
    wspec = lambda r, c: pl.BlockSpec((1, r, c), lambda j, be, nu: (be[j], 0, 0))
    xrow = lambda j, be, nu: (jnp.minimum(j, nu[0] - 1), 0)
    return pl.pallas_call(
        _expert_kernel,
        grid_spec=pltpu.PrefetchScalarGridSpec(
            num_scalar_prefetch=2, grid=(nb,),
            in_specs=[pl.BlockSpec((tm, d), xrow), wspec(d, ff), wspec(d, ff), wspec(ff, d)],
            out_specs=pl.BlockSpec((tm, d), lambda j, be, nu: (j, 0)),
        ),
        out_shape=jax.ShapeDtypeStruct((cap, d), F32),
        compiler_params=_cparams(("arbitrary",)), name="expert_ffn",
    )(blk_e, n_used, xs, w1, w3, w2)


def _combine_kernel(code_ref, seg_ref, h_ref, route_ref, gf_ref, yb_hbm, o_ref, ybuf, sems, *, tm):
    i = pl.program_id(0)
    last = pl.num_programs(0) - 1
    slot = lax.rem(i, 2)

    def start(tile, s):
        for c in range(2):
            _start_row_gather(yb_hbm, lambda k: _slot_of(code_ref[k], seg_ref), tile * 2 * tm + c, 2,
                              ybuf.at[s], c * tm, sems.at[s], tm)

    @pl.when(i == 0)
    def _():
        start(0, 0)

    _wait_rows(yb_hbm, ybuf.at[slot], sems.at[slot])
    nxt = jnp.minimum(i + 1, last)
    for s in range(2):
        @pl.when(slot == s)
        def _():
            start(nxt, 1 - s)
            route = route_ref[...]
            w1 = route[:, 2:3]
            w2 = route[:, 3:4]
            h = h_ref[...] + w1 * ybuf[s, :tm, :] + w2 * ybuf[s, tm:, :]
            o_ref[...] = h * lax.rsqrt(jnp.mean(h * h, axis=-1, keepdims=True) + RMS_EPS) * gf_ref[...]

    @pl.when(i == last)
    def _():
        _wait_rows(yb_hbm, ybuf.at[1 - slot], sems.at[1 - slot])


def _combine(code, seg, h, route, gf, yb, tm):
    t, d = h.shape
    return pl.pallas_call(
        functools.partial(_combine_kernel, tm=tm),
        grid_spec=pltpu.PrefetchScalarGridSpec(
            num_scalar_prefetch=2, grid=(t // tm,),
            in_specs=[pl.BlockSpec((tm, d), lambda i, c, s: (i, 0)), pl.BlockSpec((tm, LANES), lambda i, c, s: (i, 0)),
                      pl.BlockSpec((1, d), lambda i, c, s: (0, 0)), pl.BlockSpec(memory_space=pl.ANY)],
            out_specs=pl.BlockSpec((tm, d), lambda i, c, s: (i, 0)),
            scratch_shapes=[pltpu.VMEM((2, 2 * tm) + yb.shape[1:], F32), pltpu.SemaphoreType.DMA((2,))],
        ),
        out_shape=jax.ShapeDtypeStruct((t, d), F32),
        compiler_params=_cparams(("arbitrary",)), name="moe_combine",
    )(code, seg, h, route, gf, yb)


def _slot_of(code, seg_ref):
    return seg_ref[lax.shift_right_logical(code, RANK_BITS)] + jnp.bitwise_and(code, (1 << RANK_BITS) - 1)


def _dispatch_plan(route, counts, tm):
    t = route.shape[0]
    n = 2 * t
    nb = (n + N_EXPERTS * (tm - 1)) // tm + 1
    cnt = counts[0, :N_EXPERTS].astype(jnp.int32)
    padded = ((cnt + tm - 1) // tm) * tm
    pend = jnp.cumsum(padded)
    pstart = pend - padded
    seg = jnp.concatenate([pstart, pstart + cnt, pend]).astype(jnp.int32)
    blk_start = jnp.arange(nb, dtype=jnp.int32) * tm
    blk_e = jnp.minimum(jnp.sum((blk_start[:, None] >= pend[None, :]).astype(jnp.int32), axis=1), N_EXPERTS - 1)
    n_used = (pend[-1:] // tm).astype(jnp.int32)
    code = route[:, :2].astype(jnp.int32).reshape(n)
    return nb * tm, blk_e.astype(jnp.int32), n_used, code, seg


def kernel(x, meta_tokens, norm_mix_g, w_in, ssm_lambda_re, ssm_lambda_im, ssm_log_dt, ssm_b_re, ssm_b_im,
           ssm_c_re, ssm_c_im, ssm_d, ssm_glu_w, ssm_glu_b, w_branch_attn, w_branch_ssm, w_out, norm_ffn_g,
           router_group_w, router_group_b, router_expert_w, router_expert_b, expert_w1, expert_w3, expert_w2,
           norm_final_g):
    b, l, d = x.shape
    depth = w_in.shape[0]
    aw = w_branch_attn.shape[1]
    sw = w_branch_ssm.shape[1]
    heads = aw // HEAD_DIM
    t = b * l
    pad = (-N_META) % Q_BLOCK
    assert depth == 1 and l % Q_BLOCK == 0 and pad + N_META == Q_BLOCK and SSM_CHUNK == Q_BLOCK
    layer = 0
    h_real = x.reshape(t, d)
    h_meta = meta_tokens.astype(x.dtype)
    tm_tok = 512

    wi = w_in[layer].astype(BF16)
    o = 3 * aw + sw
    ws = (wi[:, :aw], wi[:, aw:2 * aw].T, wi[:, 2 * aw:3 * aw], wi[:, 3 * aw:o].T, wi[:, o:o + d], wi[:, o + d:])
    g_mix = norm_mix_g[layer].reshape(1, d)
    q, kt, v, ut, ga, gb = _inproj(h_real, g_mix, ws, tm_tok)
    _, kt_m, v_m, ut_m, _, _ = _inproj(h_meta, g_mix, ws, N_META)
    kt_meta = jnp.concatenate([jnp.zeros((aw, pad), BF16), kt_m], axis=1)
    v_meta = jnp.concatenate([jnp.zeros((pad, aw), BF16), v_m], axis=0)
    ut_meta = jnp.concatenate([jnp.zeros((sw, pad), BF16), ut_m], axis=1)

    attn = _attention(q, kt, v, kt_meta, v_meta, b, heads, pad)

    ssm_params = (ssm_lambda_re[layer], ssm_lambda_im[layer], ssm_log_dt[layer], ssm_b_re[layer], ssm_b_im[layer],
                  ssm_c_re[layer], ssm_c_im[layer], ssm_d[layer])
    yt = _ssm(ut, ut_meta, ssm_params, b)

    wr = jnp.zeros((d, LANES), F32)
    wr = wr.at[:, :N_EXPERTS].set(router_expert_w[layer]).at[:, N_EXPERTS:N_EXPERTS + MOE_GROUPS].set(router_group_w[layer])
    br = jnp.zeros((1, LANES), F32)
    br = br.at[0, :N_EXPERTS].set(router_expert_b[layer]).at[0, N_EXPERTS:N_EXPERTS + MOE_GROUPS].set(router_group_b[layer])
    h_mid, hn, route, counts = _merge(
        h_real, attn, yt, ga, gb, ssm_glu_w[layer].T.astype(BF16), ssm_glu_b[layer].reshape(sw, 1).astype(F32),
        w_branch_attn[layer].astype(BF16), w_branch_ssm[layer].astype(BF16), w_out[layer].astype(BF16),
        norm_ffn_g[layer].reshape(1, d), wr, br, tm_tok)

    cap, blk_e, n_used, code, seg = _dispatch_plan(route, counts, MOE_TILE)
    xs = _dispatch(hn, code, seg, cap, 256)
    yb = _experts(xs, blk_e, n_used, expert_w1[layer], expert_w3[layer], expert_w2[layer], MOE_TILE)
    out = _combine(code, seg, h_mid, route, norm_final_g.reshape(1, d), yb, 256)
    return out.reshape(b, l, d)
```

```python
import functools

import jax
import jax.numpy as jnp
from jax import lax
from jax.experimental import pallas as pl
from jax.experimental.pallas import tpu as pltpu

F32 = jnp.float32
BF16 = jnp.bfloat16

N_META = 16
Q_BLOCK = 128
HEAD_DIM = 64
SSM_GROUP_CH = 16
SSM_STATE = 64
MOE_GROUPS = 4
EXPERTS_PER_GROUP = 8
N_EXPERTS = MOE_GROUPS * EXPERTS_PER_GROUP
RMS_EPS = 1e-6

LANES = 128
SUBLANES = 8
SSM_CHUNK = LANES
MOE_TILE = 512
STICK_CUTOFF = 104.0
VMEM_LIMIT = 52 * 1024 * 1024

_NT = (((1,), (1,)), ((), ()))
_TN = (((0,), (0,)), ((), ()))


def _cparams(sem):
    return pltpu.CompilerParams(dimension_semantics=sem, vmem_limit_bytes=VMEM_LIMIT)


def _inproj_kernel(x_ref, g_ref, wq_ref, wkt_ref, wv_ref, wut_ref, wga_ref, wgb_ref,
                   q_ref, kt_ref, v_ref, ut_ref, ga_ref, gb_ref):
    x = x_ref[...]
    ms = jnp.mean(x * x, axis=-1, keepdims=True)
    xn = (x * lax.rsqrt(ms + RMS_EPS) * g_ref[...]).astype(BF16)
    q_ref[...] = (jnp.dot(xn, wq_ref[...], preferred_element_type=F32) * (HEAD_DIM ** -0.5)).astype(BF16)
    kt_ref[...] = lax.dot_general(wkt_ref[...], xn, _NT, preferred_element_type=F32).astype(BF16)
    v_ref[...] = jnp.dot(xn, wv_ref[...], preferred_element_type=F32).astype(BF16)
    ut_ref[...] = lax.dot_general(wut_ref[...], xn, _NT, preferred_element_type=F32).astype(BF16)
    ga_ref[...] = jax.nn.sigmoid(jnp.dot(xn, wga_ref[...], preferred_element_type=F32)).astype(BF16)
    gb_ref[...] = jax.nn.sigmoid(jnp.dot(xn, wgb_ref[...], preferred_element_type=F32)).astype(BF16)


def _inproj(x2, g, ws, tm):
    m, d = x2.shape
    aw, sw = ws[0].shape[1], ws[3].shape[0]
    row = lambda n: pl.BlockSpec((tm, n), lambda i: (i, 0))
    col = lambda n: pl.BlockSpec((n, tm), lambda i: (0, i))
    full = lambda a: pl.BlockSpec(a.shape, lambda i: (0, 0))
    sds = jax.ShapeDtypeStruct
    return pl.pallas_call(
        _inproj_kernel,
        grid=(m // tm,),
        in_specs=[row(d), full(g)] + [full(w) for w in ws],
        out_specs=[row(aw), col(aw), row(aw), col(sw), row(d), row(d)],
        out_shape=[sds((m, aw), BF16), sds((aw, m), BF16), sds((m, aw), BF16), sds((sw, m), BF16),
                   sds((m, d), BF16), sds((m, d), BF16)],
        compiler_params=_cparams(("parallel",)),
        name="inproj",
    )(x2, g, *ws)


def _attn_kernel(q_ref, kt_ref, v_ref, ktm_ref, vm_ref, tri_ref, o_ref, qm_ref, carry_ref, acc_ref, *, heads, pad):
    i = pl.program_id(1)
    pairs = heads // 2
    carry_ref[...] = jnp.zeros_like(carry_ref)
    acc_ref[...] = jnp.zeros_like(acc_ref)
    lane = lax.broadcasted_iota(jnp.int32, (Q_BLOCK, LANES), 1)
    row = lax.broadcasted_iota(jnp.int32, (Q_BLOCK, LANES), 0)
    low_half = lane < HEAD_DIM
    rows = lambda h: slice(h * Q_BLOCK, (h + 1) * Q_BLOCK)
    for p in range(pairs):
        qp = q_ref[0, :, p * LANES:(p + 1) * LANES]
        qm_ref[rows(2 * p)] = jnp.where(low_half, qp, jnp.zeros_like(qp))
        qm_ref[rows(2 * p + 1)] = jnp.where(low_half, jnp.zeros_like(qp), qp)

    def block_step(get_kt, get_v, valid):
        kts = [get_kt(p) for p in range(pairs)]
        vs = [get_v(p) for p in range(pairs)]
        zs = [jnp.dot(qm_ref[rows(h)], kts[h // 2], preferred_element_type=F32) for h in range(heads)]
        cats = []
        for h in range(heads):
            z = zs[h]
            drop = jnp.maximum(z, 0.0) + jnp.log(1.0 + jnp.exp(-jnp.abs(z)))
            if valid is not None:
                drop = jnp.where(valid, drop, 0.0)
            hi = drop.astype(BF16)
            lo = (drop - hi.astype(F32)).astype(BF16)
            cats.append(jnp.concatenate([hi, lo], axis=1))
        sums = [jnp.dot(cats[h], tri_ref[...], preferred_element_type=F32) for h in range(heads)]
        low = None
        for h in range(heads):
            carry = carry_ref[rows(h)]
            w = jnp.exp(zs[h] - (sums[h][:, :LANES] + carry))
            if valid is not None:
                w = jnp.where(valid, w, 0.0)
            acc_ref[rows(h)] += jnp.dot(w.astype(BF16), vs[h // 2], preferred_element_type=F32)
            carry = carry + sums[h][:, LANES:]
            carry_ref[rows(h)] = carry
            low = carry if low is None else jnp.minimum(low, carry)
        return jnp.min(low)

    off_d = pl.multiple_of(i * Q_BLOCK, Q_BLOCK)
    low0 = block_step(lambda p: kt_ref[p * LANES:(p + 1) * LANES, pl.ds(off_d, Q_BLOCK)],
                      lambda p: v_ref[0, pl.ds(off_d, Q_BLOCK), p * LANES:(p + 1) * LANES],
                      lane < row)

    def cond(state):
        kb, low = state
        return jnp.logical_and(kb >= 0, low < STICK_CUTOFF)

    def body(state):
        kb, _ = state
        off = pl.multiple_of(kb * Q_BLOCK, Q_BLOCK)
        low = block_step(lambda p: kt_ref[p * LANES:(p + 1) * LANES, pl.ds(off, Q_BLOCK)],
                         lambda p: v_ref[0, pl.ds(off, Q_BLOCK), p * LANES:(p + 1) * LANES],
                         None)
        return kb - 1, low

    _, low1 = lax.while_loop(cond, body, (i - 1, low0))

    @pl.when(low1 < STICK_CUTOFF)
    def _():
        block_step(lambda p: ktm_ref[p * LANES:(p + 1) * LANES, :],
                   lambda p: vm_ref[:, p * LANES:(p + 1) * LANES],
                   lane >= pad)

    for p in range(pairs):
        even = acc_ref[2 * p * Q_BLOCK:(2 * p + 1) * Q_BLOCK]
        odd = acc_ref[(2 * p + 1) * Q_BLOCK:(2 * p + 2) * Q_BLOCK]
        o_ref[0, :, p * LANES:(p + 1) * LANES] = jnp.where(low_half, even, odd).astype(o_ref.dtype)


def _attention(q, kt, v, kt_meta, v_meta, b, heads, pad):
    t, aw = q.shape
    l = t // b
    r = lax.broadcasted_iota(jnp.int32, (2 * LANES, 2 * LANES), 0) % LANES
    c = lax.broadcasted_iota(jnp.int32, (2 * LANES, 2 * LANES), 1)
    tri = jnp.where(jnp.logical_or(c >= LANES, r >= c), 1.0, 0.0).astype(BF16)
    once = pl.Buffered(1)
    out = pl.pallas_call(
        functools.partial(_attn_kernel, heads=heads, pad=pad),
        grid=(b, l // Q_BLOCK),
        in_specs=[
            pl.BlockSpec((1, Q_BLOCK, aw), lambda bi, i: (bi, i, 0)),
            pl.BlockSpec((aw, l), lambda bi, i: (0, bi), pipeline_mode=once),
            pl.BlockSpec((1, l, aw), lambda bi, i: (bi, 0, 0), pipeline_mode=once),
            pl.BlockSpec((aw, Q_BLOCK), lambda bi, i: (0, 0)),
            pl.BlockSpec((Q_BLOCK, aw), lambda bi, i: (0, 0)),
            pl.BlockSpec((2 * LANES, 2 * LANES), lambda bi, i: (0, 0)),
        ],
        out_specs=pl.BlockSpec((1, Q_BLOCK, aw), lambda bi, i: (bi, i, 0)),
        out_shape=jax.ShapeDtypeStruct((b, l, aw), BF16),
        scratch_shapes=[pltpu.VMEM((heads * Q_BLOCK, LANES), BF16), pltpu.VMEM((heads * Q_BLOCK, LANES), F32),
                        pltpu.VMEM((heads * Q_BLOCK, LANES), F32)],
        compiler_params=_cparams(("parallel", "arbitrary")),
        name="stick_attn",
    )(q.reshape(b, l, aw), kt, v.reshape(b, l, aw), kt_meta, v_meta, tri)
    return out.reshape(t, aw)


def _ssm_tables(lam_re, lam_im, log_dt, b_re, b_im, c_re, c_im, d_skip):
    hp = lax.Precision.HIGHEST
    tc = SSM_CHUNK
    g, p = lam_re.shape
    ch = b_re.shape[-1]
    dt = jnp.exp(log_dt)[:, None]
    mag = jnp.exp(lam_re * dt)
    lb_re, lb_im = mag * jnp.cos(lam_im * dt), mag * jnp.sin(lam_im * dt)
    nr, ni = lb_re - 1.0, lb_im
    den = lam_re * lam_re + lam_im * lam_im
    f_re = (nr * lam_re + ni * lam_im) / den
    f_im = (ni * lam_re - nr * lam_im) / den
    bb_re = f_re[:, :, None] * b_re - f_im[:, :, None] * b_im
    bb_im = f_re[:, :, None] * b_im + f_im[:, :, None] * b_re
    pw_re, pw_im = jnp.ones((1, g, p), F32), jnp.zeros((1, g, p), F32)
    s_re, s_im = lb_re, lb_im
    while pw_re.shape[0] < tc + 1:
        n_re = pw_re * s_re - pw_im * s_im
        n_im = pw_re * s_im + pw_im * s_re
        pw_re, pw_im = jnp.concatenate([pw_re, n_re]), jnp.concatenate([pw_im, n_im])
        s_re, s_im = s_re * s_re - s_im * s_im, 2.0 * s_re * s_im
    dec_re = pw_re[tc].reshape(g // 2, 1, 2 * p)
    dec_im = pw_im[tc].reshape(g // 2, 1, 2 * p)
    pw_re, pw_im = pw_re[:tc + 1].transpose(1, 2, 0), pw_im[:tc + 1].transpose(1, 2, 0)
    ct_re, ct_im = c_re.transpose(0, 2, 1)[..., None], c_im.transpose(0, 2, 1)[..., None]
    cl_re = ct_re * pw_re[:, :, None, :] - ct_im * pw_im[:, :, None, :]
    cl_im = ct_re * pw_im[:, :, None, :] + ct_im * pw_re[:, :, None, :]
    kern = (jnp.einsum('gpot,gpi->giot', cl_re[..., :tc], bb_re, precision=hp)
            - jnp.einsum('gpot,gpi->giot', cl_im[..., :tc], bb_im, precision=hp))
    kern = kern.at[:, :, :, 0].add(jnp.eye(ch, dtype=F32)[None] * d_skip.reshape(g, ch, 1))
    half = (jnp.arange(g) % 2)[:, None, None]

    def place_rows(m):
        m = m.reshape(g, p, ch * tc).astype(BF16)
        z = jnp.zeros_like(m)
        return jnp.where(half == 0, jnp.concatenate([m, z], axis=1), jnp.concatenate([z, m], axis=1))

    rp_re, rp_im = pw_re[:, :, None, tc - 1::-1], pw_im[:, :, None, tc - 1::-1]
    bq_re, bq_im = bb_re[..., None], bb_im[..., None]
    ms_re = place_rows(rp_re * bq_re - rp_im * bq_im)
    ms_im = place_rows(rp_re * bq_im + rp_im * bq_re)
    mo_re = place_rows(cl_re[..., 1:])
    mo_im = place_rows(-cl_im[..., 1:])
    return kern, ms_re, ms_im, mo_re, mo_im, dec_re, dec_im


def _ssm_kernel(u_ref, um_ref, k_ref, msr_ref, msi_ref, mor_ref, moi_ref, dre_ref, dim_ref, y_ref,
                toep_ref, sre_ref, sim_ref, xre_ref, xim_ref, *, batch):
    ch, tc = SSM_GROUP_CH, SSM_CHUNK
    nch = u_ref.shape[1]
    per_b = nch // batch

    def chunk_rows(ref, gl):
        return jnp.concatenate([ref[gl * ch + ci] for ci in range(ch)], axis=1)

    u = [chunk_rows(u_ref, gl) for gl in range(2)]
    um = [jnp.concatenate([jnp.broadcast_to(um_ref[gl * ch + ci:gl * ch + ci + 1, :], (8, tc)) for ci in range(ch)], axis=1)
          for gl in range(2)]

    def to_state(lhs, ms_ref):
        return (lax.dot_general(lhs[0], ms_ref[0], _NT, preferred_element_type=F32)
                + lax.dot_general(lhs[1], ms_ref[1], _NT, preferred_element_type=F32))

    sre_ref[...] = to_state(u, msr_ref)
    sim_ref[...] = to_state(u, msi_ref)
    x0_re = to_state(um, msr_ref)[0:1]
    x0_im = to_state(um, msi_ref)[0:1]
    d_re, d_im = dre_ref[0], dim_ref[0]

    def step(c, xs):
        new = []
        for bi in range(batch):
            x_re, x_im = xs[2 * bi], xs[2 * bi + 1]
            r = bi * per_b + c
            xre_ref[pl.ds(r, 1), :] = x_re
            xim_ref[pl.ds(r, 1), :] = x_im
            new.append(d_re * x_re - d_im * x_im + sre_ref[pl.ds(r, 1), :])
            new.append(d_re * x_im + d_im * x_re + sim_ref[pl.ds(r, 1), :])
        return tuple(new)

    lax.fori_loop(0, per_b, step, (x0_re, x0_im) * batch)
    xs_re = xre_ref[...].astype(BF16)
    xs_im = xim_ref[...].astype(BF16)

    lane = lax.broadcasted_iota(jnp.int32, (tc, tc), 1)
    row = lax.broadcasted_iota(jnp.int32, (tc, tc), 0)
    causal = lane >= row
    for gl in range(2):
        def fill(ci, carry):
            for co in range(ch):
                taps = jnp.broadcast_to(k_ref[gl, ci, pl.ds(co, 1), :], (tc, tc))
                blk = pltpu.roll(taps, 0, 1, stride=1, stride_axis=0)
                blk = jnp.where(causal, blk, 0.0).astype(BF16)
                toep_ref[pl.ds(pl.multiple_of(ci * tc, tc), tc), co * tc:(co + 1) * tc] = blk
            return carry
        lax.fori_loop(0, ch, fill, 0)
        y = jnp.dot(u[gl], toep_ref[...], preferred_element_type=F32)
        y = y + jnp.dot(xs_re, mor_ref[gl], preferred_element_type=F32)
        y = y + jnp.dot(xs_im, moi_ref[gl], preferred_element_type=F32)
        y = jax.nn.gelu(y).astype(y_ref.dtype)
        for co in range(ch):
            y_ref[gl * ch + co] = y[:, co * tc:(co + 1) * tc]


def _ssm(ut, ut_meta, params, batch):
    s, t = ut.shape
    tc, ch, p = SSM_CHUNK, SSM_GROUP_CH, SSM_STATE
    g = s // ch
    nch = t // tc
    kern, ms_re, ms_im, mo_re, mo_im, dec_re, dec_im = _ssm_tables(*params)
    u3 = ut.reshape(s, nch, tc)
    pair3 = lambda a, b_: pl.BlockSpec((2, a, b_), lambda j: (j, 0, 0))
    y3 = pl.pallas_call(
        functools.partial(_ssm_kernel, batch=batch),
        grid=(g // 2,),
        in_specs=[
            pl.BlockSpec((2 * ch, nch, tc), lambda j: (j, 0, 0)),
            pl.BlockSpec((2 * ch, tc), lambda j: (j, 0)),
            pl.BlockSpec((2, ch, ch, tc), lambda j: (j, 0, 0, 0)),
            pair3(2 * p, ch * tc), pair3(2 * p, ch * tc), pair3(2 * p, ch * tc), pair3(2 * p, ch * tc),
            pl.BlockSpec((1, 1, 2 * p), lambda j: (j, 0, 0)), pl.BlockSpec((1, 1, 2 * p), lambda j: (j, 0, 0)),
        ],
        out_specs=pl.BlockSpec((2 * ch, nch, tc), lambda j: (j, 0, 0)),
        out_shape=jax.ShapeDtypeStruct((s, nch, tc), BF16),
        scratch_shapes=[pltpu.VMEM((ch * tc, ch * tc), BF16)] + [pltpu.VMEM((nch, 2 * p), F32)] * 4,
        compiler_params=_cparams(("parallel",)), name="ssm_chunked",
    )(u3, ut_meta, kern, ms_re, ms_im, mo_re, mo_im, dec_re, dec_im)
    return y3.reshape(s, t)


RANK_BITS = 16


def _merge_kernel(x_ref, a_ref, yt_ref, ga_ref, gb_ref, gwt_ref, gbias_ref, wpa_ref, wpb_ref, wo_ref, gn_ref,
                  wr_ref, br_ref, ltri_ref, h_ref, hn_ref, route_ref, count_ref, run_ref):
    @pl.when(pl.program_id(0) == 0)
    def _():
        run_ref[...] = jnp.zeros_like(run_ref)

    yt = yt_ref[...]
    gate = jax.nn.sigmoid(jnp.dot(gwt_ref[...], yt, preferred_element_type=F32) + gbias_ref[...])
    ssm = (yt.astype(F32) * gate).T.astype(BF16)
    merged = (ga_ref[...].astype(F32) * jnp.dot(a_ref[...], wpa_ref[...], preferred_element_type=F32)
              + gb_ref[...].astype(F32) * jnp.dot(ssm, wpb_ref[...], preferred_element_type=F32))
    h = x_ref[...] + jnp.dot(merged.astype(BF16), wo_ref[...], preferred_element_type=F32)
    h_ref[...] = h
    hn = h * lax.rsqrt(jnp.mean(h * h, axis=-1, keepdims=True) + RMS_EPS) * gn_ref[...]
    hn_ref[...] = hn
    hn_hi = hn.astype(BF16)
    hn_lo = (hn - hn_hi.astype(F32)).astype(BF16)
    logits = (jnp.dot(hn_hi, wr_ref[0], preferred_element_type=F32)
              + jnp.dot(hn_hi, wr_ref[1], preferred_element_type=F32)
              + jnp.dot(hn_lo, wr_ref[0], preferred_element_type=F32)) + br_ref[...]
    lane = lax.broadcasted_iota(jnp.int32, logits.shape, 1)
    neg = jnp.full_like(logits, -jnp.inf)
    big = jnp.int32(LANES)
    is_grp = jnp.logical_and(lane >= N_EXPERTS, lane < N_EXPERTS + MOE_GROUPS)
    glog = jnp.where(is_grp, logits, neg)
    gmax = jnp.max(glog, axis=-1, keepdims=True)
    g_top = jnp.min(jnp.where(glog == gmax, lane, big), axis=-1, keepdims=True) - N_EXPERTS
    p_top = 1.0 / jnp.sum(jnp.exp(glog - gmax), axis=-1, keepdims=True)
    in_grp = (lane // EXPERTS_PER_GROUP) == g_top
    elog = jnp.where(jnp.logical_and(in_grp, lane < N_EXPERTS), logits, neg)
    e1 = jnp.max(elog, axis=-1, keepdims=True)
    i1 = jnp.min(jnp.where(elog == e1, lane, big), axis=-1, keepdims=True)
    elog2 = jnp.where(lane == i1, neg, elog)
    e2 = jnp.max(elog2, axis=-1, keepdims=True)
    i2 = jnp.min(jnp.where(elog2 == e2, lane, big), axis=-1, keepdims=True)
    t = jnp.exp(e2 - e1)
    w1 = p_top / (1.0 + t)
    w2 = p_top * t / (1.0 + t)
    hit1, hit2 = lane == i1, lane == i2
    chosen = jnp.where(jnp.logical_or(hit1, hit2), 1.0, 0.0)
    before = jnp.dot(ltri_ref[...], chosen.astype(BF16), preferred_element_type=F32) + run_ref[0:1, :]
    r1 = jnp.sum(jnp.where(hit1, before, 0.0), axis=-1, keepdims=True)
    r2 = jnp.sum(jnp.where(hit2, before, 0.0), axis=-1, keepdims=True)
    run_ref[...] = run_ref[...] + jnp.sum(chosen, axis=0, keepdims=True)
    count_ref[...] = run_ref[...]
    scale = float(2 ** RANK_BITS)
    route = jnp.where(lane == 0, i1.astype(F32) * scale + r1,
                      jnp.where(lane == 1, i2.astype(F32) * scale + r2,
                                jnp.where(lane == 2, w1, jnp.where(lane == 3, w2, 0.0))))
    route_ref[...] = route


def _merge(x2, attn, yt, ga, gb, glu_wt, glu_b, wpa, wpb, wo, gn, wr, br, tm):
    m, d = x2.shape
    aw, sw = attn.shape[1], yt.shape[0]
    row = lambda n: pl.BlockSpec((tm, n), lambda i: (i, 0))
    full = lambda a: pl.BlockSpec(a.shape, lambda i: (0,) * a.ndim)
    wr_hi = wr.astype(BF16)
    wr = jnp.stack([wr_hi, (wr - wr_hi.astype(F32)).astype(BF16)])
    ltri = jnp.where(lax.broadcasted_iota(jnp.int32, (tm, tm), 0) > lax.broadcasted_iota(jnp.int32, (tm, tm), 1),
                     1.0, 0.0).astype(BF16)
    ws = (glu_wt, glu_b, wpa, wpb, wo, gn, wr, br, ltri)
    return pl.pallas_call(
        _merge_kernel, grid=(m // tm,),
        in_specs=[row(d), row(aw), pl.BlockSpec((sw, tm), lambda i: (0, i)), row(d), row(d)] + [full(a) for a in ws],
        out_specs=[row(d), row(d), row(LANES), pl.BlockSpec((8, LANES), lambda i: (0, 0))],
        out_shape=[jax.ShapeDtypeStruct((m, d), F32), jax.ShapeDtypeStruct((m, d), F32),
                   jax.ShapeDtypeStruct((m, LANES), F32), jax.ShapeDtypeStruct((8, LANES), F32)],
        scratch_shapes=[pltpu.VMEM((8, LANES), F32)],
        compiler_params=_cparams(("arbitrary",)), name="merge_router",
    )(x2, attn, yt, ga, gb, *ws)


def _start_row_gather(src_hbm, row_of, base, stride, buf, row0, sem, n):
    for r in range(n):
        row = row_of(base + stride * r)
        pltpu.make_async_copy(src_hbm.at[pl.ds(row, 1)], buf.at[pl.ds(row0 + r, 1)], sem).start(priority=r % 2)


def _wait_rows(src_hbm, buf, sem):
    pltpu.make_async_copy(src_hbm.at[pl.ds(0, buf.shape[0])], buf, sem).wait()


def _dispatch_kernel(code_ref, seg_ref, hn_ref, xs_hbm, zero_ref, sem, zsem, *, tm):
    i = pl.program_id(0)

    @pl.when(i == 0)
    def _():
        zero_ref[...] = jnp.zeros_like(zero_ref)
        zrows = zero_ref.shape[0]

        def zero_run(lo, hi, act):
            def copy(start, n):
                return pltpu.make_async_copy(zero_ref.at[pl.ds(0, n)], xs_hbm.at[pl.ds(start, n)], zsem)
            head = jnp.minimum(jnp.bitwise_and(-lo, SUBLANES - 1), hi - lo)
            lax.fori_loop(0, head, lambda k, c: (act(copy(lo + k, 1)), c)[1], 0)
            lo = lo + head
            n_full = (hi - lo) // zrows
            lax.fori_loop(0, n_full,
                          lambda k, c: (act(copy(pl.multiple_of(lo + k * zrows, SUBLANES), zrows)), c)[1], 0)
            rest = (hi - lo) - n_full * zrows
            start = lo + n_full * zrows
            bit = zrows // 2
            while bit >= SUBLANES:
                pl.when(jnp.bitwise_and(rest, bit) != 0)(
                    functools.partial(lambda st, n: act(copy(pl.multiple_of(st, SUBLANES), n)), start, bit))
                start = start + jnp.bitwise_and(rest, bit)
                bit //= 2

        def each_gap(act):
            def run(e, carry):
                zero_run(seg_ref[N_EXPERTS + e], seg_ref[2 * N_EXPERTS + e], act)
                return carry
            lax.fori_loop(0, N_EXPERTS, run, 0)
            zero_run(seg_ref[3 * N_EXPERTS - 1], xs_hbm.shape[0], act)
        each_gap(lambda cp: cp.start())
        each_gap(lambda cp: cp.wait())

    for r in range(tm):
        for c in range(2):
            slot = _slot_of(code_ref[2 * (i * tm + r) + c], seg_ref)
            pltpu.make_async_copy(hn_ref.at[pl.ds(r, 1)], xs_hbm.at[pl.ds(slot, 1)], sem).start(priority=c)
    pltpu.make_async_copy(xs_hbm.at[pl.ds(0, 2 * tm)], xs_hbm.at[pl.ds(0, 2 * tm)], sem).wait()


def _dispatch(hn, code, seg, cap, tm):
    t, d = hn.shape
    return pl.pallas_call(
        functools.partial(_dispatch_kernel, tm=tm),
        grid_spec=pltpu.PrefetchScalarGridSpec(
            num_scalar_prefetch=2, grid=(t // tm,),
            in_specs=[pl.BlockSpec((tm, d), lambda i, c, s: (i, 0))],
            out_specs=pl.BlockSpec(memory_space=pl.ANY),
            scratch_shapes=[pltpu.VMEM((MOE_TILE, d), F32), pltpu.SemaphoreType.DMA(()), pltpu.SemaphoreType.DMA(())],
        ),
        out_shape=jax.ShapeDtypeStruct((cap, d), F32),
        compiler_params=_cparams(("arbitrary",)), name="moe_dispatch",
    )(code, seg, hn)


def _expert_kernel(blk_e_ref, n_used_ref, x_ref, w1_ref, w3_ref, w2_ref, y_ref):
    j = pl.program_id(0)

    @pl.when(j < n_used_ref[0])
    def _():
        x = x_ref[...].astype(BF16)
        a = jnp.dot(x, w1_ref[0].astype(BF16), preferred_element_type=F32)
        b = jnp.dot(x, w3_ref[0].astype(BF16), preferred_element_type=F32)
        hdn = (a * jax.nn.sigmoid(a) * b).astype(BF16)
        y_ref[...] = jnp.dot(hdn, w2_ref[0].astype(BF16), preferred_element_type=F32)

    @pl.when(j >= n_used_ref[0])
    def _():
        y_ref[...] = jnp.zeros_like(y_ref)


def _experts(xs, blk_e, n_used, w1, w3, w2, tm):
    cap, d = xs.shape
    e, _, ff = w1.shape
    nb = cap // tm
    wspec = lambda r, c: pl.BlockSpec((1, r, c), lambda j, be, nu: (be[j], 0, 0))
    xrow = lambda j, be, nu: (jnp.minimum(j, nu[0] - 1), 0)
    return pl.pallas_call(
        _expert_kernel,
        grid_spec=pltpu.PrefetchScalarGridSpec(
            num_scalar_prefetch=2, grid=(nb,),
            in_specs=[pl.BlockSpec((tm, d), xrow), wspec(d, ff), wspec(d, ff), wspec(ff, d)],
            out_specs=pl.BlockSpec((tm, d), lambda j, be, nu: (j, 0)),
        ),
        out_shape=jax.ShapeDtypeStruct((cap, d), F32),
        compiler_params=_cparams(("arbitrary",)), name="expert_ffn",
    )(blk_e, n_used, xs, w1, w3, w2)


def _combine_kernel(code_ref, seg_ref, h_ref, route_ref, gf_ref, yb_hbm, o_ref, ybuf, sems, *, tm):
    i = pl.program_id(0)
    last = pl.num_programs(0) - 1
    slot = lax.rem(i, 2)

    def start(tile, s):
        for c in range(2):
            _start_row_gather(yb_hbm, lambda k: _slot_of(code_ref[k], seg_ref), tile * 2 * tm + c, 2,
                              ybuf.at[s], c * tm, sems.at[s], tm)

    @pl.when(i == 0)
    def _():
        start(0, 0)

    _wait_rows(yb_hbm, ybuf.at[slot], sems.at[slot])
    nxt = jnp.minimum(i + 1, last)
    for s in range(2):
        @pl.when(slot == s)
        def _():
            start(nxt, 1 - s)
            route = route_ref[...]
            w1 = route[:, 2:3]
            w2 = route[:, 3:4]
            h = h_ref[...] + w1 * ybuf[s, :tm, :] + w2 * ybuf[s, tm:, :]
            o_ref[...] = h * lax.rsqrt(jnp.mean(h * h, axis=-1, keepdims=True) + RMS_EPS) * gf_ref[...]

    @pl.when(i == last)
    def _():
        _wait_rows(yb_hbm, ybuf.at[1 - slot], sems.at[1 - slot])


def _combine(code, seg, h, route, gf, yb, tm):
    t, d = h.shape
    return pl.pallas_call(
        functools.partial(_combine_kernel, tm=tm),
        grid_spec=pltpu.PrefetchScalarGridSpec(
            num_scalar_prefetch=2, grid=(t // tm,),
            in_specs=[pl.BlockSpec((tm, d), lambda i, c, s: (i, 0)), pl.BlockSpec((tm, LANES), lambda i, c, s: (i, 0)),
                      pl.BlockSpec((1, d), lambda i, c, s: (0, 0)), pl.BlockSpec(memory_space=pl.ANY)],
            out_specs=pl.BlockSpec((tm, d), lambda i, c, s: (i, 0)),
            scratch_shapes=[pltpu.VMEM((2, 2 * tm) + yb.shape[1:], F32), pltpu.SemaphoreType.DMA((2,))],
        ),
        out_shape=jax.ShapeDtypeStruct((t, d), F32),
        compiler_params=_cparams(("arbitrary",)), name="moe_combine",
    )(code, seg, h, route, gf, yb)


def _slot_of(code, seg_ref):
    return seg_ref[lax.shift_right_logical(code, RANK_BITS)] + jnp.bitwise_and(code, (1 << RANK_BITS) - 1)


def _dispatch_plan(route, counts, tm):
    t = route.shape[0]
    n = 2 * t
    nb = (n + N_EXPERTS * (tm - 1)) // tm + 1
    cnt = counts[0, :N_EXPERTS].astype(jnp.int32)
    padded = ((cnt + tm - 1) // tm) * tm
    pend = jnp.cumsum(padded)
    pstart = pend - padded
    seg = jnp.concatenate([pstart, pstart + cnt, pend]).astype(jnp.int32)
    blk_start = jnp.arange(nb, dtype=jnp.int32) * tm
    blk_e = jnp.minimum(jnp.sum((blk_start[:, None] >= pend[None, :]).astype(jnp.int32), axis=1), N_EXPERTS - 1)
    n_used = (pend[-1:] // tm).astype(jnp.int32)
    code = route[:, :2].astype(jnp.int32).reshape(n)
    return nb * tm, blk_e.astype(jnp.int32), n_used, code, seg


def kernel(x, meta_tokens, norm_mix_g, w_in, ssm_lambda_re, ssm_lambda_im, ssm_log_dt, ssm_b_re, ssm_b_im,
           ssm_c_re, ssm_c_im, ssm_d, ssm_glu_w, ssm_glu_b, w_branch_attn, w_branch_ssm, w_out, norm_ffn_g,
           router_group_w, router_group_b, router_expert_w, router_expert_b, expert_w1, expert_w3, expert_w2,
           norm_final_g):
    b, l, d = x.shape
    depth = w_in.shape[0]
    aw = w_branch_attn.shape[1]
    sw = w_branch_ssm.shape[1]
    heads = aw // HEAD_DIM
    t = b * l
    pad = (-N_META) % Q_BLOCK
    assert depth == 1 and l % Q_BLOCK == 0 and pad + N_META == Q_BLOCK and SSM_CHUNK == Q_BLOCK
    layer = 0
    h_real = x.reshape(t, d)
    h_meta = meta_tokens.astype(x.dtype)
    tm_tok = 512

    wi = w_in[layer].astype(BF16)
    o = 3 * aw + sw
    ws = (wi[:, :aw], wi[:, aw:2 * aw].T, wi[:, 2 * aw:3 * aw], wi[:, 3 * aw:o].T, wi[:, o:o + d], wi[:, o + d:])
    g_mix = norm_mix_g[layer].reshape(1, d)
    q, kt, v, ut, ga, gb = _inproj(h_real, g_mix, ws, tm_tok)
    _, kt_m, v_m, ut_m, _, _ = _inproj(h_meta, g_mix, ws, N_META)
    kt_meta = jnp.concatenate([jnp.zeros((aw, pad), BF16), kt_m], axis=1)
    v_meta = jnp.concatenate([jnp.zeros((pad, aw), BF16), v_m], axis=0)
    ut_meta = jnp.concatenate([jnp.zeros((sw, pad), BF16), ut_m], axis=1)

    attn = _attention(q, kt, v, kt_meta, v_meta, b, heads, pad)

    ssm_params = (ssm_lambda_re[layer], ssm_lambda_im[layer], ssm_log_dt[layer], ssm_b_re[layer], ssm_b_im[layer],
                  ssm_c_re[layer], ssm_c_im[layer], ssm_d[layer])
    yt = _ssm(ut, ut_meta, ssm_params, b)

    wr = jnp.zeros((d, LANES), F32)
    wr = wr.at[:, :N_EXPERTS].set(router_expert_w[layer]).at[:, N_EXPERTS:N_EXPERTS + MOE_GROUPS].set(router_group_w[layer])
    br = jnp.zeros((1, LANES), F32)
    br = br.at[0, :N_EXPERTS].set(router_expert_b[layer]).at[0, N_EXPERTS:N_EXPERTS + MOE_GROUPS].set(router_group_b[layer])
    h_mid, hn, route, counts = _merge(
        h_real, attn, yt, ga, gb, ssm_glu_w[layer].T.astype(BF16), ssm_glu_b[layer].reshape(sw, 1).astype(F32),
        w_branch_attn[layer].astype(BF16), w_branch_ssm[layer].astype(BF16), w_out[layer].astype(BF16),
        norm_ffn_g[layer].reshape(1, d), wr, br, tm_tok)

    cap, blk_e, n_used, code, seg = _dispatch_plan(route, counts, MOE_TILE)
    xs = _dispatch(hn, code, seg, cap, 256)
    yb = _experts(xs, blk_e, n_used, expert_w1[layer], expert_w3[layer], expert_w2[layer], MOE_TILE)
    out = _combine(code, seg, h_mid, route, norm_final_g.reshape(1, d), yb, 256)
    return out.reshape(b, l, d)
```

```python
import functools

import jax
import jax.numpy as jnp
from jax import lax
from jax.experimental import pallas as pl
from jax.experimental.pallas import tpu as pltpu

F32 = jnp.float32
BF16 = jnp.bfloat16

N_META = 16
Q_BLOCK = 128
HEAD_DIM = 64
SSM_GROUP_CH = 16
SSM_STATE = 64
MOE_GROUPS = 4
EXPERTS_PER_GROUP = 8
N_EXPERTS = MOE_GROUPS * EXPERTS_PER_GROUP
RMS_EPS = 1e-6

LANES = 128
SUBLANES = 8
SSM_CHUNK = LANES
MOE_TILE = 512
STICK_CUTOFF = 104.0
VMEM_LIMIT = 52 * 1024 * 1024

_NT = (((1,), (1,)), ((), ()))
_TN = (((0,), (0,)), ((), ()))


def _cparams(sem):
    return pltpu.CompilerParams(dimension_semantics=sem, vmem_limit_bytes=VMEM_LIMIT)


def _inproj_kernel(x_ref, g_ref, wq_ref, wkt_ref, wv_ref, wut_ref, wga_ref, wgb_ref,
                   q_ref, kt_ref, v_ref, ut_ref, ga_ref, gb_ref):
    x = x_ref[...]
    ms = jnp.mean(x * x, axis=-1, keepdims=True)
    xn = (x * lax.rsqrt(ms + RMS_EPS) * g_ref[...]).astype(BF16)
    q_ref[...] = (jnp.dot(xn, wq_ref[...], preferred_element_type=F32) * (HEAD_DIM ** -0.5)).astype(BF16)
    kt_ref[...] = lax.dot_general(wkt_ref[...], xn, _NT, preferred_element_type=F32).astype(BF16)
    v_ref[...] = jnp.dot(xn, wv_ref[...], preferred_element_type=F32).astype(BF16)
    ut_ref[...] = lax.dot_general(wut_ref[...], xn, _NT, preferred_element_type=F32).astype(BF16)
    ga_ref[...] = jax.nn.sigmoid(jnp.dot(xn, wga_ref[...], preferred_element_type=F32)).astype(BF16)
    gb_ref[...] = jax.nn.sigmoid(jnp.dot(xn, wgb_ref[...], preferred_element_type=F32)).astype(BF16)


def _inproj(x2, g, ws, tm):
    m, d = x2.shape
    aw, sw = ws[0].shape[1], ws[3].shape[0]
    row = lambda n: pl.BlockSpec((tm, n), lambda i: (i, 0))
    col = lambda n: pl.BlockSpec((n, tm), lambda i: (0, i))
    full = lambda a: pl.BlockSpec(a.shape, lambda i: (0, 0))
    sds = jax.ShapeDtypeStruct
    return pl.pallas_call(
        _inproj_kernel,
        grid=(m // tm,),
        in_specs=[row(d), full(g)] + [full(w) for w in ws],
        out_specs=[row(aw), col(aw), row(aw), col(sw), row(d), row(d)],
        out_shape=[sds((m, aw), BF16), sds((aw, m), BF16), sds((m, aw), BF16), sds((sw, m), BF16),
                   sds((m, d), BF16), sds((m, d), BF16)],
        compiler_params=_cparams(("parallel",)),
        name="inproj",
    )(x2, g, *ws)


def _attn_kernel(q_ref, kt_ref, v_ref, ktm_ref, vm_ref, tri_ref, o_ref, qm_ref, carry_ref, acc_ref, *, heads, pad):
    i = pl.program_id(1)
    pairs = heads // 2
    carry_ref[...] = jnp.zeros_like(carry_ref)
    acc_ref[...] = jnp.zeros_like(acc_ref)
    lane = lax.broadcasted_iota(jnp.int32, (Q_BLOCK, LANES), 1)
    row = lax.broadcasted_iota(jnp.int32, (Q_BLOCK, LANES), 0)
    low_half = lane < HEAD_DIM
    rows = lambda h: slice(h * Q_BLOCK, (h + 1) * Q_BLOCK)
    for p in range(pairs):
        qp = q_ref[0, :, p * LANES:(p + 1) * LANES]
        qm_ref[rows(2 * p)] = jnp.where(low_half, qp, jnp.zeros_like(qp))
        qm_ref[rows(2 * p + 1)] = jnp.where(low_half, jnp.zeros_like(qp), qp)

    def block_step(get_kt, get_v, valid):
        kts = [get_kt(p) for p in range(pairs)]
        vs = [get_v(p) for p in range(pairs)]
        zs = [jnp.dot(qm_ref[rows(h)], kts[h // 2], preferred_element_type=F32) for h in range(heads)]
        cats = []
        for h in range(heads):
            z = zs[h]
            drop = jnp.maximum(z, 0.0) + jnp.log(1.0 + jnp.exp(-jnp.abs(z)))
            if valid is not None:
                drop = jnp.where(valid, drop, 0.0)
            hi = drop.astype(BF16)
            lo = (drop - hi.astype(F32)).astype(BF16)
            cats.append(jnp.concatenate([hi, lo], axis=1))
        sums = [jnp.dot(cats[h], tri_ref[...], preferred_element_type=F32) for h in range(heads)]
        low = None
        for h in range(heads):
            carry = carry_ref[rows(h)]
            w = jnp.exp(zs[h] - (sums[h][:, :LANES] + carry))
            if valid is not None:
                w = jnp.where(valid, w, 0.0)
            acc_ref[rows(h)] += jnp.dot(w.astype(BF16), vs[h // 2], preferred_element_type=F32)
            carry = carry + sums[h][:, LANES:]
            carry_ref[rows(h)] = carry
            low = carry if low is None else jnp.minimum(low, carry)
        return jnp.min(low)

    off_d = pl.multiple_of(i * Q_BLOCK, Q_BLOCK)
    low0 = block_step(lambda p: kt_ref[p * LANES:(p + 1) * LANES, pl.ds(off_d, Q_BLOCK)],
                      lambda p: v_ref[0, pl.ds(off_d, Q_BLOCK), p * LANES:(p + 1) * LANES],
                      lane < row)

    def cond(state):
        kb, low = state
        return jnp.logical_and(kb >= 0, low < STICK_CUTOFF)

    def body(state):
        kb, _ = state
        off = pl.multiple_of(kb * Q_BLOCK, Q_BLOCK)
        low = block_step(lambda p: kt_ref[p * LANES:(p + 1) * LANES, pl.ds(off, Q_BLOCK)],
                         lambda p: v_ref[0, pl.ds(off, Q_BLOCK), p * LANES:(p + 1) * LANES],
                         None)
        return kb - 1, low

    _, low1 = lax.while_loop(cond, body, (i - 1, low0))

    @pl.when(low1 < STICK_CUTOFF)
    def _():
        block_step(lambda p: ktm_ref[p * LANES:(p + 1) * LANES, :],
                   lambda p: vm_ref[:, p * LANES:(p + 1) * LANES],
                   lane >= pad)

    for p in range(pairs):
        even = acc_ref[2 * p * Q_BLOCK:(2 * p + 1) * Q_BLOCK]
        odd = acc_ref[(2 * p + 1) * Q_BLOCK:(2 * p + 2) * Q_BLOCK]
        o_ref[0, :, p * LANES:(p + 1) * LANES] = jnp.where(low_half, even, odd).astype(o_ref.dtype)


def _attention(q, kt, v, kt_meta, v_meta, b, heads, pad):
    t, aw = q.shape
    l = t // b
    r = lax.broadcasted_iota(jnp.int32, (2 * LANES, 2 * LANES), 0) % LANES
    c = lax.broadcasted_iota(jnp.int32, (2 * LANES, 2 * LANES), 1)
    tri = jnp.where(jnp.logical_or(c >= LANES, r >= c), 1.0, 0.0).astype(BF16)
    once = pl.Buffered(1)
    out = pl.pallas_call(
        functools.partial(_attn_kernel, heads=heads, pad=pad),
        grid=(b, l // Q_BLOCK),
        in_specs=[
            pl.BlockSpec((1, Q_BLOCK, aw), lambda bi, i: (bi, i, 0)),
            pl.BlockSpec((aw, l), lambda bi, i: (0, bi), pipeline_mode=once),
            pl.BlockSpec((1, l, aw), lambda bi, i: (bi, 0, 0), pipeline_mode=once),
            pl.BlockSpec((aw, Q_BLOCK), lambda bi, i: (0, 0)),
            pl.BlockSpec((Q_BLOCK, aw), lambda bi, i: (0, 0)),
            pl.BlockSpec((2 * LANES, 2 * LANES), lambda bi, i: (0, 0)),
        ],
        out_specs=pl.BlockSpec((1, Q_BLOCK, aw), lambda bi, i: (bi, i, 0)),
        out_shape=jax.ShapeDtypeStruct((b, l, aw), BF16),
        scratch_shapes=[pltpu.VMEM((heads * Q_BLOCK, LANES), BF16), pltpu.VMEM((heads * Q_BLOCK, LANES), F32),
                        pltpu.VMEM((heads * Q_BLOCK, LANES), F32)],
        compiler_params=_cparams(("parallel", "arbitrary")),
        name="stick_attn",
    )(q.reshape(b, l, aw), kt, v.reshape(b, l, aw), kt_meta, v_meta, tri)
    return out.reshape(t, aw)


def _ssm_tables(lam_re, lam_im, log_dt, b_re, b_im, c_re, c_im, d_skip):
    hp = lax.Precision.HIGHEST
    tc = SSM_CHUNK
    g, p = lam_re.shape
    ch = b_re.shape[-1]
    dt = jnp.exp(log_dt)[:, None]
    mag = jnp.exp(lam_re * dt)
    lb_re, lb_im = mag * jnp.cos(lam_im * dt), mag * jnp.sin(lam_im * dt)
    nr, ni = lb_re - 1.0, lb_im
    den = lam_re * lam_re + lam_im * lam_im
    f_re = (nr * lam_re + ni * lam_im) / den
    f_im = (ni * lam_re - nr * lam_im) / den
    bb_re = f_re[:, :, None] * b_re - f_im[:, :, None] * b_im
    bb_im = f_re[:, :, None] * b_im + f_im[:, :, None] * b_re
    pw_re, pw_im = jnp.ones((1, g, p), F32), jnp.zeros((1, g, p), F32)
    s_re, s_im = lb_re, lb_im
    while pw_re.shape[0] < tc + 1:
        n_re = pw_re * s_re - pw_im * s_im
        n_im = pw_re * s_im + pw_im * s_re
        pw_re, pw_im = jnp.concatenate([pw_re, n_re]), jnp.concatenate([pw_im, n_im])
        s_re, s_im = s_re * s_re - s_im * s_im, 2.0 * s_re * s_im
    dec_re = pw_re[tc].reshape(g // 2, 1, 2 * p)
    dec_im = pw_im[tc].reshape(g // 2, 1, 2 * p)
    pw_re, pw_im = pw_re[:tc + 1].transpose(1, 2, 0), pw_im[:tc + 1].transpose(1, 2, 0)
    ct_re, ct_im = c_re.transpose(0, 2, 1)[..., None], c_im.transpose(0, 2, 1)[..., None]
    cl_re = ct_re * pw_re[:, :, None, :] - ct_im * pw_im[:, :, None, :]
    cl_im = ct_re * pw_im[:, :, None, :] + ct_im * pw_re[:, :, None, :]
    kern = (jnp.einsum('gpot,gpi->giot', cl_re[..., :tc], bb_re, precision=hp)
            - jnp.einsum('gpot,gpi->giot', cl_im[..., :tc], bb_im, precision=hp))
    kern = kern.at[:, :, :, 0].add(jnp.eye(ch, dtype=F32)[None] * d_skip.reshape(g, ch, 1))
    half = (jnp.arange(g) % 2)[:, None, None]

    def place_rows(m):
        m = m.reshape(g, p, ch * tc).astype(BF16)
        z = jnp.zeros_like(m)
        return jnp.where(half == 0, jnp.concatenate([m, z], axis=1), jnp.concatenate([z, m], axis=1))

    rp_re, rp_im = pw_re[:, :, None, tc - 1::-1], pw_im[:, :, None, tc - 1::-1]
    bq_re, bq_im = bb_re[..., None], bb_im[..., None]
    ms_re = place_rows(rp_re * bq_re - rp_im * bq_im)
    ms_im = place_rows(rp_re * bq_im + rp_im * bq_re)
    mo_re = place_rows(cl_re[..., 1:])
    mo_im = place_rows(-cl_im[..., 1:])
    return kern, ms_re, ms_im, mo_re, mo_im, dec_re, dec_im


def _ssm_kernel(u_ref, um_ref, k_ref, msr_ref, msi_ref, mor_ref, moi_ref, dre_ref, dim_ref, y_ref,
                toep_ref, sre_ref, sim_ref, xre_ref, xim_ref, *, batch):
    ch, tc = SSM_GROUP_CH, SSM_CHUNK
    nch = u_ref.shape[1]
    per_b = nch // batch

    def chunk_rows(ref, gl):
        return jnp.concatenate([ref[gl * ch + ci] for ci in range(ch)], axis=1)

    u = [chunk_rows(u_ref, gl) for gl in range(2)]
    um = [jnp.concatenate([jnp.broadcast_to(um_ref[gl * ch + ci:gl * ch + ci + 1, :], (8, tc)) for ci in range(ch)], axis=1)
          for gl in range(2)]

    def to_state(lhs, ms_ref):
        return (lax.dot_general(lhs[0], ms_ref[0], _NT, preferred_element_type=F32)
                + lax.dot_general(lhs[1], ms_ref[1], _NT, preferred_element_type=F32))

    sre_ref[...] = to_state(u, msr_ref)
    sim_ref[...] = to_state(u, msi_ref)
    x0_re = to_state(um, msr_ref)[0:1]
    x0_im = to_state(um, msi_ref)[0:1]
    d_re, d_im = dre_ref[0], dim_ref[0]

    def step(c, xs):
        new = []
        for bi in range(batch):
            x_re, x_im = xs[2 * bi], xs[2 * bi + 1]
            r = bi * per_b + c
            xre_ref[pl.ds(r, 1), :] = x_re
            xim_ref[pl.ds(r, 1), :] = x_im
            new.append(d_re * x_re - d_im * x_im + sre_ref[pl.ds(r, 1), :])
            new.append(d_re * x_im + d_im * x_re + sim_ref[pl.ds(r, 1), :])
        return tuple(new)

    lax.fori_loop(0, per_b, step, (x0_re, x0_im) * batch)
    xs_re = xre_ref[...].astype(BF16)
    xs_im = xim_ref[...].astype(BF16)

    lane = lax.broadcasted_iota(jnp.int32, (tc, tc), 1)
    row = lax.broadcasted_iota(jnp.int32, (tc, tc), 0)
    causal = lane >= row
    for gl in range(2):
        def fill(ci, carry):
            for co in range(ch):
                taps = jnp.broadcast_to(k_ref[gl, ci, pl.ds(co, 1), :], (tc, tc))
                blk = pltpu.roll(taps, 0, 1, stride=1, stride_axis=0)
                blk = jnp.where(causal, blk, 0.0).astype(BF16)
                toep_ref[pl.ds(pl.multiple_of(ci * tc, tc), tc), co * tc:(co + 1) * tc] = blk
            return carry
        lax.fori_loop(0, ch, fill, 0)
        y = jnp.dot(u[gl], toep_ref[...], preferred_element_type=F32)
        y = y + jnp.dot(xs_re, mor_ref[gl], preferred_element_type=F32)
        y = y + jnp.dot(xs_im, moi_ref[gl], preferred_element_type=F32)
        y = jax.nn.gelu(y).astype(y_ref.dtype)
        for co in range(ch):
            y_ref[gl * ch + co] = y[:, co * tc:(co + 1) * tc]


def _ssm(ut, ut_meta, params, batch):
    s, t = ut.shape
    tc, ch, p = SSM_CHUNK, SSM_GROUP_CH, SSM_STATE
    g = s // ch
    nch = t // tc
    kern, ms_re, ms_im, mo_re, mo_im, dec_re, dec_im = _ssm_tables(*params)
    u3 = ut.reshape(s, nch, tc)
    pair3 = lambda a, b_: pl.BlockSpec((2, a, b_), lambda j: (j, 0, 0))
    y3 = pl.pallas_call(
        functools.partial(_ssm_kernel, batch=batch),
        grid=(g // 2,),
        in_specs=[
            pl.BlockSpec((2 * ch, nch, tc), lambda j: (j, 0, 0)),
            pl.BlockSpec((2 * ch, tc), lambda j: (j, 0)),
            pl.BlockSpec((2, ch, ch, tc), lambda j: (j, 0, 0, 0)),
            pair3(2 * p, ch * tc), pair3(2 * p, ch * tc), pair3(2 * p, ch * tc), pair3(2 * p, ch * tc),
            pl.BlockSpec((1, 1, 2 * p), lambda j: (j, 0, 0)), pl.BlockSpec((1, 1, 2 * p), lambda j: (j, 0, 0)),
        ],
        out_specs=pl.BlockSpec((2 * ch, nch, tc), lambda j: (j, 0, 0)),
        out_shape=jax.ShapeDtypeStruct((s, nch, tc), BF16),
        scratch_shapes=[pltpu.VMEM((ch * tc, ch * tc), BF16)] + [pltpu.VMEM((nch, 2 * p), F32)] * 4,
        compiler_params=_cparams(("parallel",)), name="ssm_chunked",
    )(u3, ut_meta, kern, ms_re, ms_im, mo_re, mo_im, dec_re, dec_im)
    return y3.reshape(s, t)


RANK_BITS = 16


def _pack_halves(x):
    n = x.shape[1] // 2
    as_bits = lambda v: lax.bitcast_convert_type(v.astype(BF16).astype(F32), jnp.uint32)
    return jnp.bitwise_or(lax.shift_right_logical(as_bits(x[:, :n]), jnp.uint32(16)),
                          jnp.bitwise_and(as_bits(x[:, n:]), jnp.uint32(0xFFFF0000)))


def _unpack_halves(w):
    lo = lax.bitcast_convert_type(lax.shift_left(w, jnp.uint32(16)), F32)
    hi = lax.bitcast_convert_type(jnp.bitwise_and(w, jnp.uint32(0xFFFF0000)), F32)
    return jnp.concatenate([lo, hi], axis=1)


def _merge_kernel(x_ref, a_ref, yt_ref, ga_ref, gb_ref, gwt_ref, gbias_ref, wpa_ref, wpb_ref, wo_ref, gn_ref,
                  wr_ref, br_ref, ltri_ref, h_ref, hn_ref, route_ref, count_ref, run_ref):
    @pl.when(pl.program_id(0) == 0)
    def _():
        run_ref[...] = jnp.zeros_like(run_ref)

    yt = yt_ref[...]
    gate = jax.nn.sigmoid(jnp.dot(gwt_ref[...], yt, preferred_element_type=F32) + gbias_ref[...])
    ssm = (yt.astype(F32) * gate).T.astype(BF16)
    merged = (ga_ref[...].astype(F32) * jnp.dot(a_ref[...], wpa_ref[...], preferred_element_type=F32)
              + gb_ref[...].astype(F32) * jnp.dot(ssm, wpb_ref[...], preferred_element_type=F32))
    h = x_ref[...] + jnp.dot(merged.astype(BF16), wo_ref[...], preferred_element_type=F32)
    h_ref[...] = h
    hn = h * lax.rsqrt(jnp.mean(h * h, axis=-1, keepdims=True) + RMS_EPS) * gn_ref[...]
    hn_ref[...] = _pack_halves(hn)
    hn_hi = hn.astype(BF16)
    hn_lo = (hn - hn_hi.astype(F32)).astype(BF16)
    logits = (jnp.dot(hn_hi, wr_ref[0], preferred_element_type=F32)
              + jnp.dot(hn_hi, wr_ref[1], preferred_element_type=F32)
              + jnp.dot(hn_lo, wr_ref[0], preferred_element_type=F32)) + br_ref[...]
    lane = lax.broadcasted_iota(jnp.int32, logits.shape, 1)
    neg = jnp.full_like(logits, -jnp.inf)
    big = jnp.int32(LANES)
    is_grp = jnp.logical_and(lane >= N_EXPERTS, lane < N_EXPERTS + MOE_GROUPS)
    glog = jnp.where(is_grp, logits, neg)
    gmax = jnp.max(glog, axis=-1, keepdims=True)
    g_top = jnp.min(jnp.where(glog == gmax, lane, big), axis=-1, keepdims=True) - N_EXPERTS
    p_top = 1.0 / jnp.sum(jnp.exp(glog - gmax), axis=-1, keepdims=True)
    in_grp = (lane // EXPERTS_PER_GROUP) == g_top
    elog = jnp.where(jnp.logical_and(in_grp, lane < N_EXPERTS), logits, neg)
    e1 = jnp.max(elog, axis=-1, keepdims=True)
    i1 = jnp.min(jnp.where(elog == e1, lane, big), axis=-1, keepdims=True)
    elog2 = jnp.where(lane == i1, neg, elog)
    e2 = jnp.max(elog2, axis=-1, keepdims=True)
    i2 = jnp.min(jnp.where(elog2 == e2, lane, big), axis=-1, keepdims=True)
    t = jnp.exp(e2 - e1)
    w1 = p_top / (1.0 + t)
    w2 = p_top * t / (1.0 + t)
    hit1, hit2 = lane == i1, lane == i2
    chosen = jnp.where(jnp.logical_or(hit1, hit2), 1.0, 0.0)
    before = jnp.dot(ltri_ref[...], chosen.astype(BF16), preferred_element_type=F32) + run_ref[0:1, :]
    r1 = jnp.sum(jnp.where(hit1, before, 0.0), axis=-1, keepdims=True)
    r2 = jnp.sum(jnp.where(hit2, before, 0.0), axis=-1, keepdims=True)
    run_ref[...] = run_ref[...] + jnp.sum(chosen, axis=0, keepdims=True)
    count_ref[...] = run_ref[...]
    scale = float(2 ** RANK_BITS)
    route = jnp.where(lane == 0, i1.astype(F32) * scale + r1,
                      jnp.where(lane == 1, i2.astype(F32) * scale + r2,
                                jnp.where(lane == 2, w1, jnp.where(lane == 3, w2, 0.0))))
    route_ref[...] = route


def _merge(x2, attn, yt, ga, gb, glu_wt, glu_b, wpa, wpb, wo, gn, wr, br, tm):
    m, d = x2.shape
    aw, sw = attn.shape[1], yt.shape[0]
    row = lambda n: pl.BlockSpec((tm, n), lambda i: (i, 0))
    full = lambda a: pl.BlockSpec(a.shape, lambda i: (0,) * a.ndim)
    wr_hi = wr.astype(BF16)
    wr = jnp.stack([wr_hi, (wr - wr_hi.astype(F32)).astype(BF16)])
    ltri = jnp.where(lax.broadcasted_iota(jnp.int32, (tm, tm), 0) > lax.broadcasted_iota(jnp.int32, (tm, tm), 1),
                     1.0, 0.0).astype(BF16)
    ws = (glu_wt, glu_b, wpa, wpb, wo, gn, wr, br, ltri)
    return pl.pallas_call(
        _merge_kernel, grid=(m // tm,),
        in_specs=[row(d), row(aw), pl.BlockSpec((sw, tm), lambda i: (0, i)), row(d), row(d)] + [full(a) for a in ws],
        out_specs=[row(d), row(d // 2), row(LANES), pl.BlockSpec((8, LANES), lambda i: (0, 0))],
        out_shape=[jax.ShapeDtypeStruct((m, d), F32), jax.ShapeDtypeStruct((m, d // 2), jnp.uint32),
                   jax.ShapeDtypeStruct((m, LANES), F32), jax.ShapeDtypeStruct((8, LANES), F32)],
        scratch_shapes=[pltpu.VMEM((8, LANES), F32)],
        compiler_params=_cparams(("arbitrary",)), name="merge_router",
    )(x2, attn, yt, ga, gb, *ws)


def _start_row_gather(src_hbm, row_of, base, stride, buf, row0, sem, n):
    for r in range(n):
        row = row_of(base + stride * r)
        pltpu.make_async_copy(src_hbm.at[pl.ds(row, 1)], buf.at[pl.ds(row0 + r, 1)], sem).start(priority=r % 2)


def _wait_rows(src_hbm, buf, sem):
    pltpu.make_async_copy(src_hbm.at[pl.ds(0, buf.shape[0])], buf, sem).wait()


def _dispatch_kernel(pos_ref, seg_ref, hn_ref, xs_hbm, zero_ref, sem, zsem, *, tm):
    i = pl.program_id(0)

    @pl.when(i == 0)
    def _():
        zero_ref[...] = jnp.zeros_like(zero_ref)
        zrows = zero_ref.shape[0]

        def zero_run(lo, hi, act):
            def copy(start, n):
                return pltpu.make_async_copy(zero_ref.at[pl.ds(0, n)], xs_hbm.at[pl.ds(start, n)], zsem)
            head = jnp.minimum(jnp.bitwise_and(-lo, SUBLANES - 1), hi - lo)
            lax.fori_loop(0, head, lambda k, c: (act(copy(lo + k, 1)), c)[1], 0)
            lo = lo + head
            n_full = (hi - lo) // zrows
            lax.fori_loop(0, n_full,
                          lambda k, c: (act(copy(pl.multiple_of(lo + k * zrows, SUBLANES), zrows)), c)[1], 0)
            rest = (hi - lo) - n_full * zrows
            start = lo + n_full * zrows
            bit = zrows // 2
            while bit >= SUBLANES:
                pl.when(jnp.bitwise_and(rest, bit) != 0)(
                    functools.partial(lambda st, n: act(copy(pl.multiple_of(st, SUBLANES), n)), start, bit))
                start = start + jnp.bitwise_and(rest, bit)
                bit //= 2

        def each_gap(act):
            def run(e, carry):
                zero_run(seg_ref[N_EXPERTS + e], seg_ref[2 * N_EXPERTS + e], act)
                return carry
            lax.fori_loop(0, N_EXPERTS, run, 0)
            zero_run(seg_ref[3 * N_EXPERTS - 1], xs_hbm.shape[0], act)
        each_gap(lambda cp: cp.start())
        each_gap(lambda cp: cp.wait())

    for r in range(tm):
        for c in range(2):
            slot = pos_ref[2 * (i * tm + r) + c]
            pltpu.make_async_copy(hn_ref.at[pl.ds(r, 1)], xs_hbm.at[pl.ds(slot, 1)], sem).start(priority=c)
    pltpu.make_async_copy(xs_hbm.at[pl.ds(0, 2 * tm)], xs_hbm.at[pl.ds(0, 2 * tm)], sem).wait()


def _dispatch(hn, pos, seg, cap, tm):
    t, d = hn.shape
    return pl.pallas_call(
        functools.partial(_dispatch_kernel, tm=tm),
        grid_spec=pltpu.PrefetchScalarGridSpec(
            num_scalar_prefetch=2, grid=(t // tm,),
            in_specs=[pl.BlockSpec((tm, d), lambda i, c, s: (i, 0))],
            out_specs=pl.BlockSpec(memory_space=pl.ANY),
            scratch_shapes=[pltpu.VMEM((MOE_TILE, d), hn.dtype), pltpu.SemaphoreType.DMA(()),
                            pltpu.SemaphoreType.DMA(())],
        ),
        out_shape=jax.ShapeDtypeStruct((cap, d), hn.dtype),
        compiler_params=_cparams(("arbitrary",)), name="moe_dispatch",
    )(pos, seg, hn)


def _expert_kernel(blk_e_ref, n_used_ref, x_ref, w1_ref, w3_ref, w2_ref, y_ref):
    j = pl.program_id(0)

    @pl.when(j < n_used_ref[0])
    def _():
        x = _unpack_halves(x_ref[...]).astype(BF16)
        a = jnp.dot(x, w1_ref[0].astype(BF16), preferred_element_type=F32)
        b = jnp.dot(x, w3_ref[0].astype(BF16), preferred_element_type=F32)
        hdn = (a * jax.nn.sigmoid(a) * b).astype(BF16)
        y_ref[...] = _pack_halves(jnp.dot(hdn, w2_ref[0].astype(BF16), preferred_element_type=F32))

    @pl.when(j >= n_used_ref[0])
    def _():
        y_ref[...] = jnp.zeros_like(y_ref)


def _experts(xs, blk_e, n_used, w1, w3, w2, tm):
    cap, dp = xs.shape
    e, d, ff = w1.shape
    nb = cap // tm
    wspec = lambda r, c: pl.BlockSpec((1, r, c), lambda j, be, nu: (be[j], 0, 0))
    xrow = lambda j, be, nu: (jnp.minimum(j, nu[0] - 1), 0)
    return pl.pallas_call(
        _expert_kernel,
        grid_spec=pltpu.PrefetchScalarGridSpec(
            num_scalar_prefetch=2, grid=(nb,),
            in_specs=[pl.BlockSpec((tm, dp), xrow), wspec(d, ff), wspec(d, ff), wspec(ff, d)],
            out_specs=pl.BlockSpec((tm, dp), lambda j, be, nu: (j, 0)),
        ),
        out_shape=jax.ShapeDtypeStruct((cap, dp), xs.dtype),
        compiler_params=_cparams(("arbitrary",)), name="expert_ffn",
    )(blk_e, n_used, xs, w1, w3, w2)


def _combine_kernel(pos_ref, h_ref, route_ref, gf_ref, yb_hbm, o_ref, ybuf, sems, *, tm):
    i = pl.program_id(0)
    last = pl.num_programs(0) - 1
    slot = lax.rem(i, 2)

    def start(tile, s):
        for c in range(2):
            _start_row_gather(yb_hbm, lambda k: pos_ref[k], tile * 2 * tm + c, 2, ybuf.at[s], c * tm, sems.at[s], tm)

    @pl.when(i == 0)
    def _():
        start(0, 0)

    _wait_rows(yb_hbm, ybuf.at[slot], sems.at[slot])
    nxt = jnp.minimum(i + 1, last)
    for s in range(2):
        @pl.when(slot == s)
        def _():
            start(nxt, 1 - s)
            route = route_ref[...]
            w1 = route[:, 2:3]
            w2 = route[:, 3:4]
            h = h_ref[...] + w1 * _unpack_halves(ybuf[s, :tm, :]) + w2 * _unpack_halves(ybuf[s, tm:, :])
            o_ref[...] = h * lax.rsqrt(jnp.mean(h * h, axis=-1, keepdims=True) + RMS_EPS) * gf_ref[...]

    @pl.when(i == last)
    def _():
        _wait_rows(yb_hbm, ybuf.at[1 - slot], sems.at[1 - slot])


def _combine(pos, h, route, gf, yb, tm):
    t, d = h.shape
    return pl.pallas_call(
        functools.partial(_combine_kernel, tm=tm),
        grid_spec=pltpu.PrefetchScalarGridSpec(
            num_scalar_prefetch=1, grid=(t // tm,),
            in_specs=[pl.BlockSpec((tm, d), lambda i, p: (i, 0)), pl.BlockSpec((tm, LANES), lambda i, p: (i, 0)),
                      pl.BlockSpec((1, d), lambda i, p: (0, 0)), pl.BlockSpec(memory_space=pl.ANY)],
            out_specs=pl.BlockSpec((tm, d), lambda i, p: (i, 0)),
            scratch_shapes=[pltpu.VMEM((2, 2 * tm) + yb.shape[1:], yb.dtype), pltpu.SemaphoreType.DMA((2,))],
        ),
        out_shape=jax.ShapeDtypeStruct((t, d), F32),
        compiler_params=_cparams(("arbitrary",)), name="moe_combine",
    )(pos, h, route, gf, yb)


def _position_kernel(seg_ref, code_ref, pos_ref):
    code = code_ref[...]
    expert = lax.shift_right_logical(code, RANK_BITS)
    pos = jnp.bitwise_and(code, (1 << RANK_BITS) - 1)
    for e in range(N_EXPERTS):
        pos = pos + jnp.where(expert == e, seg_ref[e], 0)
    pos_ref[...] = pos


def _dispatch_plan(route, counts, tm):
    t = route.shape[0]
    n = 2 * t
    nb = (n + N_EXPERTS * (tm - 1)) // tm + 1
    cnt = counts[0, :N_EXPERTS].astype(jnp.int32)
    padded = ((cnt + tm - 1) // tm) * tm
    pend = jnp.cumsum(padded)
    pstart = pend - padded
    seg = jnp.concatenate([pstart, pstart + cnt, pend]).astype(jnp.int32)
    blk_start = jnp.arange(nb, dtype=jnp.int32) * tm
    blk_e = jnp.minimum(jnp.sum((blk_start[:, None] >= pend[None, :]).astype(jnp.int32), axis=1), N_EXPERTS - 1)
    n_used = (pend[-1:] // tm).astype(jnp.int32)
    code = route[:, :2].astype(jnp.int32).reshape(n // LANES, LANES)
    whole = pl.BlockSpec(code.shape, lambda i, s: (0, 0))
    pos = pl.pallas_call(
        _position_kernel,
        grid_spec=pltpu.PrefetchScalarGridSpec(num_scalar_prefetch=1, grid=(1,), in_specs=[whole], out_specs=whole),
        out_shape=jax.ShapeDtypeStruct(code.shape, jnp.int32),
        compiler_params=_cparams(("arbitrary",)), name="slot_positions",
    )(seg, code)
    return nb * tm, blk_e.astype(jnp.int32), n_used, pos.reshape(n), seg


def kernel(x, meta_tokens, norm_mix_g, w_in, ssm_lambda_re, ssm_lambda_im, ssm_log_dt, ssm_b_re, ssm_b_im,
           ssm_c_re, ssm_c_im, ssm_d, ssm_glu_w, ssm_glu_b, w_branch_attn, w_branch_ssm, w_out, norm_ffn_g,
           router_group_w, router_group_b, router_expert_w, router_expert_b, expert_w1, expert_w3, expert_w2,
           norm_final_g):
    b, l, d = x.shape
    depth = w_in.shape[0]
    aw = w_branch_attn.shape[1]
    sw = w_branch_ssm.shape[1]
    heads = aw // HEAD_DIM
    t = b * l
    pad = (-N_META) % Q_BLOCK
    assert depth == 1 and l % Q_BLOCK == 0 and pad + N_META == Q_BLOCK and SSM_CHUNK == Q_BLOCK
    layer = 0
    h_real = x.reshape(t, d)
    h_meta = meta_tokens.astype(x.dtype)
    tm_tok = 512

    wi = w_in[layer].astype(BF16)
    o = 3 * aw + sw
    ws = (wi[:, :aw], wi[:, aw:2 * aw].T, wi[:, 2 * aw:3 * aw], wi[:, 3 * aw:o].T, wi[:, o:o + d], wi[:, o + d:])
    g_mix = norm_mix_g[layer].reshape(1, d)
    q, kt, v, ut, ga, gb = _inproj(h_real, g_mix, ws, tm_tok)
    _, kt_m, v_m, ut_m, _, _ = _inproj(h_meta, g_mix, ws, N_META)
    kt_meta = jnp.concatenate([jnp.zeros((aw, pad), BF16), kt_m], axis=1)
    v_meta = jnp.concatenate([jnp.zeros((pad, aw), BF16), v_m], axis=0)
    ut_meta = jnp.concatenate([jnp.zeros((sw, pad), BF16), ut_m], axis=1)

    attn = _attention(q, kt, v, kt_meta, v_meta, b, heads, pad)

    ssm_params = (ssm_lambda_re[layer], ssm_lambda_im[layer], ssm_log_dt[layer], ssm_b_re[layer], ssm_b_im[layer],
                  ssm_c_re[layer], ssm_c_im[layer], ssm_d[layer])
    yt = _ssm(ut, ut_meta, ssm_params, b)

    wr = jnp.zeros((d, LANES), F32)
    wr = wr.at[:, :N_EXPERTS].set(router_expert_w[layer]).at[:, N_EXPERTS:N_EXPERTS + MOE_GROUPS].set(router_group_w[layer])
    br = jnp.zeros((1, LANES), F32)
    br = br.at[0, :N_EXPERTS].set(router_expert_b[layer]).at[0, N_EXPERTS:N_EXPERTS + MOE_GROUPS].set(router_group_b[layer])
    h_mid, hn, route, counts = _merge(
        h_real, attn, yt, ga, gb, ssm_glu_w[layer].T.astype(BF16), ssm_glu_b[layer].reshape(sw, 1).astype(F32),
        w_branch_attn[layer].astype(BF16), w_branch_ssm[layer].astype(BF16), w_out[layer].astype(BF16),
        norm_ffn_g[layer].reshape(1, d), wr, br, tm_tok)

    cap, blk_e, n_used, pos, seg = _dispatch_plan(route, counts, MOE_TILE)
    xs = _dispatch(hn, pos, seg, cap, 256)
    yb = _experts(xs, blk_e, n_used, expert_w1[layer], expert_w3[layer], expert_w2[layer], MOE_TILE)
    out = _combine(pos, h_mid, route, norm_final_g.reshape(1, d), yb, 256)
    return out.reshape(b, l, d)
```

```python
import functools

import jax
import jax.numpy as jnp
from jax import lax
from jax.experimental import pallas as pl
from jax.experimental.pallas import tpu as pltpu

F32 = jnp.float32
BF16 = jnp.bfloat16

N_META = 16
Q_BLOCK = 128
HEAD_DIM = 64
SSM_GROUP_CH = 16
SSM_STATE = 64
MOE_GROUPS = 4
EXPERTS_PER_GROUP = 8
N_EXPERTS = MOE_GROUPS * EXPERTS_PER_GROUP
RMS_EPS = 1e-6

LANES = 128
SUBLANES = 8
SSM_CHUNK = LANES
MOE_TILE = 512
STICK_CUTOFF = 104.0
VMEM_LIMIT = 52 * 1024 * 1024

_NT = (((1,), (1,)), ((), ()))
_TN = (((0,), (0,)), ((), ()))


def _cparams(sem):
    return pltpu.CompilerParams(dimension_semantics=sem, vmem_limit_bytes=VMEM_LIMIT)


def _inproj_kernel(x_ref, g_ref, wq_ref, wkt_ref, wv_ref, wut_ref, wga_ref, wgb_ref,
                   q_ref, kt_ref, v_ref, ut_ref, ga_ref, gb_ref):
    x = x_ref[...]
    ms = jnp.mean(x * x, axis=-1, keepdims=True)
    xn = (x * lax.rsqrt(ms + RMS_EPS) * g_ref[...]).astype(BF16)
    q_ref[...] = (jnp.dot(xn, wq_ref[...], preferred_element_type=F32) * (HEAD_DIM ** -0.5)).astype(BF16)
    kt_ref[...] = lax.dot_general(wkt_ref[...], xn, _NT, preferred_element_type=F32).astype(BF16)
    v_ref[...] = jnp.dot(xn, wv_ref[...], preferred_element_type=F32).astype(BF16)
    ut_ref[...] = lax.dot_general(wut_ref[...], xn, _NT, preferred_element_type=F32).astype(BF16)
    ga_ref[...] = jax.nn.sigmoid(jnp.dot(xn, wga_ref[...], preferred_element_type=F32)).astype(BF16)
    gb_ref[...] = jax.nn.sigmoid(jnp.dot(xn, wgb_ref[...], preferred_element_type=F32)).astype(BF16)


def _inproj(x2, g, ws, tm):
    m, d = x2.shape
    aw, sw = ws[0].shape[1], ws[3].shape[0]
    row = lambda n: pl.BlockSpec((tm, n), lambda i: (i, 0))
    col = lambda n: pl.BlockSpec((n, tm), lambda i: (0, i))
    full = lambda a: pl.BlockSpec(a.shape, lambda i: (0, 0))
    sds = jax.ShapeDtypeStruct
    return pl.pallas_call(
        _inproj_kernel,
        grid=(m // tm,),
        in_specs=[row(d), full(g)] + [full(w) for w in ws],
        out_specs=[row(aw), col(aw), row(aw), col(sw), row(d), row(d)],
        out_shape=[sds((m, aw), BF16), sds((aw, m), BF16), sds((m, aw), BF16), sds((sw, m), BF16),
                   sds((m, d), BF16), sds((m, d), BF16)],
        compiler_params=_cparams(("parallel",)),
        name="inproj",
    )(x2, g, *ws)


def _attn_kernel(q_ref, kt_ref, v_ref, ktm_ref, vm_ref, tri_ref, o_ref, qm_ref, carry_ref, acc_ref, *, heads, pad):
    i = pl.program_id(1)
    pairs = heads // 2
    carry_ref[...] = jnp.zeros_like(carry_ref)
    acc_ref[...] = jnp.zeros_like(acc_ref)
    lane = lax.broadcasted_iota(jnp.int32, (Q_BLOCK, LANES), 1)
    row = lax.broadcasted_iota(jnp.int32, (Q_BLOCK, LANES), 0)
    low_half = lane < HEAD_DIM
    rows = lambda h: slice(h * Q_BLOCK, (h + 1) * Q_BLOCK)
    for p in range(pairs):
        qp = q_ref[0, :, p * LANES:(p + 1) * LANES]
        qm_ref[rows(2 * p)] = jnp.where(low_half, qp, jnp.zeros_like(qp))
        qm_ref[rows(2 * p + 1)] = jnp.where(low_half, jnp.zeros_like(qp), qp)

    def block_step(get_kt, get_v, valid):
        kts = [get_kt(p) for p in range(pairs)]
        vs = [get_v(p) for p in range(pairs)]
        zs = [jnp.dot(qm_ref[rows(h)], kts[h // 2], preferred_element_type=F32) for h in range(heads)]
        cats = []
        for h in range(heads):
            z = zs[h]
            drop = jnp.maximum(z, 0.0) + jnp.log(1.0 + jnp.exp(-jnp.abs(z)))
            if valid is not None:
                drop = jnp.where(valid, drop, 0.0)
            hi = drop.astype(BF16)
            lo = (drop - hi.astype(F32)).astype(BF16)
            cats.append(jnp.concatenate([hi, lo], axis=1))
        sums = [jnp.dot(cats[h], tri_ref[...], preferred_element_type=F32) for h in range(heads)]
        low = None
        for h in range(heads):
            carry = carry_ref[rows(h)]
            w = jnp.exp(zs[h] - (sums[h][:, :LANES] + carry))
            if valid is not None:
                w = jnp.where(valid, w, 0.0)
            acc_ref[rows(h)] += jnp.dot(w.astype(BF16), vs[h // 2], preferred_element_type=F32)
            carry = carry + sums[h][:, LANES:]
            carry_ref[rows(h)] = carry
            low = carry if low is None else jnp.minimum(low, carry)
        return jnp.min(low)

    off_d = pl.multiple_of(i * Q_BLOCK, Q_BLOCK)
    low0 = block_step(lambda p: kt_ref[p * LANES:(p + 1) * LANES, pl.ds(off_d, Q_BLOCK)],
                      lambda p: v_ref[0, pl.ds(off_d, Q_BLOCK), p * LANES:(p + 1) * LANES],
                      lane < row)

    def cond(state):
        kb, low = state
        return jnp.logical_and(kb >= 0, low < STICK_CUTOFF)

    def body(state):
        kb, _ = state
        off = pl.multiple_of(kb * Q_BLOCK, Q_BLOCK)
        low = block_step(lambda p: kt_ref[p * LANES:(p + 1) * LANES, pl.ds(off, Q_BLOCK)],
                         lambda p: v_ref[0, pl.ds(off, Q_BLOCK), p * LANES:(p + 1) * LANES],
                         None)
        return kb - 1, low

    _, low1 = lax.while_loop(cond, body, (i - 1, low0))

    @pl.when(low1 < STICK_CUTOFF)
    def _():
        block_step(lambda p: ktm_ref[p * LANES:(p + 1) * LANES, :],
                   lambda p: vm_ref[:, p * LANES:(p + 1) * LANES],
                   lane >= pad)

    for p in range(pairs):
        even = acc_ref[2 * p * Q_BLOCK:(2 * p + 1) * Q_BLOCK]
        odd = acc_ref[(2 * p + 1) * Q_BLOCK:(2 * p + 2) * Q_BLOCK]
        o_ref[0, :, p * LANES:(p + 1) * LANES] = jnp.where(low_half, even, odd).astype(o_ref.dtype)


def _attention(q, kt, v, kt_meta, v_meta, b, heads, pad):
    t, aw = q.shape
    l = t // b
    r = lax.broadcasted_iota(jnp.int32, (2 * LANES, 2 * LANES), 0) % LANES
    c = lax.broadcasted_iota(jnp.int32, (2 * LANES, 2 * LANES), 1)
    tri = jnp.where(jnp.logical_or(c >= LANES, r >= c), 1.0, 0.0).astype(BF16)
    once = pl.Buffered(1)
    out = pl.pallas_call(
        functools.partial(_attn_kernel, heads=heads, pad=pad),
        grid=(b, l // Q_BLOCK),
        in_specs=[
            pl.BlockSpec((1, Q_BLOCK, aw), lambda bi, i: (bi, i, 0)),
            pl.BlockSpec((aw, l), lambda bi, i: (0, bi), pipeline_mode=once),
            pl.BlockSpec((1, l, aw), lambda bi, i: (bi, 0, 0), pipeline_mode=once),
            pl.BlockSpec((aw, Q_BLOCK), lambda bi, i: (0, 0)),
            pl.BlockSpec((Q_BLOCK, aw), lambda bi, i: (0, 0)),
            pl.BlockSpec((2 * LANES, 2 * LANES), lambda bi, i: (0, 0)),
        ],
        out_specs=pl.BlockSpec((1, Q_BLOCK, aw), lambda bi, i: (bi, i, 0)),
        out_shape=jax.ShapeDtypeStruct((b, l, aw), BF16),
        scratch_shapes=[pltpu.VMEM((heads * Q_BLOCK, LANES), BF16), pltpu.VMEM((heads * Q_BLOCK, LANES), F32),
                        pltpu.VMEM((heads * Q_BLOCK, LANES), F32)],
        compiler_params=_cparams(("parallel", "arbitrary")),
        name="stick_attn",
    )(q.reshape(b, l, aw), kt, v.reshape(b, l, aw), kt_meta, v_meta, tri)
    return out.reshape(t, aw)


def _ssm_tables(lam_re, lam_im, log_dt, b_re, b_im, c_re, c_im, d_skip):
    hp = lax.Precision.HIGHEST
    tc = SSM_CHUNK
    g, p = lam_re.shape
    ch = b_re.shape[-1]
    dt = jnp.exp(log_dt)[:, None]
    mag = jnp.exp(lam_re * dt)
    lb_re, lb_im = mag * jnp.cos(lam_im * dt), mag * jnp.sin(lam_im * dt)
    nr, ni = lb_re - 1.0, lb_im
    den = lam_re * lam_re + lam_im * lam_im
    f_re = (nr * lam_re + ni * lam_im) / den
    f_im = (ni * lam_re - nr * lam_im) / den
    bb_re = f_re[:, :, None] * b_re - f_im[:, :, None] * b_im
    bb_im = f_re[:, :, None] * b_im + f_im[:, :, None] * b_re
    pw_re, pw_im = jnp.ones((1, g, p), F32), jnp.zeros((1, g, p), F32)
    s_re, s_im = lb_re, lb_im
    while pw_re.shape[0] < tc + 1:
        n_re = pw_re * s_re - pw_im * s_im
        n_im = pw_re * s_im + pw_im * s_re
        pw_re, pw_im = jnp.concatenate([pw_re, n_re]), jnp.concatenate([pw_im, n_im])
        s_re, s_im = s_re * s_re - s_im * s_im, 2.0 * s_re * s_im
    dec_re = pw_re[tc].reshape(g // 2, 1, 2 * p)
    dec_im = pw_im[tc].reshape(g // 2, 1, 2 * p)
    pw_re, pw_im = pw_re[:tc + 1].transpose(1, 2, 0), pw_im[:tc + 1].transpose(1, 2, 0)
    ct_re, ct_im = c_re.transpose(0, 2, 1)[..., None], c_im.transpose(0, 2, 1)[..., None]
    cl_re = ct_re * pw_re[:, :, None, :] - ct_im * pw_im[:, :, None, :]
    cl_im = ct_re * pw_im[:, :, None, :] + ct_im * pw_re[:, :, None, :]
    kern = (jnp.einsum('gpot,gpi->giot', cl_re[..., :tc], bb_re, precision=hp)
            - jnp.einsum('gpot,gpi->giot', cl_im[..., :tc], bb_im, precision=hp))
    kern = kern.at[:, :, :, 0].add(jnp.eye(ch, dtype=F32)[None] * d_skip.reshape(g, ch, 1))
    powers = jnp.stack([pw_re[:, :, tc - 1::-1], pw_im[:, :, tc - 1::-1], pw_re[:, :, 1:], pw_im[:, :, 1:]], axis=1)
    coeffs = jnp.stack([bb_re, bb_im, ct_re[..., 0], ct_im[..., 0]], axis=1)
    return kern, powers, coeffs, dec_re, dec_im


def _ssm_kernel(u_ref, um_ref, k_ref, pw_ref, cf_ref, dre_ref, dim_ref, y_ref,
                toep_ref, sre_ref, sim_ref, xre_ref, xim_ref, *, batch):
    ch, tc = SSM_GROUP_CH, SSM_CHUNK
    nch = u_ref.shape[1]
    per_b = nch // batch
    p = pw_ref.shape[2]

    def chunk_rows(ref, gl):
        return jnp.concatenate([ref[gl * ch + ci] for ci in range(ch)], axis=1)

    u = [chunk_rows(u_ref, gl) for gl in range(2)]
    um = [jnp.concatenate([jnp.broadcast_to(um_ref[gl * ch + ci:gl * ch + ci + 1, :], (8, tc)) for ci in range(ch)], axis=1)
          for gl in range(2)]

    def expand(gl, which):
        a_re, a_im = pw_ref[gl, 2 * which], pw_ref[gl, 2 * which + 1]
        c_re, c_im = cf_ref[gl, 2 * which], cf_ref[gl, 2 * which + 1]
        m_re, m_im = [], []
        for c in range(ch):
            b_re = jnp.broadcast_to(c_re[:, c:c + 1], (p, tc))
            b_im = jnp.broadcast_to(c_im[:, c:c + 1], (p, tc))
            m_re.append(a_re * b_re - a_im * b_im)
            m_im.append(a_re * b_im + a_im * b_re)
        zeros = jnp.zeros((p, ch * tc), BF16)

        def place(parts):
            m = jnp.concatenate(parts, axis=1).astype(BF16)
            return jnp.concatenate([m, zeros] if gl == 0 else [zeros, m], axis=0)
        return place(m_re), place(m_im)

    to_st = [expand(gl, 0) for gl in range(2)]

    def to_state(lhs, part):
        return (lax.dot_general(lhs[0], to_st[0][part], _NT, preferred_element_type=F32)
                + lax.dot_general(lhs[1], to_st[1][part], _NT, preferred_element_type=F32))

    sre_ref[...] = to_state(u, 0)
    sim_ref[...] = to_state(u, 1)
    x0_re = to_state(um, 0)[0:1]
    x0_im = to_state(um, 1)[0:1]
    d_re, d_im = dre_ref[0], dim_ref[0]

    def step(c, xs):
        new = []
        for bi in range(batch):
            x_re, x_im = xs[2 * bi], xs[2 * bi + 1]
            r = bi * per_b + c
            xre_ref[pl.ds(r, 1), :] = x_re
            xim_ref[pl.ds(r, 1), :] = x_im
            new.append(d_re * x_re - d_im * x_im + sre_ref[pl.ds(r, 1), :])
            new.append(d_re * x_im + d_im * x_re + sim_ref[pl.ds(r, 1), :])
        return tuple(new)

    lax.fori_loop(0, per_b, step, (x0_re, x0_im) * batch)
    xs_re = xre_ref[...].astype(BF16)
    xs_im = xim_ref[...].astype(BF16)

    lane = lax.broadcasted_iota(jnp.int32, (tc, tc), 1)
    row = lax.broadcasted_iota(jnp.int32, (tc, tc), 0)
    causal = lane >= row
    for gl in range(2):
        def fill(ci, carry):
            for co in range(ch):
                taps = jnp.broadcast_to(k_ref[gl, ci, pl.ds(co, 1), :], (tc, tc))
                blk = pltpu.roll(taps, 0, 1, stride=1, stride_axis=0)
                blk = jnp.where(causal, blk, 0.0).astype(BF16)
                toep_ref[pl.ds(pl.multiple_of(ci * tc, tc), tc), co * tc:(co + 1) * tc] = blk
            return carry
        lax.fori_loop(0, ch, fill, 0)
        from_re, from_im = expand(gl, 1)
        y = jnp.dot(u[gl], toep_ref[...], preferred_element_type=F32)
        y = y + jnp.dot(xs_re, from_re, preferred_element_type=F32)
        y = y - jnp.dot(xs_im, from_im, preferred_element_type=F32)
        y = jax.nn.gelu(y).astype(y_ref.dtype)
        for co in range(ch):
            y_ref[gl * ch + co] = y[:, co * tc:(co + 1) * tc]


def _ssm(ut, ut_meta, params, batch):
    s, t = ut.shape
    tc, ch, p = SSM_CHUNK, SSM_GROUP_CH, SSM_STATE
    g = s // ch
    nch = t // tc
    kern, powers, coeffs, dec_re, dec_im = _ssm_tables(*params)
    u3 = ut.reshape(s, nch, tc)
    y3 = pl.pallas_call(
        functools.partial(_ssm_kernel, batch=batch),
        grid=(g // 2,),
        in_specs=[
            pl.BlockSpec((2 * ch, nch, tc), lambda j: (j, 0, 0)),
            pl.BlockSpec((2 * ch, tc), lambda j: (j, 0)),
            pl.BlockSpec((2, ch, ch, tc), lambda j: (j, 0, 0, 0)),
            pl.BlockSpec((2, 4, p, tc), lambda j: (j, 0, 0, 0)), pl.BlockSpec((2, 4, p, ch), lambda j: (j, 0, 0, 0)),
            pl.BlockSpec((1, 1, 2 * p), lambda j: (j, 0, 0)), pl.BlockSpec((1, 1, 2 * p), lambda j: (j, 0, 0)),
        ],
        out_specs=pl.BlockSpec((2 * ch, nch, tc), lambda j: (j, 0, 0)),
        out_shape=jax.ShapeDtypeStruct((s, nch, tc), BF16),
        scratch_shapes=[pltpu.VMEM((ch * tc, ch * tc), BF16)] + [pltpu.VMEM((nch, 2 * p), F32)] * 4,
        compiler_params=_cparams(("parallel",)), name="ssm_chunked",
    )(u3, ut_meta, kern, powers, coeffs, dec_re, dec_im)
    return y3.reshape(s, t)


RANK_BITS = 16


def _pack_halves(x):
    n = x.shape[1] // 2
    as_bits = lambda v: lax.bitcast_convert_type(v.astype(BF16).astype(F32), jnp.uint32)
    return jnp.bitwise_or(lax.shift_right_logical(as_bits(x[:, :n]), jnp.uint32(16)),
                          jnp.bitwise_and(as_bits(x[:, n:]), jnp.uint32(0xFFFF0000)))


def _unpack_halves(w):
    lo = lax.bitcast_convert_type(lax.shift_left(w, jnp.uint32(16)), F32)
    hi = lax.bitcast_convert_type(jnp.bitwise_and(w, jnp.uint32(0xFFFF0000)), F32)
    return jnp.concatenate([lo, hi], axis=1)


def _merge_kernel(x_ref, a_ref, yt_ref, ga_ref, gb_ref, gwt_ref, gbias_ref, wpa_ref, wpb_ref, wo_ref, gn_ref,
                  wr_ref, br_ref, ltri_ref, h_ref, hn_ref, route_ref, count_ref, run_ref):
    @pl.when(pl.program_id(0) == 0)
    def _():
        run_ref[...] = jnp.zeros_like(run_ref)

    yt = yt_ref[...]
    gate = jax.nn.sigmoid(jnp.dot(gwt_ref[...], yt, preferred_element_type=F32) + gbias_ref[...])
    ssm = (yt.astype(F32) * gate).T.astype(BF16)
    merged = (ga_ref[...].astype(F32) * jnp.dot(a_ref[...], wpa_ref[...], preferred_element_type=F32)
              + gb_ref[...].astype(F32) * jnp.dot(ssm, wpb_ref[...], preferred_element_type=F32))
    h = x_ref[...] + jnp.dot(merged.astype(BF16), wo_ref[...], preferred_element_type=F32)
    h_ref[...] = h
    hn = h * lax.rsqrt(jnp.mean(h * h, axis=-1, keepdims=True) + RMS_EPS) * gn_ref[...]
    hn_ref[...] = _pack_halves(hn)
    hn_hi = hn.astype(BF16)
    hn_lo = (hn - hn_hi.astype(F32)).astype(BF16)
    logits = (jnp.dot(hn_hi, wr_ref[0], preferred_element_type=F32)
              + jnp.dot(hn_hi, wr_ref[1], preferred_element_type=F32)
              + jnp.dot(hn_lo, wr_ref[0], preferred_element_type=F32)) + br_ref[...]
    lane = lax.broadcasted_iota(jnp.int32, logits.shape, 1)
    neg = jnp.full_like(logits, -jnp.inf)
    big = jnp.int32(LANES)
    is_grp = jnp.logical_and(lane >= N_EXPERTS, lane < N_EXPERTS + MOE_GROUPS)
    glog = jnp.where(is_grp, logits, neg)
    gmax = jnp.max(glog, axis=-1, keepdims=True)
    g_top = jnp.min(jnp.where(glog == gmax, lane, big), axis=-1, keepdims=True) - N_EXPERTS
    p_top = 1.0 / jnp.sum(jnp.exp(glog - gmax), axis=-1, keepdims=True)
    in_grp = (lane // EXPERTS_PER_GROUP) == g_top
    elog = jnp.where(jnp.logical_and(in_grp, lane < N_EXPERTS), logits, neg)
    e1 = jnp.max(elog, axis=-1, keepdims=True)
    i1 = jnp.min(jnp.where(elog == e1, lane, big), axis=-1, keepdims=True)
    elog2 = jnp.where(lane == i1, neg, elog)
    e2 = jnp.max(elog2, axis=-1, keepdims=True)
    i2 = jnp.min(jnp.where(elog2 == e2, lane, big), axis=-1, keepdims=True)
    t = jnp.exp(e2 - e1)
    w1 = p_top / (1.0 + t)
    w2 = p_top * t / (1.0 + t)
    hit1, hit2 = lane == i1, lane == i2
    chosen = jnp.where(jnp.logical_or(hit1, hit2), 1.0, 0.0)
    before = jnp.dot(ltri_ref[...], chosen.astype(BF16), preferred_element_type=F32) + run_ref[0:1, :]
    r1 = jnp.sum(jnp.where(hit1, before, 0.0), axis=-1, keepdims=True)
    r2 = jnp.sum(jnp.where(hit2, before, 0.0), axis=-1, keepdims=True)
    run_ref[...] = run_ref[...] + jnp.sum(chosen, axis=0, keepdims=True)
    count_ref[...] = run_ref[...]
    scale = float(2 ** RANK_BITS)
    route = jnp.where(lane == 0, i1.astype(F32) * scale + r1,
                      jnp.where(lane == 1, i2.astype(F32) * scale + r2,
                                jnp.where(lane == 2, w1, jnp.where(lane == 3, w2, 0.0))))
    route_ref[...] = route


def _merge(x2, attn, yt, ga, gb, glu_wt, glu_b, wpa, wpb, wo, gn, wr, br, tm):
    m, d = x2.shape
    aw, sw = attn.shape[1], yt.shape[0]
    row = lambda n: pl.BlockSpec((tm, n), lambda i: (i, 0))
    full = lambda a: pl.BlockSpec(a.shape, lambda i: (0,) * a.ndim)
    wr_hi = wr.astype(BF16)
    wr = jnp.stack([wr_hi, (wr - wr_hi.astype(F32)).astype(BF16)])
    ltri = jnp.where(lax.broadcasted_iota(jnp.int32, (tm, tm), 0) > lax.broadcasted_iota(jnp.int32, (tm, tm), 1),
                     1.0, 0.0).astype(BF16)
    ws = (glu_wt, glu_b, wpa, wpb, wo, gn, wr, br, ltri)
    return pl.pallas_call(
        _merge_kernel, grid=(m // tm,),
        in_specs=[row(d), row(aw), pl.BlockSpec((sw, tm), lambda i: (0, i)), row(d), row(d)] + [full(a) for a in ws],
        out_specs=[row(d), row(d // 2), row(LANES), pl.BlockSpec((8, LANES), lambda i: (0, 0))],
        out_shape=[jax.ShapeDtypeStruct((m, d), F32), jax.ShapeDtypeStruct((m, d // 2), jnp.uint32),
                   jax.ShapeDtypeStruct((m, LANES), F32), jax.ShapeDtypeStruct((8, LANES), F32)],
        scratch_shapes=[pltpu.VMEM((8, LANES), F32)],
        compiler_params=_cparams(("arbitrary",)), name="merge_router",
    )(x2, attn, yt, ga, gb, *ws)


def _start_row_gather(src_hbm, row_of, base, stride, buf, row0, sem, n):
    for r in range(n):
        row = row_of(base + stride * r)
        pltpu.make_async_copy(src_hbm.at[pl.ds(row, 1)], buf.at[pl.ds(row0 + r, 1)], sem).start(priority=r % 2)


def _wait_rows(src_hbm, buf, sem):
    pltpu.make_async_copy(src_hbm.at[pl.ds(0, buf.shape[0])], buf, sem).wait()


def _dispatch_kernel(pos_ref, seg_ref, hn_ref, xs_hbm, zero_ref, sem, zsem, *, tm):
    i = pl.program_id(0)

    @pl.when(i == 0)
    def _():
        zero_ref[...] = jnp.zeros_like(zero_ref)
        zrows = zero_ref.shape[0]

        def zero_run(lo, hi, act):
            def copy(start, n):
                return pltpu.make_async_copy(zero_ref.at[pl.ds(0, n)], xs_hbm.at[pl.ds(start, n)], zsem)
            head = jnp.minimum(jnp.bitwise_and(-lo, SUBLANES - 1), hi - lo)
            lax.fori_loop(0, head, lambda k, c: (act(copy(lo + k, 1)), c)[1], 0)
            lo = lo + head
            n_full = (hi - lo) // zrows
            lax.fori_loop(0, n_full,
                          lambda k, c: (act(copy(pl.multiple_of(lo + k * zrows, SUBLANES), zrows)), c)[1], 0)
            rest = (hi - lo) - n_full * zrows
            start = lo + n_full * zrows
            bit = zrows // 2
            while bit >= SUBLANES:
                pl.when(jnp.bitwise_and(rest, bit) != 0)(
                    functools.partial(lambda st, n: act(copy(pl.multiple_of(st, SUBLANES), n)), start, bit))
                start = start + jnp.bitwise_and(rest, bit)
                bit //= 2

        def each_gap(act):
            def run(e, carry):
                zero_run(seg_ref[N_EXPERTS + e], seg_ref[2 * N_EXPERTS + e], act)
                return carry
            lax.fori_loop(0, N_EXPERTS, run, 0)
            zero_run(seg_ref[3 * N_EXPERTS - 1], xs_hbm.shape[0], act)
        each_gap(lambda cp: cp.start())
        each_gap(lambda cp: cp.wait())

    for r in range(tm):
        for c in range(2):
            slot = pos_ref[2 * (i * tm + r) + c]
            pltpu.make_async_copy(hn_ref.at[pl.ds(r, 1)], xs_hbm.at[pl.ds(slot, 1)], sem).start(priority=c)
    pltpu.make_async_copy(xs_hbm.at[pl.ds(0, 2 * tm)], xs_hbm.at[pl.ds(0, 2 * tm)], sem).wait()


def _dispatch(hn, pos, seg, cap, tm):
    t, d = hn.shape
    return pl.pallas_call(
        functools.partial(_dispatch_kernel, tm=tm),
        grid_spec=pltpu.PrefetchScalarGridSpec(
            num_scalar_prefetch=2, grid=(t // tm,),
            in_specs=[pl.BlockSpec((tm, d), lambda i, c, s: (i, 0))],
            out_specs=pl.BlockSpec(memory_space=pl.ANY),
            scratch_shapes=[pltpu.VMEM((MOE_TILE, d), hn.dtype), pltpu.SemaphoreType.DMA(()),
                            pltpu.SemaphoreType.DMA(())],
        ),
        out_shape=jax.ShapeDtypeStruct((cap, d), hn.dtype),
        compiler_params=_cparams(("arbitrary",)), name="moe_dispatch",
    )(pos, seg, hn)


def _expert_kernel(blk_e_ref, n_used_ref, x_ref, w1_ref, w3_ref, w2_ref, y_ref):
    j = pl.program_id(0)

    @pl.when(j < n_used_ref[0])
    def _():
        x = _unpack_halves(x_ref[...]).astype(BF16)
        a = jnp.dot(x, w1_ref[0].astype(BF16), preferred_element_type=F32)
        b = jnp.dot(x, w3_ref[0].astype(BF16), preferred_element_type=F32)
        hdn = (a * jax.nn.sigmoid(a) * b).astype(BF16)
        y_ref[...] = _pack_halves(jnp.dot(hdn, w2_ref[0].astype(BF16), preferred_element_type=F32))

    @pl.when(j >= n_used_ref[0])
    def _():
        y_ref[...] = jnp.zeros_like(y_ref)


def _experts(xs, blk_e, n_used, w1, w3, w2, tm):
    cap, dp = xs.shape
    e, d, ff = w1.shape
    nb = cap // tm
    wspec = lambda r, c: pl.BlockSpec((1, r, c), lambda j, be, nu: (be[j], 0, 0))
    xrow = lambda j, be, nu: (jnp.minimum(j, nu[0] - 1), 0)
    return pl.pallas_call(
        _expert_kernel,
        grid_spec=pltpu.PrefetchScalarGridSpec(
            num_scalar_prefetch=2, grid=(nb,),
            in_specs=[pl.BlockSpec((tm, dp), xrow), wspec(d, ff), wspec(d, ff), wspec(ff, d)],
            out_specs=pl.BlockSpec((tm, dp), lambda j, be, nu: (j, 0)),
        ),
        out_shape=jax.ShapeDtypeStruct((cap, dp), xs.dtype),
        compiler_params=_cparams(("arbitrary",)), name="expert_ffn",
    )(blk_e, n_used, xs, w1, w3, w2)


def _combine_kernel(pos_ref, h_ref, route_ref, gf_ref, yb_hbm, o_ref, ybuf, sems, *, tm):
    i = pl.program_id(0)
    last = pl.num_programs(0) - 1
    slot = lax.rem(i, 2)

    def start(tile, s):
        for c in range(2):
            _start_row_gather(yb_hbm, lambda k: pos_ref[k], tile * 2 * tm + c, 2, ybuf.at[s], c * tm, sems.at[s], tm)

    @pl.when(i == 0)
    def _():
        start(0, 0)

    _wait_rows(yb_hbm, ybuf.at[slot], sems.at[slot])
    nxt = jnp.minimum(i + 1, last)
    for s in range(2):
        @pl.when(slot == s)
        def _():
            start(nxt, 1 - s)
            route = route_ref[...]
            w1 = route[:, 2:3]
            w2 = route[:, 3:4]
            h = h_ref[...] + w1 * _unpack_halves(ybuf[s, :tm, :]) + w2 * _unpack_halves(ybuf[s, tm:, :])
            o_ref[...] = h * lax.rsqrt(jnp.mean(h * h, axis=-1, keepdims=True) + RMS_EPS) * gf_ref[...]

    @pl.when(i == last)
    def _():
        _wait_rows(yb_hbm, ybuf.at[1 - slot], sems.at[1 - slot])


def _combine(pos, h, route, gf, yb, tm):
    t, d = h.shape
    return pl.pallas_call(
        functools.partial(_combine_kernel, tm=tm),
        grid_spec=pltpu.PrefetchScalarGridSpec(
            num_scalar_prefetch=1, grid=(t // tm,),
            in_specs=[pl.BlockSpec((tm, d), lambda i, p: (i, 0)), pl.BlockSpec((tm, LANES), lambda i, p: (i, 0)),
                      pl.BlockSpec((1, d), lambda i, p: (0, 0)), pl.BlockSpec(memory_space=pl.ANY)],
            out_specs=pl.BlockSpec((tm, d), lambda i, p: (i, 0)),
            scratch_shapes=[pltpu.VMEM((2, 2 * tm) + yb.shape[1:], yb.dtype), pltpu.SemaphoreType.DMA((2,))],
        ),
        out_shape=jax.ShapeDtypeStruct((t, d), F32),
        compiler_params=_cparams(("arbitrary",)), name="moe_combine",
    )(pos, h, route, gf, yb)


def _position_kernel(seg_ref, code_ref, pos_ref):
    code = code_ref[...]
    expert = lax.shift_right_logical(code, RANK_BITS)
    pos = jnp.bitwise_and(code, (1 << RANK_BITS) - 1)
    for e in range(N_EXPERTS):
        pos = pos + jnp.where(expert == e, seg_ref[e], 0)
    pos_ref[...] = pos


def _dispatch_plan(route, counts, tm):
    t = route.shape[0]
    n = 2 * t
    nb = (n + N_EXPERTS * (tm - 1)) // tm + 1
    cnt = counts[0, :N_EXPERTS].astype(jnp.int32)
    padded = ((cnt + tm - 1) // tm) * tm
    pend = jnp.cumsum(padded)
    pstart = pend - padded
    seg = jnp.concatenate([pstart, pstart + cnt, pend]).astype(jnp.int32)
    blk_start = jnp.arange(nb, dtype=jnp.int32) * tm
    blk_e = jnp.minimum(jnp.sum((blk_start[:, None] >= pend[None, :]).astype(jnp.int32), axis=1), N_EXPERTS - 1)
    n_used = (pend[-1:] // tm).astype(jnp.int32)
    code = route[:, :2].astype(jnp.int32).reshape(n // LANES, LANES)
    whole = pl.BlockSpec(code.shape, lambda i, s: (0, 0))
    pos = pl.pallas_call(
        _position_kernel,
        grid_spec=pltpu.PrefetchScalarGridSpec(num_scalar_prefetch=1, grid=(1,), in_specs=[whole], out_specs=whole),
        out_shape=jax.ShapeDtypeStruct(code.shape, jnp.int32),
        compiler_params=_cparams(("arbitrary",)), name="slot_positions",
    )(seg, code)
    return nb * tm, blk_e.astype(jnp.int32), n_used, pos.reshape(n), seg


def kernel(x, meta_tokens, norm_mix_g, w_in, ssm_lambda_re, ssm_lambda_im, ssm_log_dt, ssm_b_re, ssm_b_im,
           ssm_c_re, ssm_c_im, ssm_d, ssm_glu_w, ssm_glu_b, w_branch_attn, w_branch_ssm, w_out, norm_ffn_g,
           router_group_w, router_group_b, router_expert_w, router_expert_b, expert_w1, expert_w3, expert_w2,
           norm_final_g):
    b, l, d = x.shape
    depth = w_in.shape[0]
    aw = w_branch_attn.shape[1]
    sw = w_branch_ssm.shape[1]
    heads = aw // HEAD_DIM
    t = b * l
    pad = (-N_META) % Q_BLOCK
    assert depth == 1 and l % Q_BLOCK == 0 and pad + N_META == Q_BLOCK and SSM_CHUNK == Q_BLOCK
    layer = 0
    h_real = x.reshape(t, d)
    h_meta = meta_tokens.astype(x.dtype)
    tm_tok = 512

    wi = w_in[layer].astype(BF16)
    o = 3 * aw + sw
    ws = (wi[:, :aw], wi[:, aw:2 * aw].T, wi[:, 2 * aw:3 * aw], wi[:, 3 * aw:o].T, wi[:, o:o + d], wi[:, o + d:])
    g_mix = norm_mix_g[layer].reshape(1, d)
    q, kt, v, ut, ga, gb = _inproj(h_real, g_mix, ws, tm_tok)
    _, kt_m, v_m, ut_m, _, _ = _inproj(h_meta, g_mix, ws, N_META)
    kt_meta = jnp.concatenate([jnp.zeros((aw, pad), BF16), kt_m], axis=1)
    v_meta = jnp.concatenate([jnp.zeros((pad, aw), BF16), v_m], axis=0)
    ut_meta = jnp.concatenate([jnp.zeros((sw, pad), BF16), ut_m], axis=1)

    attn = _attention(q, kt, v, kt_meta, v_meta, b, heads, pad)

    ssm_params = (ssm_lambda_re[layer], ssm_lambda_im[layer], ssm_log_dt[layer], ssm_b_re[layer], ssm_b_im[layer],
                  ssm_c_re[layer], ssm_c_im[layer], ssm_d[layer])
    yt = _ssm(ut, ut_meta, ssm_params, b)

    wr = jnp.zeros((d, LANES), F32)
    wr = wr.at[:, :N_EXPERTS].set(router_expert_w[layer]).at[:, N_EXPERTS:N_EXPERTS + MOE_GROUPS].set(router_group_w[layer])
    br = jnp.zeros((1, LANES), F32)
    br = br.at[0, :N_EXPERTS].set(router_expert_b[layer]).at[0, N_EXPERTS:N_EXPERTS + MOE_GROUPS].set(router_group_b[layer])
    h_mid, hn, route, counts = _merge(
        h_real, attn, yt, ga, gb, ssm_glu_w[layer].T.astype(BF16), ssm_glu_b[layer].reshape(sw, 1).astype(F32),
        w_branch_attn[layer].astype(BF16), w_branch_ssm[layer].astype(BF16), w_out[layer].astype(BF16),
        norm_ffn_g[layer].reshape(1, d), wr, br, tm_tok)

    cap, blk_e, n_used, pos, seg = _dispatch_plan(route, counts, MOE_TILE)
    xs = _dispatch(hn, pos, seg, cap, 256)
    yb = _experts(xs, blk_e, n_used, expert_w1[layer], expert_w3[layer], expert_w2[layer], MOE_TILE)
    out = _combine(pos, h_mid, route, norm_final_g.reshape(1, d), yb, 256)
    return out.reshape(b, l, d)
```

```python
import functools

import jax
import jax.numpy as jnp
from jax import lax
from jax.experimental import pallas as pl
from jax.experimental.pallas import tpu as pltpu

F32 = jnp.float32
BF16 = jnp.bfloat16

N_META = 16
Q_BLOCK = 128
HEAD_DIM = 64
SSM_GROUP_CH = 16
SSM_STATE = 64
MOE_GROUPS = 4
EXPERTS_PER_GROUP = 8
N_EXPERTS = MOE_GROUPS * EXPERTS_PER_GROUP
RMS_EPS = 1e-6

LANES = 128
SUBLANES = 8
SSM_CHUNK = LANES
MOE_TILE = 512
STICK_CUTOFF = 104.0
VMEM_LIMIT = 52 * 1024 * 1024

_NT = (((1,), (1,)), ((), ()))
_TN = (((0,), (0,)), ((), ()))


def _cparams(sem):
    return pltpu.CompilerParams(dimension_semantics=sem, vmem_limit_bytes=VMEM_LIMIT)


def _inproj_kernel(x_ref, g_ref, wq_ref, wkt_ref, wv_ref, wut_ref, wga_ref, wgb_ref,
                   q_ref, kt_ref, v_ref, ut_ref, ga_ref, gb_ref):
    x = x_ref[...]
    ms = jnp.mean(x * x, axis=-1, keepdims=True)
    xn = (x * lax.rsqrt(ms + RMS_EPS) * g_ref[...]).astype(BF16)
    q_ref[...] = (jnp.dot(xn, wq_ref[...], preferred_element_type=F32) * (HEAD_DIM ** -0.5)).astype(BF16)
    kt_ref[...] = lax.dot_general(wkt_ref[...], xn, _NT, preferred_element_type=F32).astype(BF16)
    v_ref[...] = jnp.dot(xn, wv_ref[...], preferred_element_type=F32).astype(BF16)
    ut_ref[...] = lax.dot_general(wut_ref[...], xn, _NT, preferred_element_type=F32).astype(BF16)
    ga_ref[...] = jax.nn.sigmoid(jnp.dot(xn, wga_ref[...], preferred_element_type=F32)).astype(BF16)
    gb_ref[...] = jax.nn.sigmoid(jnp.dot(xn, wgb_ref[...], preferred_element_type=F32)).astype(BF16)


def _inproj(x2, g, ws, tm):
    m, d = x2.shape
    aw, sw = ws[0].shape[1], ws[3].shape[0]
    row = lambda n: pl.BlockSpec((tm, n), lambda i: (i, 0))
    col = lambda n: pl.BlockSpec((n, tm), lambda i: (0, i))
    full = lambda a: pl.BlockSpec(a.shape, lambda i: (0, 0))
    sds = jax.ShapeDtypeStruct
    return pl.pallas_call(
        _inproj_kernel,
        grid=(m // tm,),
        in_specs=[row(d), full(g)] + [full(w) for w in ws],
        out_specs=[row(aw), col(aw), row(aw), col(sw), row(d), row(d)],
        out_shape=[sds((m, aw), BF16), sds((aw, m), BF16), sds((m, aw), BF16), sds((sw, m), BF16),
                   sds((m, d), BF16), sds((m, d), BF16)],
        compiler_params=_cparams(("parallel",)),
        name="inproj",
    )(x2, g, *ws)


def _attn_kernel(q_ref, kt_ref, v_ref, ktm_ref, vm_ref, tri_ref, o_ref, qm_ref, carry_ref, acc_ref, *, heads, pad):
    i = pl.program_id(1)
    pairs = heads // 2
    carry_ref[...] = jnp.zeros_like(carry_ref)
    acc_ref[...] = jnp.zeros_like(acc_ref)
    lane = lax.broadcasted_iota(jnp.int32, (Q_BLOCK, LANES), 1)
    row = lax.broadcasted_iota(jnp.int32, (Q_BLOCK, LANES), 0)
    low_half = lane < HEAD_DIM
    rows = lambda h: slice(h * Q_BLOCK, (h + 1) * Q_BLOCK)
    for p in range(pairs):
        qp = q_ref[0, :, p * LANES:(p + 1) * LANES]
        qm_ref[rows(2 * p)] = jnp.where(low_half, qp, jnp.zeros_like(qp))
        qm_ref[rows(2 * p + 1)] = jnp.where(low_half, jnp.zeros_like(qp), qp)

    def block_step(get_kt, get_v, valid):
        kts = [get_kt(p) for p in range(pairs)]
        vs = [get_v(p) for p in range(pairs)]
        zs = [jnp.dot(qm_ref[rows(h)], kts[h // 2], preferred_element_type=F32) for h in range(heads)]
        drops = []
        for h in range(heads):
            z = zs[h].astype(BF16)
            drop = jnp.maximum(z, 0.0) + jnp.log(1.0 + jnp.exp(-jnp.abs(z)))
            if valid is not None:
                drop = jnp.where(valid, drop, jnp.zeros_like(drop))
            drops.append(drop)
        sums = [jnp.dot(drops[h], tri_ref[...], preferred_element_type=F32) for h in range(heads)]
        low = None
        for h in range(heads):
            carry = carry_ref[rows(h)]
            w = jnp.exp(zs[h] - (sums[h][:, :LANES] + carry))
            if valid is not None:
                w = jnp.where(valid, w, 0.0)
            acc_ref[rows(h)] += jnp.dot(w.astype(BF16), vs[h // 2], preferred_element_type=F32)
            carry = carry + sums[h][:, LANES:]
            carry_ref[rows(h)] = carry
            low = carry if low is None else jnp.minimum(low, carry)
        return jnp.min(low)

    off_d = pl.multiple_of(i * Q_BLOCK, Q_BLOCK)
    low0 = block_step(lambda p: kt_ref[p * LANES:(p + 1) * LANES, pl.ds(off_d, Q_BLOCK)],
                      lambda p: v_ref[0, pl.ds(off_d, Q_BLOCK), p * LANES:(p + 1) * LANES],
                      lane < row)

    def cond(state):
        kb, low = state
        return jnp.logical_and(kb >= 0, low < STICK_CUTOFF)

    def body(state):
        kb, _ = state
        off = pl.multiple_of(kb * Q_BLOCK, Q_BLOCK)
        low = block_step(lambda p: kt_ref[p * LANES:(p + 1) * LANES, pl.ds(off, Q_BLOCK)],
                         lambda p: v_ref[0, pl.ds(off, Q_BLOCK), p * LANES:(p + 1) * LANES],
                         None)
        return kb - 1, low

    _, low1 = lax.while_loop(cond, body, (i - 1, low0))

    @pl.when(low1 < STICK_CUTOFF)
    def _():
        block_step(lambda p: ktm_ref[p * LANES:(p + 1) * LANES, :],
                   lambda p: vm_ref[:, p * LANES:(p + 1) * LANES],
                   lane >= pad)

    for p in range(pairs):
        even = acc_ref[2 * p * Q_BLOCK:(2 * p + 1) * Q_BLOCK]
        odd = acc_ref[(2 * p + 1) * Q_BLOCK:(2 * p + 2) * Q_BLOCK]
        o_ref[0, :, p * LANES:(p + 1) * LANES] = jnp.where(low_half, even, odd).astype(o_ref.dtype)


def _attention(q, kt, v, kt_meta, v_meta, b, heads, pad):
    t, aw = q.shape
    l = t // b
    r = lax.broadcasted_iota(jnp.int32, (LANES, 2 * LANES), 0)
    c = lax.broadcasted_iota(jnp.int32, (LANES, 2 * LANES), 1)
    tri = jnp.where(jnp.logical_or(c >= LANES, r >= c), 1.0, 0.0).astype(BF16)
    once = pl.Buffered(1)
    out = pl.pallas_call(
        functools.partial(_attn_kernel, heads=heads, pad=pad),
        grid=(b, l // Q_BLOCK),
        in_specs=[
            pl.BlockSpec((1, Q_BLOCK, aw), lambda bi, i: (bi, i, 0)),
            pl.BlockSpec((aw, l), lambda bi, i: (0, bi), pipeline_mode=once),
            pl.BlockSpec((1, l, aw), lambda bi, i: (bi, 0, 0), pipeline_mode=once),
            pl.BlockSpec((aw, Q_BLOCK), lambda bi, i: (0, 0)),
            pl.BlockSpec((Q_BLOCK, aw), lambda bi, i: (0, 0)),
            pl.BlockSpec((LANES, 2 * LANES), lambda bi, i: (0, 0)),
        ],
        out_specs=pl.BlockSpec((1, Q_BLOCK, aw), lambda bi, i: (bi, i, 0)),
        out_shape=jax.ShapeDtypeStruct((b, l, aw), BF16),
        scratch_shapes=[pltpu.VMEM((heads * Q_BLOCK, LANES), BF16), pltpu.VMEM((heads * Q_BLOCK, LANES), F32),
                        pltpu.VMEM((heads * Q_BLOCK, LANES), F32)],
        compiler_params=_cparams(("parallel", "arbitrary")),
        name="stick_attn",
    )(q.reshape(b, l, aw), kt, v.reshape(b, l, aw), kt_meta, v_meta, tri)
    return out.reshape(t, aw)


def _ssm_tables(lam_re, lam_im, log_dt, b_re, b_im, c_re, c_im, d_skip):
    hp = lax.Precision.HIGHEST
    tc = SSM_CHUNK
    g, p = lam_re.shape
    ch = b_re.shape[-1]
    dt = jnp.exp(log_dt)[:, None]
    mag = jnp.exp(lam_re * dt)
    lb_re, lb_im = mag * jnp.cos(lam_im * dt), mag * jnp.sin(lam_im * dt)
    nr, ni = lb_re - 1.0, lb_im
    den = lam_re * lam_re + lam_im * lam_im
    f_re = (nr * lam_re + ni * lam_im) / den
    f_im = (ni * lam_re - nr * lam_im) / den
    bb_re = f_re[:, :, None] * b_re - f_im[:, :, None] * b_im
    bb_im = f_re[:, :, None] * b_im + f_im[:, :, None] * b_re
    k = jnp.arange(tc + 1, dtype=F32)
    pw_mag = jnp.exp((lam_re * dt)[:, :, None] * k)
    pw_ang = (lam_im * dt)[:, :, None] * k
    pw_re, pw_im = pw_mag * jnp.cos(pw_ang), pw_mag * jnp.sin(pw_ang)
    dec_re = pw_re[:, :, tc].reshape(g // 2, 1, 2 * p)
    dec_im = pw_im[:, :, tc].reshape(g // 2, 1, 2 * p)
    ct_re, ct_im = c_re.transpose(0, 2, 1)[..., None], c_im.transpose(0, 2, 1)[..., None]
    cl_re = ct_re * pw_re[:, :, None, :] - ct_im * pw_im[:, :, None, :]
    cl_im = ct_re * pw_im[:, :, None, :] + ct_im * pw_re[:, :, None, :]
    kern = (jnp.einsum('gpot,gpi->giot', cl_re[..., :tc], bb_re, precision=hp)
            - jnp.einsum('gpot,gpi->giot', cl_im[..., :tc], bb_im, precision=hp))
    kern = kern.at[:, :, :, 0].add(jnp.eye(ch, dtype=F32)[None] * d_skip.reshape(g, ch, 1))
    powers = jnp.stack([pw_re[:, :, tc - 1::-1], pw_im[:, :, tc - 1::-1], pw_re[:, :, 1:], pw_im[:, :, 1:]], axis=1)
    coeffs = jnp.stack([bb_re, bb_im, ct_re[..., 0], ct_im[..., 0]], axis=1)
    return kern, powers, coeffs, dec_re, dec_im


def _ssm_kernel(u_ref, um_ref, k_ref, pw_ref, cf_ref, dre_ref, dim_ref, y_ref,
                toep_ref, sre_ref, sim_ref, xre_ref, xim_ref, *, batch):
    ch, tc = SSM_GROUP_CH, SSM_CHUNK
    nch = u_ref.shape[1]
    per_b = nch // batch
    p = pw_ref.shape[2]

    def chunk_rows(ref, gl):
        return jnp.concatenate([ref[gl * ch + ci] for ci in range(ch)], axis=1)

    u = [chunk_rows(u_ref, gl) for gl in range(2)]
    um = [jnp.concatenate([jnp.broadcast_to(um_ref[gl * ch + ci:gl * ch + ci + 1, :], (8, tc)) for ci in range(ch)], axis=1)
          for gl in range(2)]

    def expand(gl, which):
        a_re, a_im = pw_ref[gl, 2 * which], pw_ref[gl, 2 * which + 1]
        c_re, c_im = cf_ref[gl, 2 * which], cf_ref[gl, 2 * which + 1]
        m_re, m_im = [], []
        for c in range(ch):
            b_re = jnp.broadcast_to(c_re[:, c:c + 1], (p, tc))
            b_im = jnp.broadcast_to(c_im[:, c:c + 1], (p, tc))
            m_re.append(a_re * b_re - a_im * b_im)
            m_im.append(a_re * b_im + a_im * b_re)
        zeros = jnp.zeros((p, ch * tc), BF16)

        def place(parts):
            m = jnp.concatenate(parts, axis=1).astype(BF16)
            return jnp.concatenate([m, zeros] if gl == 0 else [zeros, m], axis=0)
        return place(m_re), place(m_im)

    to_st = [expand(gl, 0) for gl in range(2)]

    def to_state(lhs, part):
        return (lax.dot_general(lhs[0], to_st[0][part], _NT, preferred_element_type=F32)
                + lax.dot_general(lhs[1], to_st[1][part], _NT, preferred_element_type=F32))

    sre_ref[...] = to_state(u, 0)
    sim_ref[...] = to_state(u, 1)
    x0_re = to_state(um, 0)[0:1]
    x0_im = to_state(um, 1)[0:1]
    d_re, d_im = dre_ref[0], dim_ref[0]

    def step(c, xs):
        new = []
        for bi in range(batch):
            x_re, x_im = xs[2 * bi], xs[2 * bi + 1]
            r = bi * per_b + c
            xre_ref[pl.ds(r, 1), :] = x_re
            xim_ref[pl.ds(r, 1), :] = x_im
            new.append(d_re * x_re - d_im * x_im + sre_ref[pl.ds(r, 1), :])
            new.append(d_re * x_im + d_im * x_re + sim_ref[pl.ds(r, 1), :])
        return tuple(new)

    lax.fori_loop(0, per_b, step, (x0_re, x0_im) * batch)
    xs_re = xre_ref[...].astype(BF16)
    xs_im = xim_ref[...].astype(BF16)

    lane = lax.broadcasted_iota(jnp.int32, (tc, tc), 1)
    row = lax.broadcasted_iota(jnp.int32, (tc, tc), 0)
    causal = lane >= row
    for gl in range(2):
        def fill(ci, carry):
            for co in range(ch):
                taps = jnp.broadcast_to(k_ref[gl, ci, pl.ds(co, 1), :], (tc, tc))
                blk = pltpu.roll(taps, 0, 1, stride=1, stride_axis=0)
                blk = jnp.where(causal, blk, 0.0).astype(BF16)
                toep_ref[pl.ds(pl.multiple_of(ci * tc, tc), tc), co * tc:(co + 1) * tc] = blk
            return carry
        lax.fori_loop(0, ch, fill, 0)
        from_re, from_im = expand(gl, 1)
        y = jnp.dot(u[gl], toep_ref[...], preferred_element_type=F32)
        y = y + jnp.dot(xs_re, from_re, preferred_element_type=F32)
        y = y - jnp.dot(xs_im, from_im, preferred_element_type=F32)
        y = jax.nn.gelu(y).astype(y_ref.dtype)
        for co in range(ch):
            y_ref[gl * ch + co] = y[:, co * tc:(co + 1) * tc]


def _ssm(ut, ut_meta, params, batch):
    s, t = ut.shape
    tc, ch, p = SSM_CHUNK, SSM_GROUP_CH, SSM_STATE
    g = s // ch
    nch = t // tc
    kern, powers, coeffs, dec_re, dec_im = _ssm_tables(*params)
    u3 = ut.reshape(s, nch, tc)
    y3 = pl.pallas_call(
        functools.partial(_ssm_kernel, batch=batch),
        grid=(g // 2,),
        in_specs=[
            pl.BlockSpec((2 * ch, nch, tc), lambda j: (j, 0, 0)),
            pl.BlockSpec((2 * ch, tc), lambda j: (j, 0)),
            pl.BlockSpec((2, ch, ch, tc), lambda j: (j, 0, 0, 0)),
            pl.BlockSpec((2, 4, p, tc), lambda j: (j, 0, 0, 0)), pl.BlockSpec((2, 4, p, ch), lambda j: (j, 0, 0, 0)),
            pl.BlockSpec((1, 1, 2 * p), lambda j: (j, 0, 0)), pl.BlockSpec((1, 1, 2 * p), lambda j: (j, 0, 0)),
        ],
        out_specs=pl.BlockSpec((2 * ch, nch, tc), lambda j: (j, 0, 0)),
        out_shape=jax.ShapeDtypeStruct((s, nch, tc), BF16),
        scratch_shapes=[pltpu.VMEM((ch * tc, ch * tc), BF16)] + [pltpu.VMEM((nch, 2 * p), F32)] * 4,
        compiler_params=_cparams(("parallel",)), name="ssm_chunked",
    )(u3, ut_meta, kern, powers, coeffs, dec_re, dec_im)
    return y3.reshape(s, t)


RANK_BITS = 16


def _pack_halves(x):
    n = x.shape[1] // 2
    as_bits = lambda v: lax.bitcast_convert_type(v.astype(BF16).astype(F32), jnp.uint32)
    return jnp.bitwise_or(lax.shift_right_logical(as_bits(x[:, :n]), jnp.uint32(16)),
                          jnp.bitwise_and(as_bits(x[:, n:]), jnp.uint32(0xFFFF0000)))


def _unpack_halves(w):
    lo = lax.bitcast_convert_type(lax.shift_left(w, jnp.uint32(16)), F32)
    hi = lax.bitcast_convert_type(jnp.bitwise_and(w, jnp.uint32(0xFFFF0000)), F32)
    return jnp.concatenate([lo, hi], axis=1)


def _merge_kernel(x_ref, a_ref, yt_ref, ga_ref, gb_ref, gwt_ref, gbias_ref, wpa_ref, wpb_ref, wo_ref, gn_ref,
                  wr_ref, br_ref, ltri_ref, h_ref, hn_ref, route_ref, count_ref, run_ref):
    @pl.when(pl.program_id(0) == 0)
    def _():
        run_ref[...] = jnp.zeros_like(run_ref)

    yt = yt_ref[...]
    gate = jax.nn.sigmoid(jnp.dot(gwt_ref[...], yt, preferred_element_type=F32) + gbias_ref[...])
    ssm = (yt.astype(F32) * gate).T.astype(BF16)
    merged = (ga_ref[...].astype(F32) * jnp.dot(a_ref[...], wpa_ref[...], preferred_element_type=F32)
              + gb_ref[...].astype(F32) * jnp.dot(ssm, wpb_ref[...], preferred_element_type=F32))
    h = x_ref[...] + jnp.dot(merged.astype(BF16), wo_ref[...], preferred_element_type=F32)
    h_ref[...] = h
    hn = h * lax.rsqrt(jnp.mean(h * h, axis=-1, keepdims=True) + RMS_EPS) * gn_ref[...]
    hn_ref[...] = _pack_halves(hn)
    hn_hi = hn.astype(BF16)
    hn_lo = (hn - hn_hi.astype(F32)).astype(BF16)
    logits = (jnp.dot(hn_hi, wr_ref[0], preferred_element_type=F32)
              + jnp.dot(hn_hi, wr_ref[1], preferred_element_type=F32)
              + jnp.dot(hn_lo, wr_ref[0], preferred_element_type=F32)) + br_ref[...]
    lane = lax.broadcasted_iota(jnp.int32, logits.shape, 1)
    neg = jnp.full_like(logits, -jnp.inf)
    big = jnp.int32(LANES)
    is_grp = jnp.logical_and(lane >= N_EXPERTS, lane < N_EXPERTS + MOE_GROUPS)
    glog = jnp.where(is_grp, logits, neg)
    gmax = jnp.max(glog, axis=-1, keepdims=True)
    g_top = jnp.min(jnp.where(glog == gmax, lane, big), axis=-1, keepdims=True) - N_EXPERTS
    p_top = 1.0 / jnp.sum(jnp.exp(glog - gmax), axis=-1, keepdims=True)
    in_grp = (lane // EXPERTS_PER_GROUP) == g_top
    elog = jnp.where(jnp.logical_and(in_grp, lane < N_EXPERTS), logits, neg)
    e1 = jnp.max(elog, axis=-1, keepdims=True)
    i1 = jnp.min(jnp.where(elog == e1, lane, big), axis=-1, keepdims=True)
    elog2 = jnp.where(lane == i1, neg, elog)
    e2 = jnp.max(elog2, axis=-1, keepdims=True)
    i2 = jnp.min(jnp.where(elog2 == e2, lane, big), axis=-1, keepdims=True)
    t = jnp.exp(e2 - e1)
    w1 = p_top / (1.0 + t)
    w2 = p_top * t / (1.0 + t)
    hit1, hit2 = lane == i1, lane == i2
    chosen = jnp.where(jnp.logical_or(hit1, hit2), 1.0, 0.0)
    before = jnp.dot(ltri_ref[...], chosen.astype(BF16), preferred_element_type=F32) + run_ref[0:1, :]
    r1 = jnp.sum(jnp.where(hit1, before, 0.0), axis=-1, keepdims=True)
    r2 = jnp.sum(jnp.where(hit2, before, 0.0), axis=-1, keepdims=True)
    run_ref[...] = run_ref[...] + jnp.sum(chosen, axis=0, keepdims=True)
    count_ref[...] = run_ref[...]
    scale = float(2 ** RANK_BITS)
    route = jnp.where(lane == 0, i1.astype(F32) * scale + r1,
                      jnp.where(lane == 1, i2.astype(F32) * scale + r2,
                                jnp.where(lane == 2, w1, jnp.where(lane == 3, w2, 0.0))))
    route_ref[...] = route


def _merge(x2, attn, yt, ga, gb, glu_wt, glu_b, wpa, wpb, wo, gn, wr, br, tm):
    m, d = x2.shape
    aw, sw = attn.shape[1], yt.shape[0]
    row = lambda n: pl.BlockSpec((tm, n), lambda i: (i, 0))
    full = lambda a: pl.BlockSpec(a.shape, lambda i: (0,) * a.ndim)
    wr_hi = wr.astype(BF16)
    wr = jnp.stack([wr_hi, (wr - wr_hi.astype(F32)).astype(BF16)])
    ltri = jnp.where(lax.broadcasted_iota(jnp.int32, (tm, tm), 0) > lax.broadcasted_iota(jnp.int32, (tm, tm), 1),
                     1.0, 0.0).astype(BF16)
    ws = (glu_wt, glu_b, wpa, wpb, wo, gn, wr, br, ltri)
    return pl.pallas_call(
        _merge_kernel, grid=(m // tm,),
        in_specs=[row(d), row(aw), pl.BlockSpec((sw, tm), lambda i: (0, i)), row(d), row(d)] + [full(a) for a in ws],
        out_specs=[row(d), row(d // 2), row(LANES), pl.BlockSpec((8, LANES), lambda i: (0, 0))],
        out_shape=[jax.ShapeDtypeStruct((m, d), F32), jax.ShapeDtypeStruct((m, d // 2), jnp.uint32),
                   jax.ShapeDtypeStruct((m, LANES), F32), jax.ShapeDtypeStruct((8, LANES), F32)],
        scratch_shapes=[pltpu.VMEM((8, LANES), F32)],
        compiler_params=_cparams(("arbitrary",)), name="merge_router",
    )(x2, attn, yt, ga, gb, *ws)


def _start_row_gather(src_hbm, row_of, base, stride, buf, row0, sem, n):
    for r in range(n):
        row = row_of(base + stride * r)
        pltpu.make_async_copy(src_hbm.at[pl.ds(row, 1)], buf.at[pl.ds(row0 + r, 1)], sem).start(priority=r % 2)


def _wait_rows(src_hbm, buf, sem):
    pltpu.make_async_copy(src_hbm.at[pl.ds(0, buf.shape[0])], buf, sem).wait()


def _dispatch_kernel(pos_ref, seg_ref, hn_ref, xs_hbm, zero_ref, sem, zsem, *, tm):
    i = pl.program_id(0)

    @pl.when(i == 0)
    def _():
        zero_ref[...] = jnp.zeros_like(zero_ref)
        zrows = zero_ref.shape[0]

        def zero_run(lo, hi, act):
            def copy(start, n):
                return pltpu.make_async_copy(zero_ref.at[pl.ds(0, n)], xs_hbm.at[pl.ds(start, n)], zsem)
            head = jnp.minimum(jnp.bitwise_and(-lo, SUBLANES - 1), hi - lo)
            lax.fori_loop(0, head, lambda k, c: (act(copy(lo + k, 1)), c)[1], 0)
            lo = lo + head
            n_full = (hi - lo) // zrows
            lax.fori_loop(0, n_full,
                          lambda k, c: (act(copy(pl.multiple_of(lo + k * zrows, SUBLANES), zrows)), c)[1], 0)
            rest = (hi - lo) - n_full * zrows
            start = lo + n_full * zrows
            bit = zrows // 2
            while bit >= SUBLANES:
                pl.when(jnp.bitwise_and(rest, bit) != 0)(
                    functools.partial(lambda st, n: act(copy(pl.multiple_of(st, SUBLANES), n)), start, bit))
                start = start + jnp.bitwise_and(rest, bit)
                bit //= 2

        def each_gap(act):
            def run(e, carry):
                zero_run(seg_ref[N_EXPERTS + e], seg_ref[2 * N_EXPERTS + e], act)
                return carry
            lax.fori_loop(0, N_EXPERTS, run, 0)
            zero_run(seg_ref[3 * N_EXPERTS - 1], xs_hbm.shape[0], act)
        each_gap(lambda cp: cp.start())
        each_gap(lambda cp: cp.wait())

    for r in range(tm):
        for c in range(2):
            slot = pos_ref[2 * (i * tm + r) + c]
            pltpu.make_async_copy(hn_ref.at[pl.ds(r, 1)], xs_hbm.at[pl.ds(slot, 1)], sem).start(priority=c)
    pltpu.make_async_copy(xs_hbm.at[pl.ds(0, 2 * tm)], xs_hbm.at[pl.ds(0, 2 * tm)], sem).wait()


def _dispatch(hn, pos, seg, cap, tm):
    t, d = hn.shape
    return pl.pallas_call(
        functools.partial(_dispatch_kernel, tm=tm),
        grid_spec=pltpu.PrefetchScalarGridSpec(
            num_scalar_prefetch=2, grid=(t // tm,),
            in_specs=[pl.BlockSpec((tm, d), lambda i, c, s: (i, 0))],
            out_specs=pl.BlockSpec(memory_space=pl.ANY),
            scratch_shapes=[pltpu.VMEM((MOE_TILE, d), hn.dtype), pltpu.SemaphoreType.DMA(()),
                            pltpu.SemaphoreType.DMA(())],
        ),
        out_shape=jax.ShapeDtypeStruct((cap, d), hn.dtype),
        compiler_params=_cparams(("arbitrary",)), name="moe_dispatch",
    )(pos, seg, hn)


def _expert_kernel(blk_e_ref, n_used_ref, x_ref, w1_ref, w3_ref, w2_ref, y_ref):
    j = pl.program_id(0)

    @pl.when(j < n_used_ref[0])
    def _():
        x = _unpack_halves(x_ref[...]).astype(BF16)
        a = jnp.dot(x, w1_ref[0].astype(BF16), preferred_element_type=F32)
        b = jnp.dot(x, w3_ref[0].astype(BF16), preferred_element_type=F32)
        hdn = (a * jax.nn.sigmoid(a) * b).astype(BF16)
        y_ref[...] = _pack_halves(jnp.dot(hdn, w2_ref[0].astype(BF16), preferred_element_type=F32))

    @pl.when(j >= n_used_ref[0])
    def _():
        y_ref[...] = jnp.zeros_like(y_ref)


def _experts(xs, blk_e, n_used, w1, w3, w2, tm):
    cap, dp = xs.shape
    e, d, ff = w1.shape
    nb = cap // tm
    wspec = lambda r, c: pl.BlockSpec((1, r, c), lambda j, be, nu: (be[j], 0, 0))
    xrow = lambda j, be, nu: (jnp.minimum(j, nu[0] - 1), 0)
    return pl.pallas_call(
        _expert_kernel,
        grid_spec=pltpu.PrefetchScalarGridSpec(
            num_scalar_prefetch=2, grid=(nb,),
            in_specs=[pl.BlockSpec((tm, dp), xrow), wspec(d, ff), wspec(d, ff), wspec(ff, d)],
            out_specs=pl.BlockSpec((tm, dp), lambda j, be, nu: (j, 0)),
        ),
        out_shape=jax.ShapeDtypeStruct((cap, dp), xs.dtype),
        compiler_params=_cparams(("arbitrary",)), name="expert_ffn",
    )(blk_e, n_used, xs, w1, w3, w2)


def _combine_kernel(pos_ref, h_ref, route_ref, gf_ref, yb_hbm, o_ref, ybuf, sems, *, tm):
    i = pl.program_id(0)
    last = pl.num_programs(0) - 1
    slot = lax.rem(i, 2)

    def start(tile, s):
        for c in range(2):
            _start_row_gather(yb_hbm, lambda k: pos_ref[k], tile * 2 * tm + c, 2, ybuf.at[s], c * tm, sems.at[s], tm)

    @pl.when(i == 0)
    def _():
        start(0, 0)

    _wait_rows(yb_hbm, ybuf.at[slot], sems.at[slot])
    nxt = jnp.minimum(i + 1, last)
    for s in range(2):
        @pl.when(slot == s)
        def _():
            start(nxt, 1 - s)
            route = route_ref[...]
            w1 = route[:, 2:3]
            w2 = route[:, 3:4]
            h = h_ref[...] + w1 * _unpack_halves(ybuf[s, :tm, :]) + w2 * _unpack_halves(ybuf[s, tm:, :])
            o_ref[...] = h * lax.rsqrt(jnp.mean(h * h, axis=-1, keepdims=True) + RMS_EPS) * gf_ref[...]

    @pl.when(i == last)
    def _():
        _wait_rows(yb_hbm, ybuf.at[1 - slot], sems.at[1 - slot])


def _combine(pos, h, route, gf, yb, tm):
    t, d = h.shape
    return pl.pallas_call(
        functools.partial(_combine_kernel, tm=tm),
        grid_spec=pltpu.PrefetchScalarGridSpec(
            num_scalar_prefetch=1, grid=(t // tm,),
            in_specs=[pl.BlockSpec((tm, d), lambda i, p: (i, 0)), pl.BlockSpec((tm, LANES), lambda i, p: (i, 0)),
                      pl.BlockSpec((1, d), lambda i, p: (0, 0)), pl.BlockSpec(memory_space=pl.ANY)],
            out_specs=pl.BlockSpec((tm, d), lambda i, p: (i, 0)),
            scratch_shapes=[pltpu.VMEM((2, 2 * tm) + yb.shape[1:], yb.dtype), pltpu.SemaphoreType.DMA((2,))],
        ),
        out_shape=jax.ShapeDtypeStruct((t, d), F32),
        compiler_params=_cparams(("arbitrary",)), name="moe_combine",
    )(pos, h, route, gf, yb)


def _position_kernel(seg_ref, code_ref, pos_ref):
    code = code_ref[...]
    expert = lax.shift_right_logical(code, RANK_BITS)
    pos = jnp.bitwise_and(code, (1 << RANK_BITS) - 1)
    for e in range(N_EXPERTS):
        pos = pos + jnp.where(expert == e, seg_ref[e], 0)
    pos_ref[...] = pos


def _dispatch_plan(route, counts, tm):
    t = route.shape[0]
    n = 2 * t
    nb = (n + N_EXPERTS * (tm - 1)) // tm + 1
    cnt = counts[0, :N_EXPERTS].astype(jnp.int32)
    padded = ((cnt + tm - 1) // tm) * tm
    pend = jnp.cumsum(padded)
    pstart = pend - padded
    seg = jnp.concatenate([pstart, pstart + cnt, pend]).astype(jnp.int32)
    blk_start = jnp.arange(nb, dtype=jnp.int32) * tm
    blk_e = jnp.minimum(jnp.sum((blk_start[:, None] >= pend[None, :]).astype(jnp.int32), axis=1), N_EXPERTS - 1)
    n_used = (pend[-1:] // tm).astype(jnp.int32)
    code = route[:, :2].astype(jnp.int32).reshape(n // LANES, LANES)
    whole = pl.BlockSpec(code.shape, lambda i, s: (0, 0))
    pos = pl.pallas_call(
        _position_kernel,
        grid_spec=pltpu.PrefetchScalarGridSpec(num_scalar_prefetch=1, grid=(1,), in_specs=[whole], out_specs=whole),
        out_shape=jax.ShapeDtypeStruct(code.shape, jnp.int32),
        compiler_params=_cparams(("arbitrary",)), name="slot_positions",
    )(seg, code)
    return nb * tm, blk_e.astype(jnp.int32), n_used, pos.reshape(n), seg


def kernel(x, meta_tokens, norm_mix_g, w_in, ssm_lambda_re, ssm_lambda_im, ssm_log_dt, ssm_b_re, ssm_b_im,
           ssm_c_re, ssm_c_im, ssm_d, ssm_glu_w, ssm_glu_b, w_branch_attn, w_branch_ssm, w_out, norm_ffn_g,
           router_group_w, router_group_b, router_expert_w, router_expert_b, expert_w1, expert_w3, expert_w2,
           norm_final_g):
    b, l, d = x.shape
    depth = w_in.shape[0]
    aw = w_branch_attn.shape[1]
    sw = w_branch_ssm.shape[1]
    heads = aw // HEAD_DIM
    t = b * l
    pad = (-N_META) % Q_BLOCK
    assert depth == 1 and l % Q_BLOCK == 0 and pad + N_META == Q_BLOCK and SSM_CHUNK == Q_BLOCK
    layer = 0
    h_real = x.reshape(t, d)
    h_meta = meta_tokens.astype(x.dtype)
    tm_tok = 512

    wi = w_in[layer].astype(BF16)
    o = 3 * aw + sw
    ws = (wi[:, :aw], wi[:, aw:2 * aw].T, wi[:, 2 * aw:3 * aw], wi[:, 3 * aw:o].T, wi[:, o:o + d], wi[:, o + d:])
    g_mix = norm_mix_g[layer].reshape(1, d)
    q, kt, v, ut, ga, gb = _inproj(h_real, g_mix, ws, tm_tok)
    _, kt_m, v_m, ut_m, _, _ = _inproj(h_meta, g_mix, ws, N_META)
    kt_meta = jnp.concatenate([jnp.zeros((aw, pad), BF16), kt_m], axis=1)
    v_meta = jnp.concatenate([jnp.zeros((pad, aw), BF16), v_m], axis=0)
    ut_meta = jnp.concatenate([jnp.zeros((sw, pad), BF16), ut_m], axis=1)

    attn = _attention(q, kt, v, kt_meta, v_meta, b, heads, pad)

    ssm_params = (ssm_lambda_re[layer], ssm_lambda_im[layer], ssm_log_dt[layer], ssm_b_re[layer], ssm_b_im[layer],
                  ssm_c_re[layer], ssm_c_im[layer], ssm_d[layer])
    yt = _ssm(ut, ut_meta, ssm_params, b)

    wr = jnp.zeros((d, LANES), F32)
    wr = wr.at[:, :N_EXPERTS].set(router_expert_w[layer]).at[:, N_EXPERTS:N_EXPERTS + MOE_GROUPS].set(router_group_w[layer])
    br = jnp.zeros((1, LANES), F32)
    br = br.at[0, :N_EXPERTS].set(router_expert_b[layer]).at[0, N_EXPERTS:N_EXPERTS + MOE_GROUPS].set(router_group_b[layer])
    h_mid, hn, route, counts = _merge(
        h_real, attn, yt, ga, gb, ssm_glu_w[layer].T.astype(BF16), ssm_glu_b[layer].reshape(sw, 1).astype(F32),
        w_branch_attn[layer].astype(BF16), w_branch_ssm[layer].astype(BF16), w_out[layer].astype(BF16),
        norm_ffn_g[layer].reshape(1, d), wr, br, tm_tok)

    cap, blk_e, n_used, pos, seg = _dispatch_plan(route, counts, MOE_TILE)
    xs = _dispatch(hn, pos, seg, cap, 256)
    yb = _experts(xs, blk_e, n_used, expert_w1[layer], expert_w3[layer], expert_w2[layer], MOE_TILE)
    out = _combine(pos, h_mid, route, norm_final_g.reshape(1, d), yb, 256)
    return out.reshape(b, l, d)
```

```python
import functools

import jax
import jax.numpy as jnp
from jax import lax
from jax.experimental import pallas as pl
from jax.experimental.pallas import tpu as pltpu

F32 = jnp.float32
BF16 = jnp.bfloat16

N_META = 16
Q_BLOCK = 128
HEAD_DIM = 64
SSM_GROUP_CH = 16
SSM_STATE = 64
MOE_GROUPS = 4
EXPERTS_PER_GROUP = 8
N_EXPERTS = MOE_GROUPS * EXPERTS_PER_GROUP
RMS_EPS = 1e-6

LANES = 128
SUBLANES = 8
SSM_CHUNK = LANES
MOE_TILE = 512
ROW_DMA_TILE = 512
STICK_CUTOFF = 104.0
VMEM_LIMIT = 52 * 1024 * 1024

_NT = (((1,), (1,)), ((), ()))
_TN = (((0,), (0,)), ((), ()))


def _cparams(sem):
    return pltpu.CompilerParams(dimension_semantics=sem, vmem_limit_bytes=VMEM_LIMIT)


def _inproj_kernel(x_ref, g_ref, wq_ref, wkt_ref, wv_ref, wut_ref, wga_ref, wgb_ref,
                   q_ref, kt_ref, v_ref, ut_ref, ga_ref, gb_ref):
    x = x_ref[...]
    ms = jnp.mean(x * x, axis=-1, keepdims=True)
    xn = (x * lax.rsqrt(ms + RMS_EPS) * g_ref[...]).astype(BF16)
    q_ref[...] = (jnp.dot(xn, wq_ref[...], preferred_element_type=F32) * (HEAD_DIM ** -0.5)).astype(BF16)
    kt_ref[...] = lax.dot_general(wkt_ref[...], xn, _NT, preferred_element_type=F32).astype(BF16)
    v_ref[...] = jnp.dot(xn, wv_ref[...], preferred_element_type=F32).astype(BF16)
    ut_ref[...] = lax.dot_general(wut_ref[...], xn, _NT, preferred_element_type=F32).astype(BF16)
    ga_ref[...] = jax.nn.sigmoid(jnp.dot(xn, wga_ref[...], preferred_element_type=F32)).astype(BF16)
    gb_ref[...] = jax.nn.sigmoid(jnp.dot(xn, wgb_ref[...], preferred_element_type=F32)).astype(BF16)


def _inproj(x2, g, ws, tm):
    m, d = x2.shape
    aw, sw = ws[0].shape[1], ws[3].shape[0]
    row = lambda n: pl.BlockSpec((tm, n), lambda i: (i, 0))
    col = lambda n: pl.BlockSpec((n, tm), lambda i: (0, i))
    full = lambda a: pl.BlockSpec(a.shape, lambda i: (0, 0))
    sds = jax.ShapeDtypeStruct
    return pl.pallas_call(
        _inproj_kernel,
        grid=(m // tm,),
        in_specs=[row(d), full(g)] + [full(w) for w in ws],
        out_specs=[row(aw), col(aw), row(aw), col(sw), row(d), row(d)],
        out_shape=[sds((m, aw), BF16), sds((aw, m), BF16), sds((m, aw), BF16), sds((sw, m), BF16),
                   sds((m, d), BF16), sds((m, d), BF16)],
        compiler_params=_cparams(("parallel",)),
        name="inproj",
    )(x2, g, *ws)


def _attn_kernel(q_ref, kt_ref, v_ref, ktm_ref, vm_ref, tri_ref, o_ref, qm_ref, carry_ref, acc_ref, *, heads, pad):
    i = pl.program_id(1)
    pairs = heads // 2
    carry_ref[...] = jnp.zeros_like(carry_ref)
    acc_ref[...] = jnp.zeros_like(acc_ref)
    lane = lax.broadcasted_iota(jnp.int32, (Q_BLOCK, LANES), 1)
    row = lax.broadcasted_iota(jnp.int32, (Q_BLOCK, LANES), 0)
    low_half = lane < HEAD_DIM
    rows = lambda h: slice(h * Q_BLOCK, (h + 1) * Q_BLOCK)
    for p in range(pairs):
        qp = q_ref[0, :, p * LANES:(p + 1) * LANES]
        qm_ref[rows(2 * p)] = jnp.where(low_half, qp, jnp.zeros_like(qp))
        qm_ref[rows(2 * p + 1)] = jnp.where(low_half, jnp.zeros_like(qp), qp)

    def block_step(get_kt, get_v, valid):
        kts = [get_kt(p) for p in range(pairs)]
        vs = [get_v(p) for p in range(pairs)]
        zs = [jnp.dot(qm_ref[rows(h)], kts[h // 2], preferred_element_type=F32) for h in range(heads)]
        drops = []
        for h in range(heads):
            z = zs[h].astype(BF16)
            drop = jnp.maximum(z, 0.0) + jnp.log(1.0 + jnp.exp(-jnp.abs(z)))
            if valid is not None:
                drop = jnp.where(valid, drop, jnp.zeros_like(drop))
            drops.append(drop)
        sums = [jnp.dot(drops[h], tri_ref[...], preferred_element_type=F32) for h in range(heads)]
        low = None
        for h in range(heads):
            carry = carry_ref[rows(h)]
            w = jnp.exp(zs[h] - (sums[h][:, :LANES] + carry))
            if valid is not None:
                w = jnp.where(valid, w, 0.0)
            acc_ref[rows(h)] += jnp.dot(w.astype(BF16), vs[h // 2], preferred_element_type=F32)
            carry = carry + sums[h][:, LANES:]
            carry_ref[rows(h)] = carry
            low = carry if low is None else jnp.minimum(low, carry)
        return jnp.min(low)

    off_d = pl.multiple_of(i * Q_BLOCK, Q_BLOCK)
    low0 = block_step(lambda p: kt_ref[p * LANES:(p + 1) * LANES, pl.ds(off_d, Q_BLOCK)],
                      lambda p: v_ref[0, pl.ds(off_d, Q_BLOCK), p * LANES:(p + 1) * LANES],
                      lane < row)

    def cond(state):
        kb, low = state
        return jnp.logical_and(kb >= 0, low < STICK_CUTOFF)

    def body(state):
        kb, _ = state
        off = pl.multiple_of(kb * Q_BLOCK, Q_BLOCK)
        low = block_step(lambda p: kt_ref[p * LANES:(p + 1) * LANES, pl.ds(off, Q_BLOCK)],
                         lambda p: v_ref[0, pl.ds(off, Q_BLOCK), p * LANES:(p + 1) * LANES],
                         None)
        return kb - 1, low

    _, low1 = lax.while_loop(cond, body, (i - 1, low0))

    @pl.when(low1 < STICK_CUTOFF)
    def _():
        block_step(lambda p: ktm_ref[p * LANES:(p + 1) * LANES, :],
                   lambda p: vm_ref[:, p * LANES:(p + 1) * LANES],
                   lane >= pad)

    for p in range(pairs):
        even = acc_ref[2 * p * Q_BLOCK:(2 * p + 1) * Q_BLOCK]
        odd = acc_ref[(2 * p + 1) * Q_BLOCK:(2 * p + 2) * Q_BLOCK]
        o_ref[0, :, p * LANES:(p + 1) * LANES] = jnp.where(low_half, even, odd).astype(o_ref.dtype)


def _attention(q, kt, v, kt_meta, v_meta, b, heads, pad):
    t, aw = q.shape
    l = t // b
    r = lax.broadcasted_iota(jnp.int32, (LANES, 2 * LANES), 0)
    c = lax.broadcasted_iota(jnp.int32, (LANES, 2 * LANES), 1)
    tri = jnp.where(jnp.logical_or(c >= LANES, r >= c), 1.0, 0.0).astype(BF16)
    once = pl.Buffered(1)
    out = pl.pallas_call(
        functools.partial(_attn_kernel, heads=heads, pad=pad),
        grid=(b, l // Q_BLOCK),
        in_specs=[
            pl.BlockSpec((1, Q_BLOCK, aw), lambda bi, i: (bi, i, 0)),
            pl.BlockSpec((aw, l), lambda bi, i: (0, bi), pipeline_mode=once),
            pl.BlockSpec((1, l, aw), lambda bi, i: (bi, 0, 0), pipeline_mode=once),
            pl.BlockSpec((aw, Q_BLOCK), lambda bi, i: (0, 0)),
            pl.BlockSpec((Q_BLOCK, aw), lambda bi, i: (0, 0)),
            pl.BlockSpec((LANES, 2 * LANES), lambda bi, i: (0, 0)),
        ],
        out_specs=pl.BlockSpec((1, Q_BLOCK, aw), lambda bi, i: (bi, i, 0)),
        out_shape=jax.ShapeDtypeStruct((b, l, aw), BF16),
        scratch_shapes=[pltpu.VMEM((heads * Q_BLOCK, LANES), BF16), pltpu.VMEM((heads * Q_BLOCK, LANES), F32),
                        pltpu.VMEM((heads * Q_BLOCK, LANES), F32)],
        compiler_params=_cparams(("parallel", "arbitrary")),
        name="stick_attn",
    )(q.reshape(b, l, aw), kt, v.reshape(b, l, aw), kt_meta, v_meta, tri)
    return out.reshape(t, aw)


def _ssm_tables(lam_re, lam_im, log_dt, b_re, b_im, c_re, c_im, d_skip):
    hp = lax.Precision.HIGHEST
    tc = SSM_CHUNK
    g, p = lam_re.shape
    ch = b_re.shape[-1]
    dt = jnp.exp(log_dt)[:, None]
    mag = jnp.exp(lam_re * dt)
    lb_re, lb_im = mag * jnp.cos(lam_im * dt), mag * jnp.sin(lam_im * dt)
    nr, ni = lb_re - 1.0, lb_im
    den = lam_re * lam_re + lam_im * lam_im
    f_re = (nr * lam_re + ni * lam_im) / den
    f_im = (ni * lam_re - nr * lam_im) / den
    bb_re = f_re[:, :, None] * b_re - f_im[:, :, None] * b_im
    bb_im = f_re[:, :, None] * b_im + f_im[:, :, None] * b_re
    k = jnp.arange(tc + 1, dtype=F32)
    pw_mag = jnp.exp((lam_re * dt)[:, :, None] * k)
    pw_ang = (lam_im * dt)[:, :, None] * k
    pw_re, pw_im = pw_mag * jnp.cos(pw_ang), pw_mag * jnp.sin(pw_ang)
    dec_re = pw_re[:, :, tc].reshape(g // 2, 1, 2 * p)
    dec_im = pw_im[:, :, tc].reshape(g // 2, 1, 2 * p)
    ct_re, ct_im = c_re.transpose(0, 2, 1)[..., None], c_im.transpose(0, 2, 1)[..., None]
    cl_re = ct_re * pw_re[:, :, None, :] - ct_im * pw_im[:, :, None, :]
    cl_im = ct_re * pw_im[:, :, None, :] + ct_im * pw_re[:, :, None, :]
    kern = (jnp.einsum('gpot,gpi->giot', cl_re[..., :tc], bb_re, precision=hp)
            - jnp.einsum('gpot,gpi->giot', cl_im[..., :tc], bb_im, precision=hp))
    kern = kern.at[:, :, :, 0].add(jnp.eye(ch, dtype=F32)[None] * d_skip.reshape(g, ch, 1))
    powers = jnp.stack([pw_re[:, :, tc - 1::-1], pw_im[:, :, tc - 1::-1], pw_re[:, :, 1:], pw_im[:, :, 1:]], axis=1)
    coeffs = jnp.stack([bb_re, bb_im, ct_re[..., 0], ct_im[..., 0]], axis=1)
    return kern, powers, coeffs, dec_re, dec_im


def _ssm_kernel(u_ref, um_ref, k_ref, pw_ref, cf_ref, dre_ref, dim_ref, y_ref,
                toep_ref, sre_ref, sim_ref, xre_ref, xim_ref, *, batch):
    ch, tc = SSM_GROUP_CH, SSM_CHUNK
    nch = u_ref.shape[1]
    per_b = nch // batch
    p = pw_ref.shape[2]

    def chunk_rows(ref, gl):
        return jnp.concatenate([ref[gl * ch + ci] for ci in range(ch)], axis=1)

    u = [chunk_rows(u_ref, gl) for gl in range(2)]
    um = [jnp.concatenate([jnp.broadcast_to(um_ref[gl * ch + ci:gl * ch + ci + 1, :], (8, tc)) for ci in range(ch)], axis=1)
          for gl in range(2)]

    def expand(gl, which):
        a_re, a_im = pw_ref[gl, 2 * which], pw_ref[gl, 2 * which + 1]
        c_re, c_im = cf_ref[gl, 2 * which], cf_ref[gl, 2 * which + 1]
        m_re, m_im = [], []
        for c in range(ch):
            b_re = jnp.broadcast_to(c_re[:, c:c + 1], (p, tc))
            b_im = jnp.broadcast_to(c_im[:, c:c + 1], (p, tc))
            m_re.append(a_re * b_re - a_im * b_im)
            m_im.append(a_re * b_im + a_im * b_re)
        zeros = jnp.zeros((p, ch * tc), BF16)

        def place(parts):
            m = jnp.concatenate(parts, axis=1).astype(BF16)
            return jnp.concatenate([m, zeros] if gl == 0 else [zeros, m], axis=0)
        return place(m_re), place(m_im)

    to_st = [expand(gl, 0) for gl in range(2)]

    def to_state(lhs, part):
        return (lax.dot_general(lhs[0], to_st[0][part], _NT, preferred_element_type=F32)
                + lax.dot_general(lhs[1], to_st[1][part], _NT, preferred_element_type=F32))

    sre_ref[...] = to_state(u, 0)
    sim_ref[...] = to_state(u, 1)
    x0_re = to_state(um, 0)[0:1]
    x0_im = to_state(um, 1)[0:1]
    d_re, d_im = dre_ref[0], dim_ref[0]

    def step(c, xs):
        new = []
        for bi in range(batch):
            x_re, x_im = xs[2 * bi], xs[2 * bi + 1]
            r = bi * per_b + c
            xre_ref[pl.ds(r, 1), :] = x_re
            xim_ref[pl.ds(r, 1), :] = x_im
            new.append(d_re * x_re - d_im * x_im + sre_ref[pl.ds(r, 1), :])
            new.append(d_re * x_im + d_im * x_re + sim_ref[pl.ds(r, 1), :])
        return tuple(new)

    lax.fori_loop(0, per_b, step, (x0_re, x0_im) * batch)
    xs_re = xre_ref[...].astype(BF16)
    xs_im = xim_ref[...].astype(BF16)

    lane = lax.broadcasted_iota(jnp.int32, (tc, tc), 1)
    row = lax.broadcasted_iota(jnp.int32, (tc, tc), 0)
    causal = lane >= row
    for gl in range(2):
        def fill(ci, carry):
            for co in range(ch):
                taps = jnp.broadcast_to(k_ref[gl, ci, pl.ds(co, 1), :], (tc, tc))
                blk = pltpu.roll(taps, 0, 1, stride=1, stride_axis=0)
                blk = jnp.where(causal, blk, 0.0).astype(BF16)
                toep_ref[pl.ds(pl.multiple_of(ci * tc, tc), tc), co * tc:(co + 1) * tc] = blk
            return carry
        lax.fori_loop(0, ch, fill, 0)
        from_re, from_im = expand(gl, 1)
        y = jnp.dot(u[gl], toep_ref[...], preferred_element_type=F32)
        y = y + jnp.dot(xs_re, from_re, preferred_element_type=F32)
        y = y - jnp.dot(xs_im, from_im, preferred_element_type=F32)
        y = jax.nn.gelu(y).astype(y_ref.dtype)
        for co in range(ch):
            y_ref[gl * ch + co] = y[:, co * tc:(co + 1) * tc]


def _ssm(ut, ut_meta, params, batch):
    s, t = ut.shape
    tc, ch, p = SSM_CHUNK, SSM_GROUP_CH, SSM_STATE
    g = s // ch
    nch = t // tc
    kern, powers, coeffs, dec_re, dec_im = _ssm_tables(*params)
    u3 = ut.reshape(s, nch, tc)
    y3 = pl.pallas_call(
        functools.partial(_ssm_kernel, batch=batch),
        grid=(g // 2,),
        in_specs=[
            pl.BlockSpec((2 * ch, nch, tc), lambda j: (j, 0, 0)),
            pl.BlockSpec((2 * ch, tc), lambda j: (j, 0)),
            pl.BlockSpec((2, ch, ch, tc), lambda j: (j, 0, 0, 0)),
            pl.BlockSpec((2, 4, p, tc), lambda j: (j, 0, 0, 0)), pl.BlockSpec((2, 4, p, ch), lambda j: (j, 0, 0, 0)),
            pl.BlockSpec((1, 1, 2 * p), lambda j: (j, 0, 0)), pl.BlockSpec((1, 1, 2 * p), lambda j: (j, 0, 0)),
        ],
        out_specs=pl.BlockSpec((2 * ch, nch, tc), lambda j: (j, 0, 0)),
        out_shape=jax.ShapeDtypeStruct((s, nch, tc), BF16),
        scratch_shapes=[pltpu.VMEM((ch * tc, ch * tc), BF16)] + [pltpu.VMEM((nch, 2 * p), F32)] * 4,
        compiler_params=_cparams(("parallel",)), name="ssm_chunked",
    )(u3, ut_meta, kern, powers, coeffs, dec_re, dec_im)
    return y3.reshape(s, t)


RANK_BITS = 16


def _pack_halves(x):
    n = x.shape[1] // 2
    as_bits = lambda v: lax.bitcast_convert_type(v.astype(BF16).astype(F32), jnp.uint32)
    return jnp.bitwise_or(lax.shift_right_logical(as_bits(x[:, :n]), jnp.uint32(16)),
                          jnp.bitwise_and(as_bits(x[:, n:]), jnp.uint32(0xFFFF0000)))


def _unpack_halves(w):
    lo = lax.bitcast_convert_type(lax.shift_left(w, jnp.uint32(16)), F32)
    hi = lax.bitcast_convert_type(jnp.bitwise_and(w, jnp.uint32(0xFFFF0000)), F32)
    return jnp.concatenate([lo, hi], axis=1)


def _merge_kernel(x_ref, a_ref, yt_ref, ga_ref, gb_ref, gwt_ref, gbias_ref, wpa_ref, wpb_ref, wo_ref, gn_ref,
                  wr_ref, br_ref, ltri_ref, h_ref, hn_ref, route_ref, count_ref, run_ref):
    @pl.when(pl.program_id(0) == 0)
    def _():
        run_ref[...] = jnp.zeros_like(run_ref)

    yt = yt_ref[...]
    gate = jax.nn.sigmoid(jnp.dot(gwt_ref[...], yt, preferred_element_type=F32) + gbias_ref[...])
    ssm = (yt.astype(F32) * gate).T.astype(BF16)
    merged = (ga_ref[...].astype(F32) * jnp.dot(a_ref[...], wpa_ref[...], preferred_element_type=F32)
              + gb_ref[...].astype(F32) * jnp.dot(ssm, wpb_ref[...], preferred_element_type=F32))
    h = x_ref[...] + jnp.dot(merged.astype(BF16), wo_ref[...], preferred_element_type=F32)
    h_ref[...] = h
    hn = h * lax.rsqrt(jnp.mean(h * h, axis=-1, keepdims=True) + RMS_EPS) * gn_ref[...]
    hn_ref[...] = _pack_halves(hn)
    hn_hi = hn.astype(BF16)
    hn_lo = (hn - hn_hi.astype(F32)).astype(BF16)
    logits = (jnp.dot(hn_hi, wr_ref[0], preferred_element_type=F32)
              + jnp.dot(hn_hi, wr_ref[1], preferred_element_type=F32)
              + jnp.dot(hn_lo, wr_ref[0], preferred_element_type=F32)) + br_ref[...]
    lane = lax.broadcasted_iota(jnp.int32, logits.shape, 1)
    neg = jnp.full_like(logits, -jnp.inf)
    big = jnp.int32(LANES)
    is_grp = jnp.logical_and(lane >= N_EXPERTS, lane < N_EXPERTS + MOE_GROUPS)
    glog = jnp.where(is_grp, logits, neg)
    gmax = jnp.max(glog, axis=-1, keepdims=True)
    g_top = jnp.min(jnp.where(glog == gmax, lane, big), axis=-1, keepdims=True) - N_EXPERTS
    p_top = 1.0 / jnp.sum(jnp.exp(glog - gmax), axis=-1, keepdims=True)
    in_grp = (lane // EXPERTS_PER_GROUP) == g_top
    elog = jnp.where(jnp.logical_and(in_grp, lane < N_EXPERTS), logits, neg)
    e1 = jnp.max(elog, axis=-1, keepdims=True)
    i1 = jnp.min(jnp.where(elog == e1, lane, big), axis=-1, keepdims=True)
    elog2 = jnp.where(lane == i1, neg, elog)
    e2 = jnp.max(elog2, axis=-1, keepdims=True)
    i2 = jnp.min(jnp.where(elog2 == e2, lane, big), axis=-1, keepdims=True)
    t = jnp.exp(e2 - e1)
    w1 = p_top / (1.0 + t)
    w2 = p_top * t / (1.0 + t)
    hit1, hit2 = lane == i1, lane == i2
    chosen = jnp.where(jnp.logical_or(hit1, hit2), 1.0, 0.0)
    before = jnp.dot(ltri_ref[...], chosen.astype(BF16), preferred_element_type=F32) + run_ref[0:1, :]
    r1 = jnp.sum(jnp.where(hit1, before, 0.0), axis=-1, keepdims=True)
    r2 = jnp.sum(jnp.where(hit2, before, 0.0), axis=-1, keepdims=True)
    run_ref[...] = run_ref[...] + jnp.sum(chosen, axis=0, keepdims=True)
    count_ref[...] = run_ref[...]
    scale = float(2 ** RANK_BITS)
    route = jnp.where(lane == 0, i1.astype(F32) * scale + r1,
                      jnp.where(lane == 1, i2.astype(F32) * scale + r2,
                                jnp.where(lane == 2, w1, jnp.where(lane == 3, w2, 0.0))))
    route_ref[...] = route


def _merge(x2, attn, yt, ga, gb, glu_wt, glu_b, wpa, wpb, wo, gn, wr, br, tm):
    m, d = x2.shape
    aw, sw = attn.shape[1], yt.shape[0]
    row = lambda n: pl.BlockSpec((tm, n), lambda i: (i, 0))
    full = lambda a: pl.BlockSpec(a.shape, lambda i: (0,) * a.ndim)
    wr_hi = wr.astype(BF16)
    wr = jnp.stack([wr_hi, (wr - wr_hi.astype(F32)).astype(BF16)])
    ltri = jnp.where(lax.broadcasted_iota(jnp.int32, (tm, tm), 0) > lax.broadcasted_iota(jnp.int32, (tm, tm), 1),
                     1.0, 0.0).astype(BF16)
    ws = (glu_wt, glu_b, wpa, wpb, wo, gn, wr, br, ltri)
    return pl.pallas_call(
        _merge_kernel, grid=(m // tm,),
        in_specs=[row(d), row(aw), pl.BlockSpec((sw, tm), lambda i: (0, i)), row(d), row(d)] + [full(a) for a in ws],
        out_specs=[row(d), row(d // 2), row(LANES), pl.BlockSpec((8, LANES), lambda i: (0, 0))],
        out_shape=[jax.ShapeDtypeStruct((m, d), F32), jax.ShapeDtypeStruct((m, d // 2), jnp.uint32),
                   jax.ShapeDtypeStruct((m, LANES), F32), jax.ShapeDtypeStruct((8, LANES), F32)],
        scratch_shapes=[pltpu.VMEM((8, LANES), F32)],
        compiler_params=_cparams(("arbitrary",)), name="merge_router",
    )(x2, attn, yt, ga, gb, *ws)


def _start_row_gather(src_hbm, row_of, base, stride, buf, row0, sem, n):
    for r in range(n):
        row = row_of(base + stride * r)
        pltpu.make_async_copy(src_hbm.at[pl.ds(row, 1)], buf.at[pl.ds(row0 + r, 1)], sem).start(priority=r % 2)


def _wait_rows(src_hbm, buf, sem):
    pltpu.make_async_copy(src_hbm.at[pl.ds(0, buf.shape[0])], buf, sem).wait()


def _dispatch_kernel(pos_ref, seg_ref, hn_ref, xs_hbm, zero_ref, sem, zsem, *, tm):
    i = pl.program_id(0)

    @pl.when(i == 0)
    def _():
        zero_ref[...] = jnp.zeros_like(zero_ref)
        zrows = zero_ref.shape[0]

        def zero_run(lo, hi, act):
            def copy(start, n):
                return pltpu.make_async_copy(zero_ref.at[pl.ds(0, n)], xs_hbm.at[pl.ds(start, n)], zsem)
            head = jnp.minimum(jnp.bitwise_and(-lo, SUBLANES - 1), hi - lo)
            lax.fori_loop(0, head, lambda k, c: (act(copy(lo + k, 1)), c)[1], 0)
            lo = lo + head
            n_full = (hi - lo) // zrows
            lax.fori_loop(0, n_full,
                          lambda k, c: (act(copy(pl.multiple_of(lo + k * zrows, SUBLANES), zrows)), c)[1], 0)
            rest = (hi - lo) - n_full * zrows
            start = lo + n_full * zrows
            bit = zrows // 2
            while bit >= SUBLANES:
                pl.when(jnp.bitwise_and(rest, bit) != 0)(
                    functools.partial(lambda st, n: act(copy(pl.multiple_of(st, SUBLANES), n)), start, bit))
                start = start + jnp.bitwise_and(rest, bit)
                bit //= 2

        def each_gap(act):
            def run(e, carry):
                zero_run(seg_ref[N_EXPERTS + e], seg_ref[2 * N_EXPERTS + e], act)
                return carry
            lax.fori_loop(0, N_EXPERTS, run, 0)
            zero_run(seg_ref[3 * N_EXPERTS - 1], xs_hbm.shape[0], act)
        each_gap(lambda cp: cp.start())
        each_gap(lambda cp: cp.wait())

    for r in range(tm):
        for c in range(2):
            slot = pos_ref[2 * (i * tm + r) + c]
            pltpu.make_async_copy(hn_ref.at[pl.ds(r, 1)], xs_hbm.at[pl.ds(slot, 1)], sem).start(priority=c)
    pltpu.make_async_copy(xs_hbm.at[pl.ds(0, 2 * tm)], xs_hbm.at[pl.ds(0, 2 * tm)], sem).wait()


def _dispatch(hn, pos, seg, cap, tm):
    t, d = hn.shape
    return pl.pallas_call(
        functools.partial(_dispatch_kernel, tm=tm),
        grid_spec=pltpu.PrefetchScalarGridSpec(
            num_scalar_prefetch=2, grid=(t // tm,),
            in_specs=[pl.BlockSpec((tm, d), lambda i, c, s: (i, 0))],
            out_specs=pl.BlockSpec(memory_space=pl.ANY),
            scratch_shapes=[pltpu.VMEM((MOE_TILE, d), hn.dtype), pltpu.SemaphoreType.DMA(()),
                            pltpu.SemaphoreType.DMA(())],
        ),
        out_shape=jax.ShapeDtypeStruct((cap, d), hn.dtype),
        compiler_params=_cparams(("arbitrary",)), name="moe_dispatch",
    )(pos, seg, hn)


def _expert_kernel(blk_e_ref, n_used_ref, x_ref, w1_ref, w3_ref, w2_ref, y_ref):
    j = pl.program_id(0)

    @pl.when(j < n_used_ref[0])
    def _():
        x = _unpack_halves(x_ref[...]).astype(BF16)
        a = jnp.dot(x, w1_ref[0].astype(BF16), preferred_element_type=F32)
        b = jnp.dot(x, w3_ref[0].astype(BF16), preferred_element_type=F32)
        hdn = (a * jax.nn.sigmoid(a) * b).astype(BF16)
        y_ref[...] = _pack_halves(jnp.dot(hdn, w2_ref[0].astype(BF16), preferred_element_type=F32))

    @pl.when(j >= n_used_ref[0])
    def _():
        y_ref[...] = jnp.zeros_like(y_ref)


def _experts(xs, blk_e, n_used, w1, w3, w2, tm):
    cap, dp = xs.shape
    e, d, ff = w1.shape
    nb = cap // tm
    wspec = lambda r, c: pl.BlockSpec((1, r, c), lambda j, be, nu: (be[j], 0, 0))
    xrow = lambda j, be, nu: (jnp.minimum(j, nu[0] - 1), 0)
    return pl.pallas_call(
        _expert_kernel,
        grid_spec=pltpu.PrefetchScalarGridSpec(
            num_scalar_prefetch=2, grid=(nb,),
            in_specs=[pl.BlockSpec((tm, dp), xrow), wspec(d, ff), wspec(d, ff), wspec(ff, d)],
            out_specs=pl.BlockSpec((tm, dp), lambda j, be, nu: (j, 0)),
        ),
        out_shape=jax.ShapeDtypeStruct((cap, dp), xs.dtype),
        compiler_params=_cparams(("arbitrary",)), name="expert_ffn",
    )(blk_e, n_used, xs, w1, w3, w2)


def _combine_kernel(pos_ref, h_ref, route_ref, gf_ref, yb_hbm, o_ref, ybuf, sems, *, tm):
    i = pl.program_id(0)
    last = pl.num_programs(0) - 1
    slot = lax.rem(i, 2)

    def start(tile, s):
        for c in range(2):
            _start_row_gather(yb_hbm, lambda k: pos_ref[k], tile * 2 * tm + c, 2, ybuf.at[s], c * tm, sems.at[s], tm)

    @pl.when(i == 0)
    def _():
        start(0, 0)

    _wait_rows(yb_hbm, ybuf.at[slot], sems.at[slot])
    nxt = jnp.minimum(i + 1, last)
    for s in range(2):
        @pl.when(slot == s)
        def _():
            start(nxt, 1 - s)
            route = route_ref[...]
            w1 = route[:, 2:3]
            w2 = route[:, 3:4]
            h = h_ref[...] + w1 * _unpack_halves(ybuf[s, :tm, :]) + w2 * _unpack_halves(ybuf[s, tm:, :])
            o_ref[...] = h * lax.rsqrt(jnp.mean(h * h, axis=-1, keepdims=True) + RMS_EPS) * gf_ref[...]

    @pl.when(i == last)
    def _():
        _wait_rows(yb_hbm, ybuf.at[1 - slot], sems.at[1 - slot])


def _combine(pos, h, route, gf, yb, tm):
    t, d = h.shape
    return pl.pallas_call(
        functools.partial(_combine_kernel, tm=tm),
        grid_spec=pltpu.PrefetchScalarGridSpec(
            num_scalar_prefetch=1, grid=(t // tm,),
            in_specs=[pl.BlockSpec((tm, d), lambda i, p: (i, 0)), pl.BlockSpec((tm, LANES), lambda i, p: (i, 0)),
                      pl.BlockSpec((1, d), lambda i, p: (0, 0)), pl.BlockSpec(memory_space=pl.ANY)],
            out_specs=pl.BlockSpec((tm, d), lambda i, p: (i, 0)),
            scratch_shapes=[pltpu.VMEM((2, 2 * tm) + yb.shape[1:], yb.dtype), pltpu.SemaphoreType.DMA((2,))],
        ),
        out_shape=jax.ShapeDtypeStruct((t, d), F32),
        compiler_params=_cparams(("arbitrary",)), name="moe_combine",
    )(pos, h, route, gf, yb)


def _position_kernel(seg_ref, code_ref, pos_ref):
    code = code_ref[...]
    expert = lax.shift_right_logical(code, RANK_BITS)
    pos = jnp.bitwise_and(code, (1 << RANK_BITS) - 1)
    for e in range(N_EXPERTS):
        pos = pos + jnp.where(expert == e, seg_ref[e], 0)
    pos_ref[...] = pos


def _dispatch_plan(route, counts, tm):
    t = route.shape[0]
    n = 2 * t
    nb = (n + N_EXPERTS * (tm - 1)) // tm + 1
    cnt = counts[0, :N_EXPERTS].astype(jnp.int32)
    padded = ((cnt + tm - 1) // tm) * tm
    pend = jnp.cumsum(padded)
    pstart = pend - padded
    seg = jnp.concatenate([pstart, pstart + cnt, pend]).astype(jnp.int32)
    blk_start = jnp.arange(nb, dtype=jnp.int32) * tm
    blk_e = jnp.minimum(jnp.sum((blk_start[:, None] >= pend[None, :]).astype(jnp.int32), axis=1), N_EXPERTS - 1)
    n_used = (pend[-1:] // tm).astype(jnp.int32)
    code = route[:, :2].astype(jnp.int32).reshape(n // LANES, LANES)
    whole = pl.BlockSpec(code.shape, lambda i, s: (0, 0))
    pos = pl.pallas_call(
        _position_kernel,
        grid_spec=pltpu.PrefetchScalarGridSpec(num_scalar_prefetch=1, grid=(1,), in_specs=[whole], out_specs=whole),
        out_shape=jax.ShapeDtypeStruct(code.shape, jnp.int32),
        compiler_params=_cparams(("arbitrary",)), name="slot_positions",
    )(seg, code)
    return nb * tm, blk_e.astype(jnp.int32), n_used, pos.reshape(n), seg


def kernel(x, meta_tokens, norm_mix_g, w_in, ssm_lambda_re, ssm_lambda_im, ssm_log_dt, ssm_b_re, ssm_b_im,
           ssm_c_re, ssm_c_im, ssm_d, ssm_glu_w, ssm_glu_b, w_branch_attn, w_branch_ssm, w_out, norm_ffn_g,
           router_group_w, router_group_b, router_expert_w, router_expert_b, expert_w1, expert_w3, expert_w2,
           norm_final_g):
    b, l, d = x.shape
    depth = w_in.shape[0]
    aw = w_branch_attn.shape[1]
    sw = w_branch_ssm.shape[1]
    heads = aw // HEAD_DIM
    t = b * l
    pad = (-N_META) % Q_BLOCK
    assert depth == 1 and l % Q_BLOCK == 0 and pad + N_META == Q_BLOCK and SSM_CHUNK == Q_BLOCK
    layer = 0
    h_real = x.reshape(t, d)
    h_meta = meta_tokens.astype(x.dtype)
    tm_tok = 512

    wi = w_in[layer].astype(BF16)
    o = 3 * aw + sw
    ws = (wi[:, :aw], wi[:, aw:2 * aw].T, wi[:, 2 * aw:3 * aw], wi[:, 3 * aw:o].T, wi[:, o:o + d], wi[:, o + d:])
    g_mix = norm_mix_g[layer].reshape(1, d)
    q, kt, v, ut, ga, gb = _inproj(h_real, g_mix, ws, tm_tok)
    _, kt_m, v_m, ut_m, _, _ = _inproj(h_meta, g_mix, ws, N_META)
    kt_meta = jnp.concatenate([jnp.zeros((aw, pad), BF16), kt_m], axis=1)
    v_meta = jnp.concatenate([jnp.zeros((pad, aw), BF16), v_m], axis=0)
    ut_meta = jnp.concatenate([jnp.zeros((sw, pad), BF16), ut_m], axis=1)

    attn = _attention(q, kt, v, kt_meta, v_meta, b, heads, pad)

    ssm_params = (ssm_lambda_re[layer], ssm_lambda_im[layer], ssm_log_dt[layer], ssm_b_re[layer], ssm_b_im[layer],
                  ssm_c_re[layer], ssm_c_im[layer], ssm_d[layer])
    yt = _ssm(ut, ut_meta, ssm_params, b)

    wr = jnp.zeros((d, LANES), F32)
    wr = wr.at[:, :N_EXPERTS].set(router_expert_w[layer]).at[:, N_EXPERTS:N_EXPERTS + MOE_GROUPS].set(router_group_w[layer])
    br = jnp.zeros((1, LANES), F32)
    br = br.at[0, :N_EXPERTS].set(router_expert_b[layer]).at[0, N_EXPERTS:N_EXPERTS + MOE_GROUPS].set(router_group_b[layer])
    h_mid, hn, route, counts = _merge(
        h_real, attn, yt, ga, gb, ssm_glu_w[layer].T.astype(BF16), ssm_glu_b[layer].reshape(sw, 1).astype(F32),
        w_branch_attn[layer].astype(BF16), w_branch_ssm[layer].astype(BF16), w_out[layer].astype(BF16),
        norm_ffn_g[layer].reshape(1, d), wr, br, tm_tok)

    cap, blk_e, n_used, pos, seg = _dispatch_plan(route, counts, MOE_TILE)
    xs = _dispatch(hn, pos, seg, cap, ROW_DMA_TILE)
    yb = _experts(xs, blk_e, n_used, expert_w1[layer], expert_w3[layer], expert_w2[layer], MOE_TILE)
    out = _combine(pos, h_mid, route, norm_final_g.reshape(1, d), yb, ROW_DMA_TILE)
    return out.reshape(b, l, d)
```

```python
import functools

import jax
import jax.numpy as jnp
from jax import lax
from jax.experimental import pallas as pl
from jax.experimental.pallas import tpu as pltpu

F32 = jnp.float32
BF16 = jnp.bfloat16

N_META = 16
Q_BLOCK = 128
HEAD_DIM = 64
SSM_GROUP_CH = 16
SSM_STATE = 64
MOE_GROUPS = 4
EXPERTS_PER_GROUP = 8
N_EXPERTS = MOE_GROUPS * EXPERTS_PER_GROUP
RMS_EPS = 1e-6

LANES = 128
SUBLANES = 8
SSM_CHUNK = LANES
MOE_TILE = 512
ROW_DMA_TILE = 1024
STICK_CUTOFF = 104.0
VMEM_LIMIT = 52 * 1024 * 1024

_NT = (((1,), (1,)), ((), ()))
_TN = (((0,), (0,)), ((), ()))


def _cparams(sem):
    return pltpu.CompilerParams(dimension_semantics=sem, vmem_limit_bytes=VMEM_LIMIT)


def _inproj_kernel(x_ref, g_ref, wq_ref, wkt_ref, wv_ref, wut_ref, wga_ref, wgb_ref,
                   q_ref, kt_ref, v_ref, ut_ref, ga_ref, gb_ref):
    x = x_ref[...]
    ms = jnp.mean(x * x, axis=-1, keepdims=True)
    xn = (x * lax.rsqrt(ms + RMS_EPS) * g_ref[...]).astype(BF16)
    q_ref[...] = (jnp.dot(xn, wq_ref[...], preferred_element_type=F32) * (HEAD_DIM ** -0.5)).astype(BF16)
    kt_ref[...] = lax.dot_general(wkt_ref[...], xn, _NT, preferred_element_type=F32).astype(BF16)
    v_ref[...] = jnp.dot(xn, wv_ref[...], preferred_element_type=F32).astype(BF16)
    ut_ref[...] = lax.dot_general(wut_ref[...], xn, _NT, preferred_element_type=F32).astype(BF16)
    ga_ref[...] = jax.nn.sigmoid(jnp.dot(xn, wga_ref[...], preferred_element_type=F32)).astype(BF16)
    gb_ref[...] = jax.nn.sigmoid(jnp.dot(xn, wgb_ref[...], preferred_element_type=F32)).astype(BF16)


def _inproj(x2, g, ws, tm):
    m, d = x2.shape
    aw, sw = ws[0].shape[1], ws[3].shape[0]
    row = lambda n: pl.BlockSpec((tm, n), lambda i: (i, 0))
    col = lambda n: pl.BlockSpec((n, tm), lambda i: (0, i))
    full = lambda a: pl.BlockSpec(a.shape, lambda i: (0, 0))
    sds = jax.ShapeDtypeStruct
    return pl.pallas_call(
        _inproj_kernel,
        grid=(m // tm,),
        in_specs=[row(d), full(g)] + [full(w) for w in ws],
        out_specs=[row(aw), col(aw), row(aw), col(sw), row(d), row(d)],
        out_shape=[sds((m, aw), BF16), sds((aw, m), BF16), sds((m, aw), BF16), sds((sw, m), BF16),
                   sds((m, d), BF16), sds((m, d), BF16)],
        compiler_params=_cparams(("parallel",)),
        name="inproj",
    )(x2, g, *ws)


def _attn_kernel(q_ref, kt_ref, v_ref, ktm_ref, vm_ref, tri_ref, o_ref, qm_ref, carry_ref, acc_ref, *, heads, pad):
    i = pl.program_id(1)
    pairs = heads // 2
    carry_ref[...] = jnp.zeros_like(carry_ref)
    acc_ref[...] = jnp.zeros_like(acc_ref)
    lane = lax.broadcasted_iota(jnp.int32, (Q_BLOCK, LANES), 1)
    row = lax.broadcasted_iota(jnp.int32, (Q_BLOCK, LANES), 0)
    low_half = lane < HEAD_DIM
    rows = lambda h: slice(h * Q_BLOCK, (h + 1) * Q_BLOCK)
    for p in range(pairs):
        qp = q_ref[0, :, p * LANES:(p + 1) * LANES]
        qm_ref[rows(2 * p)] = jnp.where(low_half, qp, jnp.zeros_like(qp))
        qm_ref[rows(2 * p + 1)] = jnp.where(low_half, jnp.zeros_like(qp), qp)

    def block_step(get_kt, get_v, valid):
        kts = [get_kt(p) for p in range(pairs)]
        vs = [get_v(p) for p in range(pairs)]
        zs = [jnp.dot(qm_ref[rows(h)], kts[h // 2], preferred_element_type=F32) for h in range(heads)]
        drops = []
        for h in range(heads):
            z = zs[h].astype(BF16)
            drop = jnp.maximum(z, 0.0) + jnp.log(1.0 + jnp.exp(-jnp.abs(z)))
            if valid is not None:
                drop = jnp.where(valid, drop, jnp.zeros_like(drop))
            drops.append(drop)
        sums = [jnp.dot(drops[h], tri_ref[...], preferred_element_type=F32) for h in range(heads)]
        low = None
        for h in range(heads):
            carry = carry_ref[rows(h)]
            w = jnp.exp(zs[h] - (sums[h][:, :LANES] + carry))
            if valid is not None:
                w = jnp.where(valid, w, 0.0)
            acc_ref[rows(h)] += jnp.dot(w.astype(BF16), vs[h // 2], preferred_element_type=F32)
            carry = carry + sums[h][:, LANES:]
            carry_ref[rows(h)] = carry
            low = carry if low is None else jnp.minimum(low, carry)
        return jnp.min(low)

    off_d = pl.multiple_of(i * Q_BLOCK, Q_BLOCK)
    low0 = block_step(lambda p: kt_ref[p * LANES:(p + 1) * LANES, pl.ds(off_d, Q_BLOCK)],
                      lambda p: v_ref[0, pl.ds(off_d, Q_BLOCK), p * LANES:(p + 1) * LANES],
                      lane < row)

    def cond(state):
        kb, low = state
        return jnp.logical_and(kb >= 0, low < STICK_CUTOFF)

    def body(state):
        kb, _ = state
        off = pl.multiple_of(kb * Q_BLOCK, Q_BLOCK)
        low = block_step(lambda p: kt_ref[p * LANES:(p + 1) * LANES, pl.ds(off, Q_BLOCK)],
                         lambda p: v_ref[0, pl.ds(off, Q_BLOCK), p * LANES:(p + 1) * LANES],
                         None)
        return kb - 1, low

    _, low1 = lax.while_loop(cond, body, (i - 1, low0))

    @pl.when(low1 < STICK_CUTOFF)
    def _():
        block_step(lambda p: ktm_ref[p * LANES:(p + 1) * LANES, :],
                   lambda p: vm_ref[:, p * LANES:(p + 1) * LANES],
                   lane >= pad)

    for p in range(pairs):
        even = acc_ref[2 * p * Q_BLOCK:(2 * p + 1) * Q_BLOCK]
        odd = acc_ref[(2 * p + 1) * Q_BLOCK:(2 * p + 2) * Q_BLOCK]
        o_ref[0, :, p * LANES:(p + 1) * LANES] = jnp.where(low_half, even, odd).astype(o_ref.dtype)


def _attention(q, kt, v, kt_meta, v_meta, b, heads, pad):
    t, aw = q.shape
    l = t // b
    r = lax.broadcasted_iota(jnp.int32, (LANES, 2 * LANES), 0)
    c = lax.broadcasted_iota(jnp.int32, (LANES, 2 * LANES), 1)
    tri = jnp.where(jnp.logical_or(c >= LANES, r >= c), 1.0, 0.0).astype(BF16)
    once = pl.Buffered(1)
    out = pl.pallas_call(
        functools.partial(_attn_kernel, heads=heads, pad=pad),
        grid=(b, l // Q_BLOCK),
        in_specs=[
            pl.BlockSpec((1, Q_BLOCK, aw), lambda bi, i: (bi, i, 0)),
            pl.BlockSpec((aw, l), lambda bi, i: (0, bi), pipeline_mode=once),
            pl.BlockSpec((1, l, aw), lambda bi, i: (bi, 0, 0), pipeline_mode=once),
            pl.BlockSpec((aw, Q_BLOCK), lambda bi, i: (0, 0)),
            pl.BlockSpec((Q_BLOCK, aw), lambda bi, i: (0, 0)),
            pl.BlockSpec((LANES, 2 * LANES), lambda bi, i: (0, 0)),
        ],
        out_specs=pl.BlockSpec((1, Q_BLOCK, aw), lambda bi, i: (bi, i, 0)),
        out_shape=jax.ShapeDtypeStruct((b, l, aw), BF16),
        scratch_shapes=[pltpu.VMEM((heads * Q_BLOCK, LANES), BF16), pltpu.VMEM((heads * Q_BLOCK, LANES), F32),
                        pltpu.VMEM((heads * Q_BLOCK, LANES), F32)],
        compiler_params=_cparams(("parallel", "arbitrary")),
        name="stick_attn",
    )(q.reshape(b, l, aw), kt, v.reshape(b, l, aw), kt_meta, v_meta, tri)
    return out.reshape(t, aw)


def _ssm_tables(lam_re, lam_im, log_dt, b_re, b_im, c_re, c_im, d_skip):
    hp = lax.Precision.HIGHEST
    tc = SSM_CHUNK
    g, p = lam_re.shape
    ch = b_re.shape[-1]
    dt = jnp.exp(log_dt)[:, None]
    mag = jnp.exp(lam_re * dt)
    lb_re, lb_im = mag * jnp.cos(lam_im * dt), mag * jnp.sin(lam_im * dt)
    nr, ni = lb_re - 1.0, lb_im
    den = lam_re * lam_re + lam_im * lam_im
    f_re = (nr * lam_re + ni * lam_im) / den
    f_im = (ni * lam_re - nr * lam_im) / den
    bb_re = f_re[:, :, None] * b_re - f_im[:, :, None] * b_im
    bb_im = f_re[:, :, None] * b_im + f_im[:, :, None] * b_re
    k = jnp.arange(tc + 1, dtype=F32)
    pw_mag = jnp.exp((lam_re * dt)[:, :, None] * k)
    pw_ang = (lam_im * dt)[:, :, None] * k
    pw_re, pw_im = pw_mag * jnp.cos(pw_ang), pw_mag * jnp.sin(pw_ang)
    dec_re = pw_re[:, :, tc].reshape(g // 2, 1, 2 * p)
    dec_im = pw_im[:, :, tc].reshape(g // 2, 1, 2 * p)
    ct_re, ct_im = c_re.transpose(0, 2, 1)[..., None], c_im.transpose(0, 2, 1)[..., None]
    cl_re = ct_re * pw_re[:, :, None, :] - ct_im * pw_im[:, :, None, :]
    cl_im = ct_re * pw_im[:, :, None, :] + ct_im * pw_re[:, :, None, :]
    kern = (jnp.einsum('gpot,gpi->giot', cl_re[..., :tc], bb_re, precision=hp)
            - jnp.einsum('gpot,gpi->giot', cl_im[..., :tc], bb_im, precision=hp))
    kern = kern.at[:, :, :, 0].add(jnp.eye(ch, dtype=F32)[None] * d_skip.reshape(g, ch, 1))
    powers = jnp.stack([pw_re[:, :, tc - 1::-1], pw_im[:, :, tc - 1::-1], pw_re[:, :, 1:], pw_im[:, :, 1:]], axis=1)
    coeffs = jnp.stack([bb_re, bb_im, ct_re[..., 0], ct_im[..., 0]], axis=1)
    return kern, powers, coeffs, dec_re, dec_im


def _ssm_kernel(u_ref, um_ref, k_ref, pw_ref, cf_ref, dre_ref, dim_ref, y_ref,
                toep_ref, sre_ref, sim_ref, xre_ref, xim_ref, *, batch):
    ch, tc = SSM_GROUP_CH, SSM_CHUNK
    nch = u_ref.shape[1]
    per_b = nch // batch
    p = pw_ref.shape[2]

    def chunk_rows(ref, gl):
        return jnp.concatenate([ref[gl * ch + ci] for ci in range(ch)], axis=1)

    u = [chunk_rows(u_ref, gl) for gl in range(2)]
    um = [jnp.concatenate([jnp.broadcast_to(um_ref[gl * ch + ci:gl * ch + ci + 1, :], (8, tc)) for ci in range(ch)], axis=1)
          for gl in range(2)]

    def expand(gl, which):
        a_re, a_im = pw_ref[gl, 2 * which], pw_ref[gl, 2 * which + 1]
        c_re, c_im = cf_ref[gl, 2 * which], cf_ref[gl, 2 * which + 1]
        m_re, m_im = [], []
        for c in range(ch):
            b_re = jnp.broadcast_to(c_re[:, c:c + 1], (p, tc))
            b_im = jnp.broadcast_to(c_im[:, c:c + 1], (p, tc))
            m_re.append(a_re * b_re - a_im * b_im)
            m_im.append(a_re * b_im + a_im * b_re)
        zeros = jnp.zeros((p, ch * tc), BF16)

        def place(parts):
            m = jnp.concatenate(parts, axis=1).astype(BF16)
            return jnp.concatenate([m, zeros] if gl == 0 else [zeros, m], axis=0)
        return place(m_re), place(m_im)

    to_st = [expand(gl, 0) for gl in range(2)]

    def to_state(lhs, part):
        return (lax.dot_general(lhs[0], to_st[0][part], _NT, preferred_element_type=F32)
                + lax.dot_general(lhs[1], to_st[1][part], _NT, preferred_element_type=F32))

    sre_ref[...] = to_state(u, 0)
    sim_ref[...] = to_state(u, 1)
    x0_re = to_state(um, 0)[0:1]
    x0_im = to_state(um, 1)[0:1]
    d_re, d_im = dre_ref[0], dim_ref[0]

    def step(c, xs):
        new = []
        for bi in range(batch):
            x_re, x_im = xs[2 * bi], xs[2 * bi + 1]
            r = bi * per_b + c
            xre_ref[pl.ds(r, 1), :] = x_re
            xim_ref[pl.ds(r, 1), :] = x_im
            new.append(d_re * x_re - d_im * x_im + sre_ref[pl.ds(r, 1), :])
            new.append(d_re * x_im + d_im * x_re + sim_ref[pl.ds(r, 1), :])
        return tuple(new)

    lax.fori_loop(0, per_b, step, (x0_re, x0_im) * batch)
    xs_re = xre_ref[...].astype(BF16)
    xs_im = xim_ref[...].astype(BF16)

    lane = lax.broadcasted_iota(jnp.int32, (tc, tc), 1)
    row = lax.broadcasted_iota(jnp.int32, (tc, tc), 0)
    causal = lane >= row
    for gl in range(2):
        def fill(ci, carry):
            for co in range(ch):
                taps = jnp.broadcast_to(k_ref[gl, ci, pl.ds(co, 1), :], (tc, tc))
                blk = pltpu.roll(taps, 0, 1, stride=1, stride_axis=0)
                blk = jnp.where(causal, blk, 0.0).astype(BF16)
                toep_ref[pl.ds(pl.multiple_of(ci * tc, tc), tc), co * tc:(co + 1) * tc] = blk
            return carry
        lax.fori_loop(0, ch, fill, 0)
        from_re, from_im = expand(gl, 1)
        y = jnp.dot(u[gl], toep_ref[...], preferred_element_type=F32)
        y = y + jnp.dot(xs_re, from_re, preferred_element_type=F32)
        y = y - jnp.dot(xs_im, from_im, preferred_element_type=F32)
        y = jax.nn.gelu(y).astype(y_ref.dtype)
        for co in range(ch):
            y_ref[gl * ch + co] = y[:, co * tc:(co + 1) * tc]


def _ssm(ut, ut_meta, params, batch):
    s, t = ut.shape
    tc, ch, p = SSM_CHUNK, SSM_GROUP_CH, SSM_STATE
    g = s // ch
    nch = t // tc
    kern, powers, coeffs, dec_re, dec_im = _ssm_tables(*params)
    u3 = ut.reshape(s, nch, tc)
    y3 = pl.pallas_call(
        functools.partial(_ssm_kernel, batch=batch),
        grid=(g // 2,),
        in_specs=[
            pl.BlockSpec((2 * ch, nch, tc), lambda j: (j, 0, 0)),
            pl.BlockSpec((2 * ch, tc), lambda j: (j, 0)),
            pl.BlockSpec((2, ch, ch, tc), lambda j: (j, 0, 0, 0)),
            pl.BlockSpec((2, 4, p, tc), lambda j: (j, 0, 0, 0)), pl.BlockSpec((2, 4, p, ch), lambda j: (j, 0, 0, 0)),
            pl.BlockSpec((1, 1, 2 * p), lambda j: (j, 0, 0)), pl.BlockSpec((1, 1, 2 * p), lambda j: (j, 0, 0)),
        ],
        out_specs=pl.BlockSpec((2 * ch, nch, tc), lambda j: (j, 0, 0)),
        out_shape=jax.ShapeDtypeStruct((s, nch, tc), BF16),
        scratch_shapes=[pltpu.VMEM((ch * tc, ch * tc), BF16)] + [pltpu.VMEM((nch, 2 * p), F32)] * 4,
        compiler_params=_cparams(("parallel",)), name="ssm_chunked",
    )(u3, ut_meta, kern, powers, coeffs, dec_re, dec_im)
    return y3.reshape(s, t)


RANK_BITS = 16


def _pack_halves(x):
    n = x.shape[1] // 2
    as_bits = lambda v: lax.bitcast_convert_type(v.astype(BF16).astype(F32), jnp.uint32)
    return jnp.bitwise_or(lax.shift_right_logical(as_bits(x[:, :n]), jnp.uint32(16)),
                          jnp.bitwise_and(as_bits(x[:, n:]), jnp.uint32(0xFFFF0000)))


def _unpack_halves(w):
    lo = lax.bitcast_convert_type(lax.shift_left(w, jnp.uint32(16)), F32)
    hi = lax.bitcast_convert_type(jnp.bitwise_and(w, jnp.uint32(0xFFFF0000)), F32)
    return jnp.concatenate([lo, hi], axis=1)


def _merge_kernel(x_ref, a_ref, yt_ref, ga_ref, gb_ref, gwt_ref, gbias_ref, wpa_ref, wpb_ref, wo_ref, gn_ref,
                  wr_ref, br_ref, ltri_ref, h_ref, hn_ref, route_ref, count_ref, run_ref):
    @pl.when(pl.program_id(0) == 0)
    def _():
        run_ref[...] = jnp.zeros_like(run_ref)

    yt = yt_ref[...]
    gate = jax.nn.sigmoid(jnp.dot(gwt_ref[...], yt, preferred_element_type=F32) + gbias_ref[...])
    ssm = (yt.astype(F32) * gate).T.astype(BF16)
    merged = (ga_ref[...].astype(F32) * jnp.dot(a_ref[...], wpa_ref[...], preferred_element_type=F32)
              + gb_ref[...].astype(F32) * jnp.dot(ssm, wpb_ref[...], preferred_element_type=F32))
    h = x_ref[...] + jnp.dot(merged.astype(BF16), wo_ref[...], preferred_element_type=F32)
    h_ref[...] = h
    hn = h * lax.rsqrt(jnp.mean(h * h, axis=-1, keepdims=True) + RMS_EPS) * gn_ref[...]
    hn_ref[...] = _pack_halves(hn)
    hn_hi = hn.astype(BF16)
    hn_lo = (hn - hn_hi.astype(F32)).astype(BF16)
    logits = (jnp.dot(hn_hi, wr_ref[0], preferred_element_type=F32)
              + jnp.dot(hn_hi, wr_ref[1], preferred_element_type=F32)
              + jnp.dot(hn_lo, wr_ref[0], preferred_element_type=F32)) + br_ref[...]
    lane = lax.broadcasted_iota(jnp.int32, logits.shape, 1)
    neg = jnp.full_like(logits, -jnp.inf)
    big = jnp.int32(LANES)
    is_grp = jnp.logical_and(lane >= N_EXPERTS, lane < N_EXPERTS + MOE_GROUPS)
    glog = jnp.where(is_grp, logits, neg)
    gmax = jnp.max(glog, axis=-1, keepdims=True)
    g_top = jnp.min(jnp.where(glog == gmax, lane, big), axis=-1, keepdims=True) - N_EXPERTS
    p_top = 1.0 / jnp.sum(jnp.exp(glog - gmax), axis=-1, keepdims=True)
    in_grp = (lane // EXPERTS_PER_GROUP) == g_top
    elog = jnp.where(jnp.logical_and(in_grp, lane < N_EXPERTS), logits, neg)
    e1 = jnp.max(elog, axis=-1, keepdims=True)
    i1 = jnp.min(jnp.where(elog == e1, lane, big), axis=-1, keepdims=True)
    elog2 = jnp.where(lane == i1, neg, elog)
    e2 = jnp.max(elog2, axis=-1, keepdims=True)
    i2 = jnp.min(jnp.where(elog2 == e2, lane, big), axis=-1, keepdims=True)
    t = jnp.exp(e2 - e1)
    w1 = p_top / (1.0 + t)
    w2 = p_top * t / (1.0 + t)
    hit1, hit2 = lane == i1, lane == i2
    chosen = jnp.where(jnp.logical_or(hit1, hit2), 1.0, 0.0)
    before = jnp.dot(ltri_ref[...], chosen.astype(BF16), preferred_element_type=F32) + run_ref[0:1, :]
    r1 = jnp.sum(jnp.where(hit1, before, 0.0), axis=-1, keepdims=True)
    r2 = jnp.sum(jnp.where(hit2, before, 0.0), axis=-1, keepdims=True)
    run_ref[...] = run_ref[...] + jnp.sum(chosen, axis=0, keepdims=True)
    count_ref[...] = run_ref[...]
    scale = float(2 ** RANK_BITS)
    route = jnp.where(lane == 0, i1.astype(F32) * scale + r1,
                      jnp.where(lane == 1, i2.astype(F32) * scale + r2,
                                jnp.where(lane == 2, w1, jnp.where(lane == 3, w2, 0.0))))
    route_ref[...] = route


def _merge(x2, attn, yt, ga, gb, glu_wt, glu_b, wpa, wpb, wo, gn, wr, br, tm):
    m, d = x2.shape
    aw, sw = attn.shape[1], yt.shape[0]
    row = lambda n: pl.BlockSpec((tm, n), lambda i: (i, 0))
    full = lambda a: pl.BlockSpec(a.shape, lambda i: (0,) * a.ndim)
    wr_hi = wr.astype(BF16)
    wr = jnp.stack([wr_hi, (wr - wr_hi.astype(F32)).astype(BF16)])
    ltri = jnp.where(lax.broadcasted_iota(jnp.int32, (tm, tm), 0) > lax.broadcasted_iota(jnp.int32, (tm, tm), 1),
                     1.0, 0.0).astype(BF16)
    ws = (glu_wt, glu_b, wpa, wpb, wo, gn, wr, br, ltri)
    return pl.pallas_call(
        _merge_kernel, grid=(m // tm,),
        in_specs=[row(d), row(aw), pl.BlockSpec((sw, tm), lambda i: (0, i)), row(d), row(d)] + [full(a) for a in ws],
        out_specs=[row(d), row(d // 2), row(LANES), pl.BlockSpec((8, LANES), lambda i: (0, 0))],
        out_shape=[jax.ShapeDtypeStruct((m, d), F32), jax.ShapeDtypeStruct((m, d // 2), jnp.uint32),
                   jax.ShapeDtypeStruct((m, LANES), F32), jax.ShapeDtypeStruct((8, LANES), F32)],
        scratch_shapes=[pltpu.VMEM((8, LANES), F32)],
        compiler_params=_cparams(("arbitrary",)), name="merge_router",
    )(x2, attn, yt, ga, gb, *ws)


def _start_row_gather(src_hbm, row_of, base, stride, buf, row0, sem, n):
    for r in range(n):
        row = row_of(base + stride * r)
        pltpu.make_async_copy(src_hbm.at[pl.ds(row, 1)], buf.at[pl.ds(row0 + r, 1)], sem).start(priority=r % 2)


def _wait_rows(src_hbm, buf, sem):
    pltpu.make_async_copy(src_hbm.at[pl.ds(0, buf.shape[0])], buf, sem).wait()


def _dispatch_kernel(pos_ref, seg_ref, hn_ref, xs_hbm, zero_ref, sem, zsem, *, tm):
    i = pl.program_id(0)

    @pl.when(i == 0)
    def _():
        zero_ref[...] = jnp.zeros_like(zero_ref)
        zrows = zero_ref.shape[0]

        def zero_run(lo, hi, act):
            def copy(start, n):
                return pltpu.make_async_copy(zero_ref.at[pl.ds(0, n)], xs_hbm.at[pl.ds(start, n)], zsem)
            head = jnp.minimum(jnp.bitwise_and(-lo, SUBLANES - 1), hi - lo)
            lax.fori_loop(0, head, lambda k, c: (act(copy(lo + k, 1)), c)[1], 0)
            lo = lo + head
            n_full = (hi - lo) // zrows
            lax.fori_loop(0, n_full,
                          lambda k, c: (act(copy(pl.multiple_of(lo + k * zrows, SUBLANES), zrows)), c)[1], 0)
            rest = (hi - lo) - n_full * zrows
            start = lo + n_full * zrows
            bit = zrows // 2
            while bit >= SUBLANES:
                pl.when(jnp.bitwise_and(rest, bit) != 0)(
                    functools.partial(lambda st, n: act(copy(pl.multiple_of(st, SUBLANES), n)), start, bit))
                start = start + jnp.bitwise_and(rest, bit)
                bit //= 2

        def each_gap(act):
            def run(e, carry):
                zero_run(seg_ref[N_EXPERTS + e], seg_ref[2 * N_EXPERTS + e], act)
                return carry
            lax.fori_loop(0, N_EXPERTS, run, 0)
            zero_run(seg_ref[3 * N_EXPERTS - 1], xs_hbm.shape[0], act)
        each_gap(lambda cp: cp.start())
        each_gap(lambda cp: cp.wait())

    for r in range(tm):
        for c in range(2):
            slot = pos_ref[2 * (i * tm + r) + c]
            pltpu.make_async_copy(hn_ref.at[pl.ds(r, 1)], xs_hbm.at[pl.ds(slot, 1)], sem).start(priority=c)
    pltpu.make_async_copy(xs_hbm.at[pl.ds(0, 2 * tm)], xs_hbm.at[pl.ds(0, 2 * tm)], sem).wait()


def _dispatch(hn, pos, seg, cap, tm):
    t, d = hn.shape
    return pl.pallas_call(
        functools.partial(_dispatch_kernel, tm=tm),
        grid_spec=pltpu.PrefetchScalarGridSpec(
            num_scalar_prefetch=2, grid=(t // tm,),
            in_specs=[pl.BlockSpec((tm, d), lambda i, c, s: (i, 0))],
            out_specs=pl.BlockSpec(memory_space=pl.ANY),
            scratch_shapes=[pltpu.VMEM((MOE_TILE, d), hn.dtype), pltpu.SemaphoreType.DMA(()),
                            pltpu.SemaphoreType.DMA(())],
        ),
        out_shape=jax.ShapeDtypeStruct((cap, d), hn.dtype),
        compiler_params=_cparams(("arbitrary",)), name="moe_dispatch",
    )(pos, seg, hn)


def _expert_kernel(blk_e_ref, n_used_ref, x_ref, w1_ref, w3_ref, w2_ref, y_ref):
    j = pl.program_id(0)

    @pl.when(j < n_used_ref[0])
    def _():
        x = _unpack_halves(x_ref[...]).astype(BF16)
        a = jnp.dot(x, w1_ref[0].astype(BF16), preferred_element_type=F32)
        b = jnp.dot(x, w3_ref[0].astype(BF16), preferred_element_type=F32)
        hdn = (a * jax.nn.sigmoid(a) * b).astype(BF16)
        y_ref[...] = _pack_halves(jnp.dot(hdn, w2_ref[0].astype(BF16), preferred_element_type=F32))

    @pl.when(j >= n_used_ref[0])
    def _():
        y_ref[...] = jnp.zeros_like(y_ref)


def _experts(xs, blk_e, n_used, w1, w3, w2, tm):
    cap, dp = xs.shape
    e, d, ff = w1.shape
    nb = cap // tm
    wspec = lambda r, c: pl.BlockSpec((1, r, c), lambda j, be, nu: (be[j], 0, 0))
    xrow = lambda j, be, nu: (jnp.minimum(j, nu[0] - 1), 0)
    return pl.pallas_call(
        _expert_kernel,
        grid_spec=pltpu.PrefetchScalarGridSpec(
            num_scalar_prefetch=2, grid=(nb,),
            in_specs=[pl.BlockSpec((tm, dp), xrow), wspec(d, ff), wspec(d, ff), wspec(ff, d)],
            out_specs=pl.BlockSpec((tm, dp), lambda j, be, nu: (j, 0)),
        ),
        out_shape=jax.ShapeDtypeStruct((cap, dp), xs.dtype),
        compiler_params=_cparams(("arbitrary",)), name="expert_ffn",
    )(blk_e, n_used, xs, w1, w3, w2)


def _combine_kernel(pos_ref, h_ref, route_ref, gf_ref, yb_hbm, o_ref, ybuf, sems, *, tm):
    i = pl.program_id(0)
    last = pl.num_programs(0) - 1
    slot = lax.rem(i, 2)

    def start(tile, s):
        for c in range(2):
            _start_row_gather(yb_hbm, lambda k: pos_ref[k], tile * 2 * tm + c, 2, ybuf.at[s], c * tm, sems.at[s], tm)

    @pl.when(i == 0)
    def _():
        start(0, 0)

    _wait_rows(yb_hbm, ybuf.at[slot], sems.at[slot])
    nxt = jnp.minimum(i + 1, last)
    for s in range(2):
        @pl.when(slot == s)
        def _():
            start(nxt, 1 - s)
            route = route_ref[...]
            w1 = route[:, 2:3]
            w2 = route[:, 3:4]
            h = h_ref[...] + w1 * _unpack_halves(ybuf[s, :tm, :]) + w2 * _unpack_halves(ybuf[s, tm:, :])
            o_ref[...] = h * lax.rsqrt(jnp.mean(h * h, axis=-1, keepdims=True) + RMS_EPS) * gf_ref[...]

    @pl.when(i == last)
    def _():
        _wait_rows(yb_hbm, ybuf.at[1 - slot], sems.at[1 - slot])


def _combine(pos, h, route, gf, yb, tm):
    t, d = h.shape
    return pl.pallas_call(
        functools.partial(_combine_kernel, tm=tm),
        grid_spec=pltpu.PrefetchScalarGridSpec(
            num_scalar_prefetch=1, grid=(t // tm,),
            in_specs=[pl.BlockSpec((tm, d), lambda i, p: (i, 0)), pl.BlockSpec((tm, LANES), lambda i, p: (i, 0)),
                      pl.BlockSpec((1, d), lambda i, p: (0, 0)), pl.BlockSpec(memory_space=pl.ANY)],
            out_specs=pl.BlockSpec((tm, d), lambda i, p: (i, 0)),
            scratch_shapes=[pltpu.VMEM((2, 2 * tm) + yb.shape[1:], yb.dtype), pltpu.SemaphoreType.DMA((2,))],
        ),
        out_shape=jax.ShapeDtypeStruct((t, d), F32),
        compiler_params=_cparams(("arbitrary",)), name="moe_combine",
    )(pos, h, route, gf, yb)


def _position_kernel(seg_ref, code_ref, pos_ref):
    code = code_ref[...]
    expert = lax.shift_right_logical(code, RANK_BITS)
    pos = jnp.bitwise_and(code, (1 << RANK_BITS) - 1)
    for e in range(N_EXPERTS):
        pos = pos + jnp.where(expert == e, seg_ref[e], 0)
    pos_ref[...] = pos


def _dispatch_plan(route, counts, tm):
    t = route.shape[0]
    n = 2 * t
    nb = (n + N_EXPERTS * (tm - 1)) // tm + 1
    cnt = counts[0, :N_EXPERTS].astype(jnp.int32)
    padded = ((cnt + tm - 1) // tm) * tm
    pend = jnp.cumsum(padded)
    pstart = pend - padded
    seg = jnp.concatenate([pstart, pstart + cnt, pend]).astype(jnp.int32)
    blk_start = jnp.arange(nb, dtype=jnp.int32) * tm
    blk_e = jnp.minimum(jnp.sum((blk_start[:, None] >= pend[None, :]).astype(jnp.int32), axis=1), N_EXPERTS - 1)
    n_used = (pend[-1:] // tm).astype(jnp.int32)
    code = route[:, :2].astype(jnp.int32).reshape(n // LANES, LANES)
    whole = pl.BlockSpec(code.shape, lambda i, s: (0, 0))
    pos = pl.pallas_call(
        _position_kernel,
        grid_spec=pltpu.PrefetchScalarGridSpec(num_scalar_prefetch=1, grid=(1,), in_specs=[whole], out_specs=whole),
        out_shape=jax.ShapeDtypeStruct(code.shape, jnp.int32),
        compiler_params=_cparams(("arbitrary",)), name="slot_positions",
    )(seg, code)
    return nb * tm, blk_e.astype(jnp.int32), n_used, pos.reshape(n), seg


def kernel(x, meta_tokens, norm_mix_g, w_in, ssm_lambda_re, ssm_lambda_im, ssm_log_dt, ssm_b_re, ssm_b_im,
           ssm_c_re, ssm_c_im, ssm_d, ssm_glu_w, ssm_glu_b, w_branch_attn, w_branch_ssm, w_out, norm_ffn_g,
           router_group_w, router_group_b, router_expert_w, router_expert_b, expert_w1, expert_w3, expert_w2,
           norm_final_g):
    b, l, d = x.shape
    depth = w_in.shape[0]
    aw = w_branch_attn.shape[1]
    sw = w_branch_ssm.shape[1]
    heads = aw // HEAD_DIM
    t = b * l
    pad = (-N_META) % Q_BLOCK
    assert depth == 1 and l % Q_BLOCK == 0 and pad + N_META == Q_BLOCK and SSM_CHUNK == Q_BLOCK
    layer = 0
    h_real = x.reshape(t, d)
    h_meta = meta_tokens.astype(x.dtype)
    tm_tok = 512

    wi = w_in[layer].astype(BF16)
    o = 3 * aw + sw
    ws = (wi[:, :aw], wi[:, aw:2 * aw].T, wi[:, 2 * aw:3 * aw], wi[:, 3 * aw:o].T, wi[:, o:o + d], wi[:, o + d:])
    g_mix = norm_mix_g[layer].reshape(1, d)
    q, kt, v, ut, ga, gb = _inproj(h_real, g_mix, ws, tm_tok)
    _, kt_m, v_m, ut_m, _, _ = _inproj(h_meta, g_mix, ws, N_META)
    kt_meta = jnp.concatenate([jnp.zeros((aw, pad), BF16), kt_m], axis=1)
    v_meta = jnp.concatenate([jnp.zeros((pad, aw), BF16), v_m], axis=0)
    ut_meta = jnp.concatenate([jnp.zeros((sw, pad), BF16), ut_m], axis=1)

    attn = _attention(q, kt, v, kt_meta, v_meta, b, heads, pad)

    ssm_params = (ssm_lambda_re[layer], ssm_lambda_im[layer], ssm_log_dt[layer], ssm_b_re[layer], ssm_b_im[layer],
                  ssm_c_re[layer], ssm_c_im[layer], ssm_d[layer])
    yt = _ssm(ut, ut_meta, ssm_params, b)

    wr = jnp.zeros((d, LANES), F32)
    wr = wr.at[:, :N_EXPERTS].set(router_expert_w[layer]).at[:, N_EXPERTS:N_EXPERTS + MOE_GROUPS].set(router_group_w[layer])
    br = jnp.zeros((1, LANES), F32)
    br = br.at[0, :N_EXPERTS].set(router_expert_b[layer]).at[0, N_EXPERTS:N_EXPERTS + MOE_GROUPS].set(router_group_b[layer])
    h_mid, hn, route, counts = _merge(
        h_real, attn, yt, ga, gb, ssm_glu_w[layer].T.astype(BF16), ssm_glu_b[layer].reshape(sw, 1).astype(F32),
        w_branch_attn[layer].astype(BF16), w_branch_ssm[layer].astype(BF16), w_out[layer].astype(BF16),
        norm_ffn_g[layer].reshape(1, d), wr, br, tm_tok)

    cap, blk_e, n_used, pos, seg = _dispatch_plan(route, counts, MOE_TILE)
    xs = _dispatch(hn, pos, seg, cap, ROW_DMA_TILE)
    yb = _experts(xs, blk_e, n_used, expert_w1[layer], expert_w3[layer], expert_w2[layer], MOE_TILE)
    out = _combine(pos, h_mid, route, norm_final_g.reshape(1, d), yb, ROW_DMA_TILE)
    return out.reshape(b, l, d)
```

```python
import functools

import jax
import jax.numpy as jnp
from jax import lax
from jax.experimental import pallas as pl
from jax.experimental.pallas import tpu as pltpu

F32 = jnp.float32
BF16 = jnp.bfloat16

N_META = 16
Q_BLOCK = 128
HEAD_DIM = 64
SSM_GROUP_CH = 16
SSM_STATE = 64
MOE_GROUPS = 4
EXPERTS_PER_GROUP = 8
N_EXPERTS = MOE_GROUPS * EXPERTS_PER_GROUP
RMS_EPS = 1e-6

LANES = 128
SUBLANES = 8
SSM_CHUNK = LANES
MOE_TILE = 512
ROW_DMA_TILE = 512
STICK_CUTOFF = 104.0
VMEM_LIMIT = 52 * 1024 * 1024

_NT = (((1,), (1,)), ((), ()))
_TN = (((0,), (0,)), ((), ()))


def _cparams(sem):
    return pltpu.CompilerParams(dimension_semantics=sem, vmem_limit_bytes=VMEM_LIMIT)


def _inproj_kernel(x_ref, g_ref, wq_ref, wkt_ref, wv_ref, wut_ref, wga_ref, wgb_ref,
                   q_ref, kt_ref, v_ref, ut_ref, ga_ref, gb_ref):
    x = x_ref[...]
    ms = jnp.mean(x * x, axis=-1, keepdims=True)
    xn = (x * lax.rsqrt(ms + RMS_EPS) * g_ref[...]).astype(BF16)
    q_ref[...] = (jnp.dot(xn, wq_ref[...], preferred_element_type=F32) * (HEAD_DIM ** -0.5)).astype(BF16)
    kt_ref[...] = lax.dot_general(wkt_ref[...], xn, _NT, preferred_element_type=F32).astype(BF16)
    v_ref[...] = jnp.dot(xn, wv_ref[...], preferred_element_type=F32).astype(BF16)
    ut_ref[...] = lax.dot_general(wut_ref[...], xn, _NT, preferred_element_type=F32).astype(BF16)
    ga_ref[...] = jax.nn.sigmoid(jnp.dot(xn, wga_ref[...], preferred_element_type=F32)).astype(BF16)
    gb_ref[...] = jax.nn.sigmoid(jnp.dot(xn, wgb_ref[...], preferred_element_type=F32)).astype(BF16)


def _inproj(x2, g, ws, tm):
    m, d = x2.shape
    aw, sw = ws[0].shape[1], ws[3].shape[0]
    row = lambda n: pl.BlockSpec((tm, n), lambda i: (i, 0))
    col = lambda n: pl.BlockSpec((n, tm), lambda i: (0, i))
    full = lambda a: pl.BlockSpec(a.shape, lambda i: (0, 0))
    sds = jax.ShapeDtypeStruct
    return pl.pallas_call(
        _inproj_kernel,
        grid=(m // tm,),
        in_specs=[row(d), full(g)] + [full(w) for w in ws],
        out_specs=[row(aw), col(aw), row(aw), col(sw), row(d), row(d)],
        out_shape=[sds((m, aw), BF16), sds((aw, m), BF16), sds((m, aw), BF16), sds((sw, m), BF16),
                   sds((m, d), BF16), sds((m, d), BF16)],
        compiler_params=_cparams(("parallel",)),
        name="inproj",
    )(x2, g, *ws)


def _attn_kernel(q_ref, kt_ref, v_ref, ktm_ref, vm_ref, tri_ref, o_ref, qm_ref, carry_ref, acc_ref, *, heads, pad):
    i = pl.program_id(1)
    pairs = heads // 2
    carry_ref[...] = jnp.zeros_like(carry_ref)
    acc_ref[...] = jnp.zeros_like(acc_ref)
    lane = lax.broadcasted_iota(jnp.int32, (Q_BLOCK, LANES), 1)
    row = lax.broadcasted_iota(jnp.int32, (Q_BLOCK, LANES), 0)
    low_half = lane < HEAD_DIM
    rows = lambda h: slice(h * Q_BLOCK, (h + 1) * Q_BLOCK)
    for p in range(pairs):
        qp = q_ref[0, :, p * LANES:(p + 1) * LANES]
        qm_ref[rows(2 * p)] = jnp.where(low_half, qp, jnp.zeros_like(qp))
        qm_ref[rows(2 * p + 1)] = jnp.where(low_half, jnp.zeros_like(qp), qp)

    def block_step(get_kt, get_v, valid):
        kts = [get_kt(p) for p in range(pairs)]
        vs = [get_v(p) for p in range(pairs)]
        zs = [jnp.dot(qm_ref[rows(h)], kts[h // 2], preferred_element_type=F32) for h in range(heads)]
        drops = []
        for h in range(heads):
            z = zs[h].astype(BF16)
            drop = jnp.maximum(z, 0.0) + jnp.log(1.0 + jnp.exp(-jnp.abs(z)))
            if valid is not None:
                drop = jnp.where(valid, drop, jnp.zeros_like(drop))
            drops.append(drop)
        sums = [jnp.dot(drops[h], tri_ref[...], preferred_element_type=F32) for h in range(heads)]
        low = None
        for h in range(heads):
            carry = carry_ref[rows(h)]
            w = jnp.exp(zs[h] - (sums[h][:, :LANES] + carry))
            if valid is not None:
                w = jnp.where(valid, w, 0.0)
            acc_ref[rows(h)] += jnp.dot(w.astype(BF16), vs[h // 2], preferred_element_type=F32)
            carry = carry + sums[h][:, LANES:]
            carry_ref[rows(h)] = carry
            low = carry if low is None else jnp.minimum(low, carry)
        return jnp.min(low)

    off_d = pl.multiple_of(i * Q_BLOCK, Q_BLOCK)
    low0 = block_step(lambda p: kt_ref[p * LANES:(p + 1) * LANES, pl.ds(off_d, Q_BLOCK)],
                      lambda p: v_ref[0, pl.ds(off_d, Q_BLOCK), p * LANES:(p + 1) * LANES],
                      lane < row)

    def cond(state):
        kb, low = state
        return jnp.logical_and(kb >= 0, low < STICK_CUTOFF)

    def body(state):
        kb, _ = state
        off = pl.multiple_of(kb * Q_BLOCK, Q_BLOCK)
        low = block_step(lambda p: kt_ref[p * LANES:(p + 1) * LANES, pl.ds(off, Q_BLOCK)],
                         lambda p: v_ref[0, pl.ds(off, Q_BLOCK), p * LANES:(p + 1) * LANES],
                         None)
        return kb - 1, low

    _, low1 = lax.while_loop(cond, body, (i - 1, low0))

    @pl.when(low1 < STICK_CUTOFF)
    def _():
        block_step(lambda p: ktm_ref[p * LANES:(p + 1) * LANES, :],
                   lambda p: vm_ref[:, p * LANES:(p + 1) * LANES],
                   lane >= pad)

    for p in range(pairs):
        even = acc_ref[2 * p * Q_BLOCK:(2 * p + 1) * Q_BLOCK]
        odd = acc_ref[(2 * p + 1) * Q_BLOCK:(2 * p + 2) * Q_BLOCK]
        o_ref[0, :, p * LANES:(p + 1) * LANES] = jnp.where(low_half, even, odd).astype(o_ref.dtype)


def _attention(q, kt, v, kt_meta, v_meta, b, heads, pad):
    t, aw = q.shape
    l = t // b
    r = lax.broadcasted_iota(jnp.int32, (LANES, 2 * LANES), 0)
    c = lax.broadcasted_iota(jnp.int32, (LANES, 2 * LANES), 1)
    tri = jnp.where(jnp.logical_or(c >= LANES, r >= c), 1.0, 0.0).astype(BF16)
    once = pl.Buffered(1)
    out = pl.pallas_call(
        functools.partial(_attn_kernel, heads=heads, pad=pad),
        grid=(b, l // Q_BLOCK),
        in_specs=[
            pl.BlockSpec((1, Q_BLOCK, aw), lambda bi, i: (bi, i, 0)),
            pl.BlockSpec((aw, l), lambda bi, i: (0, bi), pipeline_mode=once),
            pl.BlockSpec((1, l, aw), lambda bi, i: (bi, 0, 0), pipeline_mode=once),
            pl.BlockSpec((aw, Q_BLOCK), lambda bi, i: (0, 0)),
            pl.BlockSpec((Q_BLOCK, aw), lambda bi, i: (0, 0)),
            pl.BlockSpec((LANES, 2 * LANES), lambda bi, i: (0, 0)),
        ],
        out_specs=pl.BlockSpec((1, Q_BLOCK, aw), lambda bi, i: (bi, i, 0)),
        out_shape=jax.ShapeDtypeStruct((b, l, aw), BF16),
        scratch_shapes=[pltpu.VMEM((heads * Q_BLOCK, LANES), BF16), pltpu.VMEM((heads * Q_BLOCK, LANES), F32),
                        pltpu.VMEM((heads * Q_BLOCK, LANES), F32)],
        compiler_params=_cparams(("parallel", "arbitrary")),
        name="stick_attn",
    )(q.reshape(b, l, aw), kt, v.reshape(b, l, aw), kt_meta, v_meta, tri)
    return out.reshape(t, aw)


def _ssm_tables(lam_re, lam_im, log_dt, b_re, b_im, c_re, c_im, d_skip):
    hp = lax.Precision.HIGHEST
    tc = SSM_CHUNK
    g, p = lam_re.shape
    ch = b_re.shape[-1]
    dt = jnp.exp(log_dt)[:, None]
    mag = jnp.exp(lam_re * dt)
    lb_re, lb_im = mag * jnp.cos(lam_im * dt), mag * jnp.sin(lam_im * dt)
    nr, ni = lb_re - 1.0, lb_im
    den = lam_re * lam_re + lam_im * lam_im
    f_re = (nr * lam_re + ni * lam_im) / den
    f_im = (ni * lam_re - nr * lam_im) / den
    bb_re = f_re[:, :, None] * b_re - f_im[:, :, None] * b_im
    bb_im = f_re[:, :, None] * b_im + f_im[:, :, None] * b_re
    k = jnp.arange(tc + 1, dtype=F32)
    pw_mag = jnp.exp((lam_re * dt)[:, :, None] * k)
    pw_ang = (lam_im * dt)[:, :, None] * k
    pw_re, pw_im = pw_mag * jnp.cos(pw_ang), pw_mag * jnp.sin(pw_ang)
    dec_re = pw_re[:, :, tc].reshape(g // 2, 1, 2 * p)
    dec_im = pw_im[:, :, tc].reshape(g // 2, 1, 2 * p)
    ct_re, ct_im = c_re.transpose(0, 2, 1)[..., None], c_im.transpose(0, 2, 1)[..., None]
    cl_re = ct_re * pw_re[:, :, None, :] - ct_im * pw_im[:, :, None, :]
    cl_im = ct_re * pw_im[:, :, None, :] + ct_im * pw_re[:, :, None, :]
    kern = (jnp.einsum('gpot,gpi->giot', cl_re[..., :tc], bb_re, precision=hp)
            - jnp.einsum('gpot,gpi->giot', cl_im[..., :tc], bb_im, precision=hp))
    kern = kern.at[:, :, :, 0].add(jnp.eye(ch, dtype=F32)[None] * d_skip.reshape(g, ch, 1))
    powers = jnp.stack([pw_re[:, :, tc - 1::-1], pw_im[:, :, tc - 1::-1], pw_re[:, :, 1:], pw_im[:, :, 1:]], axis=1)
    coeffs = jnp.stack([bb_re, bb_im, ct_re[..., 0], ct_im[..., 0]], axis=1)
    return kern, powers, coeffs, dec_re, dec_im


def _ssm_kernel(u_ref, um_ref, k_ref, pw_ref, cf_ref, dre_ref, dim_ref, y_ref,
                toep_ref, sre_ref, sim_ref, xre_ref, xim_ref, *, batch):
    ch, tc = SSM_GROUP_CH, SSM_CHUNK
    nch = u_ref.shape[1]
    per_b = nch // batch
    p = pw_ref.shape[2]

    def chunk_rows(ref, gl):
        return jnp.concatenate([ref[gl * ch + ci] for ci in range(ch)], axis=1)

    u = [chunk_rows(u_ref, gl) for gl in range(2)]
    um = [jnp.concatenate([jnp.broadcast_to(um_ref[gl * ch + ci:gl * ch + ci + 1, :], (8, tc)) for ci in range(ch)], axis=1)
          for gl in range(2)]

    def expand(gl, which):
        a_re, a_im = pw_ref[gl, 2 * which], pw_ref[gl, 2 * which + 1]
        c_re, c_im = cf_ref[gl, 2 * which], cf_ref[gl, 2 * which + 1]
        m_re, m_im = [], []
        for c in range(ch):
            b_re = jnp.broadcast_to(c_re[:, c:c + 1], (p, tc))
            b_im = jnp.broadcast_to(c_im[:, c:c + 1], (p, tc))
            m_re.append(a_re * b_re - a_im * b_im)
            m_im.append(a_re * b_im + a_im * b_re)
        zeros = jnp.zeros((p, ch * tc), BF16)

        def place(parts):
            m = jnp.concatenate(parts, axis=1).astype(BF16)
            return jnp.concatenate([m, zeros] if gl == 0 else [zeros, m], axis=0)
        return place(m_re), place(m_im)

    to_st = [expand(gl, 0) for gl in range(2)]

    def to_state(lhs, part):
        return (lax.dot_general(lhs[0], to_st[0][part], _NT, preferred_element_type=F32)
                + lax.dot_general(lhs[1], to_st[1][part], _NT, preferred_element_type=F32))

    sre_ref[...] = to_state(u, 0)
    sim_ref[...] = to_state(u, 1)
    x0_re = to_state(um, 0)[0:1]
    x0_im = to_state(um, 1)[0:1]
    d_re, d_im = dre_ref[0], dim_ref[0]

    def step(c, xs):
        new = []
        for bi in range(batch):
            x_re, x_im = xs[2 * bi], xs[2 * bi + 1]
            r = bi * per_b + c
            xre_ref[pl.ds(r, 1), :] = x_re
            xim_ref[pl.ds(r, 1), :] = x_im
            new.append(d_re * x_re - d_im * x_im + sre_ref[pl.ds(r, 1), :])
            new.append(d_re * x_im + d_im * x_re + sim_ref[pl.ds(r, 1), :])
        return tuple(new)

    lax.fori_loop(0, per_b, step, (x0_re, x0_im) * batch)
    xs_re = xre_ref[...].astype(BF16)
    xs_im = xim_ref[...].astype(BF16)

    lane = lax.broadcasted_iota(jnp.int32, (tc, tc), 1)
    row = lax.broadcasted_iota(jnp.int32, (tc, tc), 0)
    causal = lane >= row
    for gl in range(2):
        def fill(ci, carry):
            for co in range(ch):
                taps = jnp.broadcast_to(k_ref[gl, ci, pl.ds(co, 1), :], (tc, tc))
                blk = pltpu.roll(taps, 0, 1, stride=1, stride_axis=0)
                blk = jnp.where(causal, blk, 0.0).astype(BF16)
                toep_ref[pl.ds(pl.multiple_of(ci * tc, tc), tc), co * tc:(co + 1) * tc] = blk
            return carry
        for ci_static in range(ch):
            fill(ci_static, 0)
        from_re, from_im = expand(gl, 1)
        y = jnp.dot(u[gl], toep_ref[...], preferred_element_type=F32)
        y = y + jnp.dot(xs_re, from_re, preferred_element_type=F32)
        y = y - jnp.dot(xs_im, from_im, preferred_element_type=F32)
        y = jax.nn.gelu(y).astype(y_ref.dtype)
        for co in range(ch):
            y_ref[gl * ch + co] = y[:, co * tc:(co + 1) * tc]


def _ssm(ut, ut_meta, params, batch):
    s, t = ut.shape
    tc, ch, p = SSM_CHUNK, SSM_GROUP_CH, SSM_STATE
    g = s // ch
    nch = t // tc
    kern, powers, coeffs, dec_re, dec_im = _ssm_tables(*params)
    u3 = ut.reshape(s, nch, tc)
    y3 = pl.pallas_call(
        functools.partial(_ssm_kernel, batch=batch),
        grid=(g // 2,),
        in_specs=[
            pl.BlockSpec((2 * ch, nch, tc), lambda j: (j, 0, 0)),
            pl.BlockSpec((2 * ch, tc), lambda j: (j, 0)),
            pl.BlockSpec((2, ch, ch, tc), lambda j: (j, 0, 0, 0)),
            pl.BlockSpec((2, 4, p, tc), lambda j: (j, 0, 0, 0)), pl.BlockSpec((2, 4, p, ch), lambda j: (j, 0, 0, 0)),
            pl.BlockSpec((1, 1, 2 * p), lambda j: (j, 0, 0)), pl.BlockSpec((1, 1, 2 * p), lambda j: (j, 0, 0)),
        ],
        out_specs=pl.BlockSpec((2 * ch, nch, tc), lambda j: (j, 0, 0)),
        out_shape=jax.ShapeDtypeStruct((s, nch, tc), BF16),
        scratch_shapes=[pltpu.VMEM((ch * tc, ch * tc), BF16)] + [pltpu.VMEM((nch, 2 * p), F32)] * 4,
        compiler_params=_cparams(("parallel",)), name="ssm_chunked",
    )(u3, ut_meta, kern, powers, coeffs, dec_re, dec_im)
    return y3.reshape(s, t)


RANK_BITS = 16


def _pack_halves(x):
    n = x.shape[1] // 2
    as_bits = lambda v: lax.bitcast_convert_type(v.astype(BF16).astype(F32), jnp.uint32)
    return jnp.bitwise_or(lax.shift_right_logical(as_bits(x[:, :n]), jnp.uint32(16)),
                          jnp.bitwise_and(as_bits(x[:, n:]), jnp.uint32(0xFFFF0000)))


def _unpack_halves(w):
    lo = lax.bitcast_convert_type(lax.shift_left(w, jnp.uint32(16)), F32)
    hi = lax.bitcast_convert_type(jnp.bitwise_and(w, jnp.uint32(0xFFFF0000)), F32)
    return jnp.concatenate([lo, hi], axis=1)


def _merge_kernel(x_ref, a_ref, yt_ref, ga_ref, gb_ref, gwt_ref, gbias_ref, wpa_ref, wpb_ref, wo_ref, gn_ref,
                  wr_ref, br_ref, ltri_ref, h_ref, hn_ref, route_ref, count_ref, run_ref):
    @pl.when(pl.program_id(0) == 0)
    def _():
        run_ref[...] = jnp.zeros_like(run_ref)

    yt = yt_ref[...]
    gate = jax.nn.sigmoid(jnp.dot(gwt_ref[...], yt, preferred_element_type=F32) + gbias_ref[...])
    ssm = (yt.astype(F32) * gate).T.astype(BF16)
    merged = (ga_ref[...].astype(F32) * jnp.dot(a_ref[...], wpa_ref[...], preferred_element_type=F32)
              + gb_ref[...].astype(F32) * jnp.dot(ssm, wpb_ref[...], preferred_element_type=F32))
    h = x_ref[...] + jnp.dot(merged.astype(BF16), wo_ref[...], preferred_element_type=F32)
    h_ref[...] = h
    hn = h * lax.rsqrt(jnp.mean(h * h, axis=-1, keepdims=True) + RMS_EPS) * gn_ref[...]
    hn_ref[...] = _pack_halves(hn)
    hn_hi = hn.astype(BF16)
    hn_lo = (hn - hn_hi.astype(F32)).astype(BF16)
    logits = (jnp.dot(hn_hi, wr_ref[0], preferred_element_type=F32)
              + jnp.dot(hn_hi, wr_ref[1], preferred_element_type=F32)
              + jnp.dot(hn_lo, wr_ref[0], preferred_element_type=F32)) + br_ref[...]
    lane = lax.broadcasted_iota(jnp.int32, logits.shape, 1)
    neg = jnp.full_like(logits, -jnp.inf)
    big = jnp.int32(LANES)
    is_grp = jnp.logical_and(lane >= N_EXPERTS, lane < N_EXPERTS + MOE_GROUPS)
    glog = jnp.where(is_grp, logits, neg)
    gmax = jnp.max(glog, axis=-1, keepdims=True)
    g_top = jnp.min(jnp.where(glog == gmax, lane, big), axis=-1, keepdims=True) - N_EXPERTS
    p_top = 1.0 / jnp.sum(jnp.exp(glog - gmax), axis=-1, keepdims=True)
    in_grp = (lane // EXPERTS_PER_GROUP) == g_top
    elog = jnp.where(jnp.logical_and(in_grp, lane < N_EXPERTS), logits, neg)
    e1 = jnp.max(elog, axis=-1, keepdims=True)
    i1 = jnp.min(jnp.where(elog == e1, lane, big), axis=-1, keepdims=True)
    elog2 = jnp.where(lane == i1, neg, elog)
    e2 = jnp.max(elog2, axis=-1, keepdims=True)
    i2 = jnp.min(jnp.where(elog2 == e2, lane, big), axis=-1, keepdims=True)
    t = jnp.exp(e2 - e1)
    w1 = p_top / (1.0 + t)
    w2 = p_top * t / (1.0 + t)
    hit1, hit2 = lane == i1, lane == i2
    chosen = jnp.where(jnp.logical_or(hit1, hit2), 1.0, 0.0)
    before = jnp.dot(ltri_ref[...], chosen.astype(BF16), preferred_element_type=F32) + run_ref[0:1, :]
    r1 = jnp.sum(jnp.where(hit1, before, 0.0), axis=-1, keepdims=True)
    r2 = jnp.sum(jnp.where(hit2, before, 0.0), axis=-1, keepdims=True)
    run_ref[...] = run_ref[...] + jnp.sum(chosen, axis=0, keepdims=True)
    count_ref[...] = run_ref[...]
    scale = float(2 ** RANK_BITS)
    route = jnp.where(lane == 0, i1.astype(F32) * scale + r1,
                      jnp.where(lane == 1, i2.astype(F32) * scale + r2,
                                jnp.where(lane == 2, w1, jnp.where(lane == 3, w2, 0.0))))
    route_ref[...] = route


def _merge(x2, attn, yt, ga, gb, glu_wt, glu_b, wpa, wpb, wo, gn, wr, br, tm):
    m, d = x2.shape
    aw, sw = attn.shape[1], yt.shape[0]
    row = lambda n: pl.BlockSpec((tm, n), lambda i: (i, 0))
    full = lambda a: pl.BlockSpec(a.shape, lambda i: (0,) * a.ndim)
    wr_hi = wr.astype(BF16)
    wr = jnp.stack([wr_hi, (wr - wr_hi.astype(F32)).astype(BF16)])
    ltri = jnp.where(lax.broadcasted_iota(jnp.int32, (tm, tm), 0) > lax.broadcasted_iota(jnp.int32, (tm, tm), 1),
                     1.0, 0.0).astype(BF16)
    ws = (glu_wt, glu_b, wpa, wpb, wo, gn, wr, br, ltri)
    return pl.pallas_call(
        _merge_kernel, grid=(m // tm,),
        in_specs=[row(d), row(aw), pl.BlockSpec((sw, tm), lambda i: (0, i)), row(d), row(d)] + [full(a) for a in ws],
        out_specs=[row(d), row(d // 2), row(LANES), pl.BlockSpec((8, LANES), lambda i: (0, 0))],
        out_shape=[jax.ShapeDtypeStruct((m, d), F32), jax.ShapeDtypeStruct((m, d // 2), jnp.uint32),
                   jax.ShapeDtypeStruct((m, LANES), F32), jax.ShapeDtypeStruct((8, LANES), F32)],
        scratch_shapes=[pltpu.VMEM((8, LANES), F32)],
        compiler_params=_cparams(("arbitrary",)), name="merge_router",
    )(x2, attn, yt, ga, gb, *ws)


def _start_row_gather(src_hbm, row_of, base, stride, buf, row0, sem, n):
    for r in range(n):
        row = row_of(base + stride * r)
        pltpu.make_async_copy(src_hbm.at[pl.ds(row, 1)], buf.at[pl.ds(row0 + r, 1)], sem).start(priority=r % 2)


def _wait_rows(src_hbm, buf, sem):
    pltpu.make_async_copy(src_hbm.at[pl.ds(0, buf.shape[0])], buf, sem).wait()


def _dispatch_kernel(pos_ref, seg_ref, hn_ref, xs_hbm, zero_ref, sem, zsem, *, tm):
    i = pl.program_id(0)

    @pl.when(i == 0)
    def _():
        zero_ref[...] = jnp.zeros_like(zero_ref)
        zrows = zero_ref.shape[0]

        def zero_run(lo, hi, act):
            def copy(start, n):
                return pltpu.make_async_copy(zero_ref.at[pl.ds(0, n)], xs_hbm.at[pl.ds(start, n)], zsem)
            head = jnp.minimum(jnp.bitwise_and(-lo, SUBLANES - 1), hi - lo)
            lax.fori_loop(0, head, lambda k, c: (act(copy(lo + k, 1)), c)[1], 0)
            lo = lo + head
            n_full = (hi - lo) // zrows
            lax.fori_loop(0, n_full,
                          lambda k, c: (act(copy(pl.multiple_of(lo + k * zrows, SUBLANES), zrows)), c)[1], 0)
            rest = (hi - lo) - n_full * zrows
            start = lo + n_full * zrows
            bit = zrows // 2
            while bit >= SUBLANES:
                pl.when(jnp.bitwise_and(rest, bit) != 0)(
                    functools.partial(lambda st, n: act(copy(pl.multiple_of(st, SUBLANES), n)), start, bit))
                start = start + jnp.bitwise_and(rest, bit)
                bit //= 2

        def each_gap(act):
            def run(e, carry):
                zero_run(seg_ref[N_EXPERTS + e], seg_ref[2 * N_EXPERTS + e], act)
                return carry
            lax.fori_loop(0, N_EXPERTS, run, 0)
            zero_run(seg_ref[3 * N_EXPERTS - 1], xs_hbm.shape[0], act)
        each_gap(lambda cp: cp.start())
        each_gap(lambda cp: cp.wait())

    for r in range(tm):
        for c in range(2):
            slot = pos_ref[2 * (i * tm + r) + c]
            pltpu.make_async_copy(hn_ref.at[pl.ds(r, 1)], xs_hbm.at[pl.ds(slot, 1)], sem).start(priority=c)
    pltpu.make_async_copy(xs_hbm.at[pl.ds(0, 2 * tm)], xs_hbm.at[pl.ds(0, 2 * tm)], sem).wait()


def _dispatch(hn, pos, seg, cap, tm):
    t, d = hn.shape
    return pl.pallas_call(
        functools.partial(_dispatch_kernel, tm=tm),
        grid_spec=pltpu.PrefetchScalarGridSpec(
            num_scalar_prefetch=2, grid=(t // tm,),
            in_specs=[pl.BlockSpec((tm, d), lambda i, c, s: (i, 0))],
            out_specs=pl.BlockSpec(memory_space=pl.ANY),
            scratch_shapes=[pltpu.VMEM((MOE_TILE, d), hn.dtype), pltpu.SemaphoreType.DMA(()),
                            pltpu.SemaphoreType.DMA(())],
        ),
        out_shape=jax.ShapeDtypeStruct((cap, d), hn.dtype),
        compiler_params=_cparams(("arbitrary",)), name="moe_dispatch",
    )(pos, seg, hn)


def _expert_kernel(blk_e_ref, n_used_ref, x_ref, w1_ref, w3_ref, w2_ref, y_ref):
    j = pl.program_id(0)

    @pl.when(j < n_used_ref[0])
    def _():
        x = _unpack_halves(x_ref[...]).astype(BF16)
        a = jnp.dot(x, w1_ref[0].astype(BF16), preferred_element_type=F32)
        b = jnp.dot(x, w3_ref[0].astype(BF16), preferred_element_type=F32)
        hdn = (a * jax.nn.sigmoid(a) * b).astype(BF16)
        y_ref[...] = _pack_halves(jnp.dot(hdn, w2_ref[0].astype(BF16), preferred_element_type=F32))

    @pl.when(j >= n_used_ref[0])
    def _():
        y_ref[...] = jnp.zeros_like(y_ref)


def _experts(xs, blk_e, n_used, w1, w3, w2, tm):
    cap, dp = xs.shape
    e, d, ff = w1.shape
    nb = cap // tm
    wspec = lambda r, c: pl.BlockSpec((1, r, c), lambda j, be, nu: (be[j], 0, 0))
    xrow = lambda j, be, nu: (jnp.minimum(j, nu[0] - 1), 0)
    return pl.pallas_call(
        _expert_kernel,
        grid_spec=pltpu.PrefetchScalarGridSpec(
            num_scalar_prefetch=2, grid=(nb,),
            in_specs=[pl.BlockSpec((tm, dp), xrow), wspec(d, ff), wspec(d, ff), wspec(ff, d)],
            out_specs=pl.BlockSpec((tm, dp), lambda j, be, nu: (j, 0)),
        ),
        out_shape=jax.ShapeDtypeStruct((cap, dp), xs.dtype),
        compiler_params=_cparams(("arbitrary",)), name="expert_ffn",
    )(blk_e, n_used, xs, w1, w3, w2)


def _combine_kernel(pos_ref, h_ref, route_ref, gf_ref, yb_hbm, o_ref, ybuf, sems, *, tm):
    i = pl.program_id(0)
    last = pl.num_programs(0) - 1
    slot = lax.rem(i, 2)

    def start(tile, s):
        for c in range(2):
            _start_row_gather(yb_hbm, lambda k: pos_ref[k], tile * 2 * tm + c, 2, ybuf.at[s], c * tm, sems.at[s], tm)

    @pl.when(i == 0)
    def _():
        start(0, 0)

    _wait_rows(yb_hbm, ybuf.at[slot], sems.at[slot])
    nxt = jnp.minimum(i + 1, last)
    for s in range(2):
        @pl.when(slot == s)
        def _():
            start(nxt, 1 - s)
            route = route_ref[...]
            w1 = route[:, 2:3]
            w2 = route[:, 3:4]
            h = h_ref[...] + w1 * _unpack_halves(ybuf[s, :tm, :]) + w2 * _unpack_halves(ybuf[s, tm:, :])
            o_ref[...] = h * lax.rsqrt(jnp.mean(h * h, axis=-1, keepdims=True) + RMS_EPS) * gf_ref[...]

    @pl.when(i == last)
    def _():
        _wait_rows(yb_hbm, ybuf.at[1 - slot], sems.at[1 - slot])


def _combine(pos, h, route, gf, yb, tm):
    t, d = h.shape
    return pl.pallas_call(
        functools.partial(_combine_kernel, tm=tm),
        grid_spec=pltpu.PrefetchScalarGridSpec(
            num_scalar_prefetch=1, grid=(t // tm,),
            in_specs=[pl.BlockSpec((tm, d), lambda i, p: (i, 0)), pl.BlockSpec((tm, LANES), lambda i, p: (i, 0)),
                      pl.BlockSpec((1, d), lambda i, p: (0, 0)), pl.BlockSpec(memory_space=pl.ANY)],
            out_specs=pl.BlockSpec((tm, d), lambda i, p: (i, 0)),
            scratch_shapes=[pltpu.VMEM((2, 2 * tm) + yb.shape[1:], yb.dtype), pltpu.SemaphoreType.DMA((2,))],
        ),
        out_shape=jax.ShapeDtypeStruct((t, d), F32),
        compiler_params=_cparams(("arbitrary",)), name="moe_combine",
    )(pos, h, route, gf, yb)


def _position_kernel(seg_ref, code_ref, pos_ref):
    code = code_ref[...]
    expert = lax.shift_right_logical(code, RANK_BITS)
    pos = jnp.bitwise_and(code, (1 << RANK_BITS) - 1)
    for e in range(N_EXPERTS):
        pos = pos + jnp.where(expert == e, seg_ref[e], 0)
    pos_ref[...] = pos


def _dispatch_plan(route, counts, tm):
    t = route.shape[0]
    n = 2 * t
    nb = (n + N_EXPERTS * (tm - 1)) // tm + 1
    cnt = counts[0, :N_EXPERTS].astype(jnp.int32)
    padded = ((cnt + tm - 1) // tm) * tm
    pend = jnp.cumsum(padded)
    pstart = pend - padded
    seg = jnp.concatenate([pstart, pstart + cnt, pend]).astype(jnp.int32)
    blk_start = jnp.arange(nb, dtype=jnp.int32) * tm
    blk_e = jnp.minimum(jnp.sum((blk_start[:, None] >= pend[None, :]).astype(jnp.int32), axis=1), N_EXPERTS - 1)
    n_used = (pend[-1:] // tm).astype(jnp.int32)
    code = route[:, :2].astype(jnp.int32).reshape(n // LANES, LANES)
    whole = pl.BlockSpec(code.shape, lambda i, s: (0, 0))
    pos = pl.pallas_call(
        _position_kernel,
        grid_spec=pltpu.PrefetchScalarGridSpec(num_scalar_prefetch=1, grid=(1,), in_specs=[whole], out_specs=whole),
        out_shape=jax.ShapeDtypeStruct(code.shape, jnp.int32),
        compiler_params=_cparams(("arbitrary",)), name="slot_positions",
    )(seg, code)
    return nb * tm, blk_e.astype(jnp.int32), n_used, pos.reshape(n), seg


def kernel(x, meta_tokens, norm_mix_g, w_in, ssm_lambda_re, ssm_lambda_im, ssm_log_dt, ssm_b_re, ssm_b_im,
           ssm_c_re, ssm_c_im, ssm_d, ssm_glu_w, ssm_glu_b, w_branch_attn, w_branch_ssm, w_out, norm_ffn_g,
           router_group_w, router_group_b, router_expert_w, router_expert_b, expert_w1, expert_w3, expert_w2,
           norm_final_g):
    b, l, d = x.shape
    depth = w_in.shape[0]
    aw = w_branch_attn.shape[1]
    sw = w_branch_ssm.shape[1]
    heads = aw // HEAD_DIM
    t = b * l
    pad = (-N_META) % Q_BLOCK
    assert depth == 1 and l % Q_BLOCK == 0 and pad + N_META == Q_BLOCK and SSM_CHUNK == Q_BLOCK
    layer = 0
    h_real = x.reshape(t, d)
    h_meta = meta_tokens.astype(x.dtype)
    tm_tok = 512

    wi = w_in[layer].astype(BF16)
    o = 3 * aw + sw
    ws = (wi[:, :aw], wi[:, aw:2 * aw].T, wi[:, 2 * aw:3 * aw], wi[:, 3 * aw:o].T, wi[:, o:o + d], wi[:, o + d:])
    g_mix = norm_mix_g[layer].reshape(1, d)
    q, kt, v, ut, ga, gb = _inproj(h_real, g_mix, ws, tm_tok)
    _, kt_m, v_m, ut_m, _, _ = _inproj(h_meta, g_mix, ws, N_META)
    kt_meta = jnp.concatenate([jnp.zeros((aw, pad), BF16), kt_m], axis=1)
    v_meta = jnp.concatenate([jnp.zeros((pad, aw), BF16), v_m], axis=0)
    ut_meta = jnp.concatenate([jnp.zeros((sw, pad), BF16), ut_m], axis=1)

    attn = _attention(q, kt, v, kt_meta, v_meta, b, heads, pad)

    ssm_params = (ssm_lambda_re[layer], ssm_lambda_im[layer], ssm_log_dt[layer], ssm_b_re[layer], ssm_b_im[layer],
                  ssm_c_re[layer], ssm_c_im[layer], ssm_d[layer])
    yt = _ssm(ut, ut_meta, ssm_params, b)

    wr = jnp.zeros((d, LANES), F32)
    wr = wr.at[:, :N_EXPERTS].set(router_expert_w[layer]).at[:, N_EXPERTS:N_EXPERTS + MOE_GROUPS].set(router_group_w[layer])
    br = jnp.zeros((1, LANES), F32)
    br = br.at[0, :N_EXPERTS].set(router_expert_b[layer]).at[0, N_EXPERTS:N_EXPERTS + MOE_GROUPS].set(router_group_b[layer])
    h_mid, hn, route, counts = _merge(
        h_real, attn, yt, ga, gb, ssm_glu_w[layer].T.astype(BF16), ssm_glu_b[layer].reshape(sw, 1).astype(F32),
        w_branch_attn[layer].astype(BF16), w_branch_ssm[layer].astype(BF16), w_out[layer].astype(BF16),
        norm_ffn_g[layer].reshape(1, d), wr, br, tm_tok)

    cap, blk_e, n_used, pos, seg = _dispatch_plan(route, counts, MOE_TILE)
    xs = _dispatch(hn, pos, seg, cap, ROW_DMA_TILE)
    yb = _experts(xs, blk_e, n_used, expert_w1[layer], expert_w3[layer], expert_w2[layer], MOE_TILE)
    out = _combine(pos, h_mid, route, norm_final_g.reshape(1, d), yb, ROW_DMA_TILE)
    return out.reshape(b, l, d)
```
